```python
import jax, jax.numpy as jnp
from jax import lax
import numpy as np

D_MODEL = 1024
BATCH = 8
SEQ = 4096
DEPTH = 1
DEC_BATCH = 128
DEC_SEQ = 1
PAST_LEN = 16384
PAGE_SIZE = 128

CONV_WIDTH = 512
CONV_GROUPS = 8
CONV_K = 3
MLA_HEADS = 8
QK_NOPE = 64
QK_ROPE = 32
V_HEAD = 64
Q_LORA = 384
KV_LORA = 256
ROPE_THETA = 10000.0
Q_BLOCK = 128
MIX_WIDTH = CONV_WIDTH + MLA_HEADS * V_HEAD
IN_WIDTH = 3 * CONV_WIDTH + Q_LORA + KV_LORA + QK_ROPE
SPLITS = (CONV_WIDTH, 2 * CONV_WIDTH, 3 * CONV_WIDTH, 3 * CONV_WIDTH + Q_LORA, 3 * CONV_WIDTH + Q_LORA + KV_LORA)
MEM_TOKENS = 256
MEM_HEADS = 4
MEM_HEAD_DIM = D_MODEL // MEM_HEADS
N_EXPERTS = 32
TOP_K = 4
D_FF = D_MODEL
SWIGLU_LIMIT = 7.0
SWIGLU_ALPHA = 1.702
NORM_EPS = 1e-6

kernel_name = "hymba_conv_mla_moe_decoder_step"


def rmsnorm(x, g):
    xf = x.astype(jnp.float32)
    y = xf * lax.rsqrt(jnp.mean(xf * xf, axis=-1, keepdims=True) + NORM_EPS)
    return (y * g.astype(jnp.float32)).astype(x.dtype)


def rope(x, pos):
    half = x.shape[-1] // 2
    inv = ROPE_THETA ** (-jnp.arange(half, dtype=jnp.float32) / half)
    ang = pos.astype(jnp.float32)[:, None] * inv[None, :]
    cos = jnp.cos(ang)[None, :, None, :]
    sin = jnp.sin(ang)[None, :, None, :]
    xf = x.astype(jnp.float32)
    x1, x2 = xf[..., :half], xf[..., half:]
    return jnp.concatenate([x1 * cos - x2 * sin, x1 * sin + x2 * cos], axis=-1).astype(x.dtype)


def in_proj(a, w_in, g_q, w_uq, g_kv, pos):
    z = a @ w_in
    b_g, c_g, x_c, c_q, c_kv, k_r = jnp.split(z, SPLITS, axis=-1)
    bsz, t = a.shape[:2]
    q = (rmsnorm(c_q, g_q) @ w_uq).reshape(bsz, t, MLA_HEADS, QK_NOPE + QK_ROPE)
    q_nope = q[..., :QK_NOPE]
    q_rope = rope(q[..., QK_NOPE:], pos)
    latent = rmsnorm(c_kv, g_kv)
    k_rope = rope(k_r[:, :, None, :], pos)[:, :, 0, :]
    return b_g, c_g, x_c, q_nope, q_rope, latent, k_rope


def short_conv(b_g, c_g, x_c, prev, w_conv):
    u = c_g * x_c
    t = u.shape[1]
    up = jnp.concatenate([prev.astype(u.dtype), u], axis=1)
    y = w_conv[0] * up[:, 0:t]
    for j in range(1, CONV_K):
        y = y + w_conv[j] * up[:, j:j + t]
    return b_g * y, up[:, t:]


def mla_prompt(q_nope, q_rope, latent, k_rope, w_uk, w_uv):
    bsz, s_len = latent.shape[:2]
    k_nope = jnp.einsum('bsc,chd->bshd', latent, w_uk)
    v = jnp.einsum('bsc,chd->bshd', latent, w_uv)
    nb = s_len // Q_BLOCK
    scale = (QK_NOPE + QK_ROPE) ** -0.5
    k_pos = jnp.arange(s_len)

    def blocks(t):
        return jnp.moveaxis(t.reshape(bsz, nb, Q_BLOCK, *t.shape[2:]), 1, 0)

    def one_block(args):
        i, qn, qr = args
        s = jnp.einsum('bqhd,bkhd->bhqk', qn, k_nope) + jnp.einsum('bqhr,bkr->bhqk', qr, k_rope)
        s = s.astype(jnp.float32) * scale
        q_pos = i * Q_BLOCK + jnp.arange(Q_BLOCK)
        s = jnp.where(k_pos[None, :] <= q_pos[:, None], s, -jnp.inf)
        p = jax.nn.softmax(s, axis=-1).astype(v.dtype)
        return jnp.einsum('bhqk,bkhd->bqhd', p, v)

    o = lax.map(one_block, (jnp.arange(nb), blocks(q_nope), blocks(q_rope)))
    return jnp.moveaxis(o, 0, 1).reshape(bsz, s_len, MLA_HEADS * V_HEAD)


def mla_sample(q_nope, q_rope, latent, k_rope, ckv_past, kr_past, w_uk, w_uv):
    bsz, t = latent.shape[:2]
    scale = (QK_NOPE + QK_ROPE) ** -0.5
    q_lat = jnp.einsum('bthd,chd->bthc', q_nope, w_uk)
    s_past = jnp.einsum('bthc,bkc->bhtk', q_lat, ckv_past) + jnp.einsum('bthr,bkr->bhtk', q_rope, kr_past)
    s_new = jnp.einsum('bthc,buc->bhtu', q_lat, latent) + jnp.einsum('bthr,bur->bhtu', q_rope, k_rope)
    causal = jnp.arange(t)[None, :] <= jnp.arange(t)[:, None]
    s_new = jnp.where(causal, s_new.astype(jnp.float32) * scale, -jnp.inf)
    s = jnp.concatenate([s_past.astype(jnp.float32) * scale, s_new], axis=-1)
    p = jax.nn.softmax(s, axis=-1).astype(latent.dtype)
    n_past = ckv_past.shape[1]
    o_lat = (jnp.einsum('bhtk,bkc->bthc', p[..., :n_past], ckv_past)
             + jnp.einsum('bhtu,buc->bthc', p[..., n_past:], latent))
    o = jnp.einsum('bthc,chd->bthd', o_lat, w_uv)
    return o.reshape(bsz, t, MLA_HEADS * V_HEAD)


def merge_groups(y_conv, y_attn, g_c, g_a, w_out):
    return jnp.concatenate([rmsnorm(y_conv, g_c), rmsnorm(y_attn, g_a)], axis=-1) @ w_out


def memory_kv(mem, g_mem, w_mk, w_mv):
    m = rmsnorm(mem, g_mem)
    b, n = mem.shape[:2]
    k = (m @ w_mk).reshape(b, n, MEM_HEADS, MEM_HEAD_DIM)
    v = (m @ w_mv).reshape(b, n, MEM_HEADS, MEM_HEAD_DIM)
    return k, v


def cross_attn(a, k, v, w_mq, w_mo):
    b, t = a.shape[:2]
    q = (a @ w_mq).reshape(b, t, MEM_HEADS, MEM_HEAD_DIM)
    s = jnp.einsum('bthd,bmhd->bhtm', q, k).astype(jnp.float32) * MEM_HEAD_DIM ** -0.5
    p = jax.nn.softmax(s, axis=-1).astype(v.dtype)
    o = jnp.einsum('bhtm,bmhd->bthd', p, v).reshape(b, t, MEM_HEADS * MEM_HEAD_DIM)
    return o @ w_mo


def moe(a, w_router, b_router, w_gu, b_gu, w_down, b_down):
    shp = a.shape
    xt = a.reshape(-1, shp[-1])
    logits = (xt @ w_router).astype(jnp.float32) + b_router.astype(jnp.float32)
    top_v, top_i = lax.top_k(logits, TOP_K)
    top_w = jax.nn.softmax(top_v, axis=-1)
    gate = jnp.einsum('nk,nke->ne', top_w, jax.nn.one_hot(top_i, N_EXPERTS, dtype=jnp.float32))
    acc = jnp.zeros(xt.shape, jnp.float32)
    for e in range(N_EXPERTS):
        gu = xt @ w_gu[e] + b_gu[e]
        g = jnp.minimum(gu[:, :D_FF], SWIGLU_LIMIT)
        u = jnp.clip(gu[:, D_FF:], -SWIGLU_LIMIT, SWIGLU_LIMIT)
        hdn = (u + 1.0) * (g * jax.nn.sigmoid(SWIGLU_ALPHA * g))
        acc = acc + gate[:, e:e + 1] * (hdn @ w_down[e] + b_down[e]).astype(jnp.float32)
    return acc.astype(a.dtype).reshape(shp)


def setup_inputs(seed: int = 0) -> dict:
    key = jax.random.key(seed)
    ks = iter(jax.random.split(key, 48))
    f32 = jnp.float32

    def nrm(shape, scale=1.0):
        return scale * jax.random.normal(next(ks), shape, f32)

    def gain(shape):
        return 1.0 + 0.02 * jax.random.normal(next(ks), shape, f32)

    n_pages = PAST_LEN // PAGE_SIZE
    n_used = DEC_BATCH * n_pages
    n_pool = n_used + n_used // 4
    page_table = jax.random.permutation(next(ks), n_pool)[:n_used].reshape(DEC_BATCH, n_pages).astype(jnp.int32)
    L = DEPTH
    return {
        "x_prompt": nrm((BATCH, SEQ, D_MODEL)),
        "x_sample": nrm((DEC_BATCH, DEC_SEQ, D_MODEL)),
        "mem_prompt": nrm((BATCH, MEM_TOKENS, D_MODEL)),
        "cache_conv": nrm((L, DEC_BATCH, CONV_K - 1, CONV_WIDTH)),
        "cache_ckv": nrm((L, n_pool, PAGE_SIZE, KV_LORA)),
        "cache_krope": nrm((L, n_pool, PAGE_SIZE, QK_ROPE)),
        "cache_mem_k": nrm((L, DEC_BATCH, MEM_TOKENS, MEM_HEADS, MEM_HEAD_DIM)),
        "cache_mem_v": nrm((L, DEC_BATCH, MEM_TOKENS, MEM_HEADS, MEM_HEAD_DIM)),
        "page_table": page_table,
        "g_mix": gain((L, D_MODEL)),
        "w_in": nrm((L, D_MODEL, IN_WIDTH), D_MODEL ** -0.5),
        "w_conv": nrm((L, CONV_K, CONV_WIDTH), CONV_K ** -0.5),
        "g_q": gain((L, Q_LORA)),
        "w_uq": nrm((L, Q_LORA, MLA_HEADS * (QK_NOPE + QK_ROPE)), Q_LORA ** -0.5),
        "g_kv": gain((L, KV_LORA)),
        "w_uk": nrm((L, KV_LORA, MLA_HEADS, QK_NOPE), KV_LORA ** -0.5),
        "w_uv": nrm((L, KV_LORA, MLA_HEADS, V_HEAD), KV_LORA ** -0.5),
        "g_conv_out": gain((L, CONV_WIDTH)),
        "g_attn_out": gain((L, MLA_HEADS * V_HEAD)),
        "w_out": nrm((L, MIX_WIDTH, D_MODEL), MIX_WIDTH ** -0.5),
        "g_cross": gain((L, D_MODEL)),
        "g_mem": gain((L, D_MODEL)),
        "w_mq": nrm((L, D_MODEL, MEM_HEADS * MEM_HEAD_DIM), D_MODEL ** -0.5),
        "w_mk": nrm((L, D_MODEL, MEM_HEADS * MEM_HEAD_DIM), D_MODEL ** -0.5),
        "w_mv": nrm((L, D_MODEL, MEM_HEADS * MEM_HEAD_DIM), D_MODEL ** -0.5),
        "w_mo": nrm((L, MEM_HEADS * MEM_HEAD_DIM, D_MODEL), (MEM_HEADS * MEM_HEAD_DIM) ** -0.5),
        "g_ffn": gain((L, D_MODEL)),
        "w_router": nrm((L, D_MODEL, N_EXPERTS), D_MODEL ** -0.5),
        "b_router": nrm((L, N_EXPERTS), 0.01),
        "w_gate_up": nrm((L, N_EXPERTS, D_MODEL, 2 * D_FF), D_MODEL ** -0.5),
        "b_gate_up": nrm((L, N_EXPERTS, 2 * D_FF), 0.02),
        "w_down": nrm((L, N_EXPERTS, D_FF, D_MODEL), D_FF ** -0.5),
        "b_down": nrm((L, N_EXPERTS, D_MODEL), 0.02),
        "g_final": gain((D_MODEL,)),
    }


def reference(x_prompt, x_sample, mem_prompt, cache_conv, cache_ckv, cache_krope, cache_mem_k, cache_mem_v,
              page_table, g_mix, w_in, w_conv, g_q, w_uq, g_kv, w_uk, w_uv, g_conv_out, g_attn_out, w_out,
              g_cross, g_mem, w_mq, w_mk, w_mv, w_mo, g_ffn, w_router, b_router, w_gate_up, b_gate_up,
              w_down, b_down, g_final):
    bp, s_len = x_prompt.shape[:2]
    bd, t_len = x_sample.shape[:2]
    n_pages = page_table.shape[1]
    past = n_pages * PAGE_SIZE
    pos_p = jnp.arange(s_len)
    pos_s = past + jnp.arange(t_len)
    hp, hs = x_prompt, x_sample
    conv_p_l, ckv_p_l, kr_p_l, mk_p_l, mv_p_l = [], [], [], [], []
    conv_s_l, ckv_s_l, kr_s_l = [], [], []
    for l in range(DEPTH):
        ap = rmsnorm(hp, g_mix[l])
        b_g, c_g, x_c, qn, qr, lat, kr = in_proj(ap, w_in[l], g_q[l], w_uq[l], g_kv[l], pos_p)
        y_c, conv_new = short_conv(b_g, c_g, x_c, jnp.zeros((bp, CONV_K - 1, CONV_WIDTH), hp.dtype), w_conv[l])
        y_a = mla_prompt(qn, qr, lat, kr, w_uk[l], w_uv[l])
        hp = hp + merge_groups(y_c, y_a, g_conv_out[l], g_attn_out[l], w_out[l])
        conv_p_l.append(conv_new)
        ckv_p_l.append(lat)
        kr_p_l.append(kr)
        mk, mv = memory_kv(mem_prompt, g_mem[l], w_mk[l], w_mv[l])
        hp = hp + cross_attn(rmsnorm(hp, g_cross[l]), mk, mv, w_mq[l], w_mo[l])
        mk_p_l.append(mk)
        mv_p_l.append(mv)
        hp = hp + moe(rmsnorm(hp, g_ffn[l]), w_router[l], b_router[l], w_gate_up[l], b_gate_up[l], w_down[l], b_down[l])

        a_s = rmsnorm(hs, g_mix[l])
        b_g, c_g, x_c, qn, qr, lat, kr = in_proj(a_s, w_in[l], g_q[l], w_uq[l], g_kv[l], pos_s)
        y_c, conv_new = short_conv(b_g, c_g, x_c, cache_conv[l], w_conv[l])
        ckv_past = cache_ckv[l][page_table].reshape(bd, past, KV_LORA)
        kr_past = cache_krope[l][page_table].reshape(bd, past, QK_ROPE)
        y_a = mla_sample(qn, qr, lat, kr, ckv_past, kr_past, w_uk[l], w_uv[l])
        hs = hs + merge_groups(y_c, y_a, g_conv_out[l], g_attn_out[l], w_out[l])
        conv_s_l.append(conv_new)
        ckv_s_l.append(lat)
        kr_s_l.append(kr)
        hs = hs + cross_attn(rmsnorm(hs, g_cross[l]), cache_mem_k[l], cache_mem_v[l], w_mq[l], w_mo[l])
        hs = hs + moe(rmsnorm(hs, g_ffn[l]), w_router[l], b_router[l], w_gate_up[l], b_gate_up[l], w_down[l], b_down[l])

    y_prompt = rmsnorm(hp, g_final)
    y_sample = rmsnorm(hs, g_final)
    conv_prompt = jnp.stack(conv_p_l, 0)
    ckv_prompt = jnp.stack(ckv_p_l, 0)
    krope_prompt = jnp.stack(kr_p_l, 0)
    mem_k_prompt = jnp.stack(mk_p_l, 0)
    mem_v_prompt = jnp.stack(mv_p_l, 0)
    conv_sample = jnp.stack(conv_s_l, 0)
    ckv_sample = jnp.stack(ckv_s_l, 0)
    krope_sample = jnp.stack(kr_s_l, 0)
    return (y_prompt, y_sample, conv_prompt, ckv_prompt, krope_prompt, mem_k_prompt, mem_v_prompt,
            conv_sample, ckv_sample, krope_sample)
```

```python
import functools

import jax
import jax.numpy as jnp
from jax import lax
from jax.experimental import pallas as pl
from jax.experimental.pallas import tpu as pltpu

D_MODEL = 1024
CONV_WIDTH = 512
CONV_K = 3
MLA_HEADS = 8
QK_NOPE = 64
QK_ROPE = 32
V_HEAD = 64
Q_LORA = 384
KV_LORA = 256
ROPE_THETA = 10000.0
PAGE_SIZE = 128
MEM_TOKENS = 256
MEM_HEADS = 4
MEM_HEAD_DIM = D_MODEL // MEM_HEADS
N_EXPERTS = 32
TOP_K = 4
D_FF = D_MODEL
SWIGLU_LIMIT = 7.0
SWIGLU_ALPHA = 1.702
NORM_EPS = 1e-6

LANES = 128
HALF_ROPE = QK_ROPE // 2
IN_WIDTH = 3 * CONV_WIDTH + Q_LORA + KV_LORA + QK_ROPE
IN_WIDTH_PAD = 3 * CONV_WIDTH + Q_LORA + KV_LORA + LANES
OFF_CQ = 3 * CONV_WIDTH
OFF_CKV = OFF_CQ + Q_LORA
OFF_KR = OFF_CKV + KV_LORA
MLA_SCALE = (QK_NOPE + QK_ROPE) ** -0.5
MEM_SCALE = MEM_HEAD_DIM ** -0.5
VMEM_LIMIT = 48 * 1024 * 1024
PAGES_PER_STEP = 16

BF16 = jnp.bfloat16
F32 = jnp.float32
NT_DIMS = (((1,), (1,)), ((), ()))


def _params(*sem):
    return pltpu.CompilerParams(dimension_semantics=sem, vmem_limit_bytes=VMEM_LIMIT)


def _rms(x, g):
    return x * lax.rsqrt(jnp.mean(x * x, axis=-1, keepdims=True) + NORM_EPS) * g


def _dot(a, b):
    return jnp.dot(a, b, preferred_element_type=F32)


def _dot_nt(a, b):
    return lax.dot_general(a, b, NT_DIMS, preferred_element_type=F32)


def _rope_group(x, rc, ra, rb):
    return x * rc + pltpu.roll(x, LANES - HALF_ROPE, 1) * ra + pltpu.roll(x, HALF_ROPE, 1) * rb


def _full(shape):
    return pl.BlockSpec(shape, lambda *_: (0,) * len(shape))


def _memkv_kernel(mem_ref, g_ref, wk_ref, wv_ref, k_ref, v_ref):
    m = _rms(mem_ref[0], g_ref[...]).astype(BF16)
    k_ref[0] = _dot(m, wk_ref[...])
    v_ref[0] = _dot(m, wv_ref[...])


def _memory_kv(mem, g_mem, w_mk, w_mv):
    b, n, d = mem.shape
    blk = pl.BlockSpec((1, n, d), lambda i: (i, 0, 0))
    return pl.pallas_call(
        _memkv_kernel,
        grid=(b,),
        in_specs=[blk, _full((1, d)), _full((d, d)), _full((d, d))],
        out_specs=[blk, blk],
        out_shape=[jax.ShapeDtypeStruct((b, n, d), F32)] * 2,
        compiler_params=_params("arbitrary"),
        name="memory_kv",
    )(mem, g_mem, w_mk, w_mv)


def _inproj_common(x, gmix, win, gq, wuq, gkv, rc, ra, rb):
    a = _rms(x, gmix).astype(BF16)
    z = _dot(a, win)
    b_g = z[:, 0:CONV_WIDTH]
    u = z[:, CONV_WIDTH:2 * CONV_WIDTH] * z[:, 2 * CONV_WIDTH:3 * CONV_WIDTH]
    cq = _rms(z[:, OFF_CQ:OFF_CKV], gq).astype(BF16)
    q = _dot(cq, wuq)
    q_heads = [_rope_group(q[:, h * LANES:(h + 1) * LANES], rc, ra, rb) for h in range(MLA_HEADS)]
    lat = _rms(z[:, OFF_CKV:OFF_KR], gkv)
    kr = _rope_group(z[:, OFF_KR:OFF_KR + LANES], rc, ra, rb)
    return b_g, u, q_heads, lat, kr


def _inproj_prompt_kernel(x_ref, gmix_ref, win_ref, wconv_ref, gq_ref, wuq_ref, gkv_ref, wuk_ref, wuv_ref,
                          gco_ref, rc_ref, ra_ref, rb_ref,
                          q_ref, k_ref, v_ref, lat_ref, kr_ref, ycn_ref, conv_ref, ubuf, *, tm):
    j = pl.program_id(1)
    b_g, u, q_heads, lat, kr = _inproj_common(
        x_ref[0], gmix_ref[...], win_ref[...], gq_ref[...], wuq_ref[...], gkv_ref[...],
        rc_ref[...], ra_ref[...], rb_ref[...])
    for h in range(MLA_HEADS):
        q_ref[0, h] = q_heads[h].astype(BF16)
    lat_ref[0] = lat
    kr_ref[0] = kr[:, QK_NOPE:QK_NOPE + QK_ROPE]
    lat_b = lat.astype(BF16)
    kn = _dot(lat_b, wuk_ref[...])
    v = _dot(lat_b, wuv_ref[...])
    for h in range(MLA_HEADS):
        k_ref[0, h] = (kn[:, h * LANES:(h + 1) * LANES] + kr).astype(BF16)
        v_ref[0, h] = v[:, h * LANES:(h + 1) * LANES].astype(BF16)

    @pl.when(j == 0)
    def _():
        ubuf[0:8, :] = jnp.zeros((8, CONV_WIDTH), F32)

    ubuf[8:8 + tm, :] = u
    u1 = ubuf[7:7 + tm, :]
    u2 = ubuf[6:6 + tm, :]
    wc = wconv_ref[...]
    yc = b_g * (wc[0:1, :] * u2 + wc[1:2, :] * u1 + wc[2:3, :] * u)
    ycn_ref[0] = _rms(yc, gco_ref[...]).astype(BF16)
    ubuf[0:8, :] = ubuf[tm:tm + 8, :]
    conv_ref[0] = u[tm - (CONV_K - 1):tm, :]


def _inproj_prompt(x, w, tm):
    b, s, d = x.shape
    h = MLA_HEADS
    tok = lambda n: pl.BlockSpec((1, tm, n), lambda i, j: (i, j, 0))
    head = pl.BlockSpec((1, h, tm, LANES), lambda i, j: (i, 0, j, 0))
    rope = pl.BlockSpec((tm, LANES), lambda i, j: (j, 0))
    qkv_shape = jax.ShapeDtypeStruct((b, h, s, LANES), BF16)
    return pl.pallas_call(
        functools.partial(_inproj_prompt_kernel, tm=tm),
        grid=(b, s // tm),
        in_specs=[tok(d), _full((1, d)), _full((d, IN_WIDTH_PAD)), _full((CONV_K, CONV_WIDTH)),
                  _full((1, Q_LORA)), _full((Q_LORA, h * LANES)), _full((1, KV_LORA)),
                  _full((KV_LORA, h * LANES)), _full((KV_LORA, h * LANES)), _full((1, CONV_WIDTH)),
                  rope, rope, rope],
        out_specs=[head, head, head, tok(KV_LORA), tok(QK_ROPE), tok(CONV_WIDTH),
                   pl.BlockSpec((1, CONV_K - 1, CONV_WIDTH), lambda i, j: (i, 0, 0))],
        out_shape=[qkv_shape, qkv_shape, qkv_shape,
                   jax.ShapeDtypeStruct((b, s, KV_LORA), F32),
                   jax.ShapeDtypeStruct((b, s, QK_ROPE), F32),
                   jax.ShapeDtypeStruct((b, s, CONV_WIDTH), BF16),
                   jax.ShapeDtypeStruct((b, CONV_K - 1, CONV_WIDTH), F32)],
        scratch_shapes=[pltpu.VMEM((tm + 8, CONV_WIDTH), F32)],
        compiler_params=_params("arbitrary", "arbitrary"),
        name="inproj_prompt",
    )(x, w["g_mix"], w["w_in"], w["w_conv"], w["g_q"], w["w_uq"], w["g_kv"], w["w_uk"], w["w_uv"],
      w["g_conv_out"], *w["rope_prompt"])


def _inproj_sample_kernel(x_ref, gmix_ref, win_ref, wconv_ref, gq_ref, wuq_ref, gkv_ref, wabs_ref,
                          gco_ref, rc_ref, ra_ref, rb_ref, p0_ref, p1_ref,
                          q_ref, qlat_ref, lat_ref, kr_ref, ycn_ref, u_ref):
    b_g, u, q_heads, lat, kr = _inproj_common(
        x_ref[...], gmix_ref[...], win_ref[...], gq_ref[...], wuq_ref[...], gkv_ref[...],
        rc_ref[...], ra_ref[...], rb_ref[...])
    for h in range(MLA_HEADS):
        qh = q_heads[h].astype(BF16)
        q_ref[h] = qh
        qlat_ref[h] = _dot(qh, wabs_ref[h]).astype(BF16)
    lat_ref[...] = lat
    kr_ref[...] = kr[:, QK_NOPE:QK_NOPE + QK_ROPE]
    wc = wconv_ref[...]
    yc = b_g * (wc[0:1, :] * p0_ref[...] + wc[1:2, :] * p1_ref[...] + wc[2:3, :] * u)
    ycn_ref[...] = _rms(yc, gco_ref[...]).astype(BF16)
    u_ref[...] = u


def _inproj_sample(x, prev0, prev1, w):
    n, d = x.shape
    h = MLA_HEADS
    return pl.pallas_call(
        _inproj_sample_kernel,
        grid=(1,),
        in_specs=[_full((n, d)), _full((1, d)), _full((d, IN_WIDTH_PAD)), _full((CONV_K, CONV_WIDTH)),
                  _full((1, Q_LORA)), _full((Q_LORA, h * LANES)), _full((1, KV_LORA)),
                  _full((h, LANES, KV_LORA)), _full((1, CONV_WIDTH)),
                  _full((n, LANES)), _full((n, LANES)), _full((n, LANES)),
                  _full((n, CONV_WIDTH)), _full((n, CONV_WIDTH))],
        out_specs=[_full((h, n, LANES)), _full((h, n, KV_LORA)), _full((n, KV_LORA)), _full((n, QK_ROPE)),
                   _full((n, CONV_WIDTH)), _full((n, CONV_WIDTH))],
        out_shape=[jax.ShapeDtypeStruct((h, n, LANES), BF16),
                   jax.ShapeDtypeStruct((h, n, KV_LORA), BF16),
                   jax.ShapeDtypeStruct((n, KV_LORA), F32),
                   jax.ShapeDtypeStruct((n, QK_ROPE), F32),
                   jax.ShapeDtypeStruct((n, CONV_WIDTH), BF16),
                   jax.ShapeDtypeStruct((n, CONV_WIDTH), F32)],
        compiler_params=_params("arbitrary"),
        name="inproj_sample",
    )(x, w["g_mix"], w["w_in"], w["w_conv"], w["g_q"], w["w_uq"], w["g_kv"], w["w_abs"],
      w["g_conv_out"], *w["rope_sample"], prev0, prev1)


def _softmax_step(s, v, m_sc, l_sc, acc_sc):
    m_prev = m_sc[...]
    m_next = jnp.maximum(m_prev, jnp.max(s, axis=1, keepdims=True))
    p = jnp.exp(s - m_next[:, 0:1])
    alpha = jnp.exp(m_prev - m_next)
    l_sc[...] = alpha * l_sc[...] + jnp.sum(p, axis=1, keepdims=True)
    pv = _dot(p.astype(BF16), v)
    acc_sc[...] = acc_sc[...] * alpha[:, 0:1] + pv
    m_sc[...] = m_next


def _attn_kernel(q_ref, k_ref, v_ref, o_ref, m_sc, l_sc, acc_sc, *, tq):
    qi = pl.program_id(2)
    row = lax.broadcasted_iota(jnp.int32, (tq, tq), 0)
    col = lax.broadcasted_iota(jnp.int32, (tq, tq), 1)
    out = None
    for hh in range(2):
        q = q_ref[0, hh]
        m_sc[...] = jnp.full(m_sc.shape, -jnp.inf, F32)
        l_sc[...] = jnp.zeros(l_sc.shape, F32)
        acc_sc[...] = jnp.zeros(acc_sc.shape, F32)

        def scores(j):
            start = pl.multiple_of(j * tq, tq)
            k = k_ref[0, hh, pl.ds(start, tq), :]
            v = v_ref[0, hh, pl.ds(start, tq), :]
            return _dot_nt(q, k) * MLA_SCALE, v

        def body(j, carry):
            s, v = scores(j)
            _softmax_step(s, v, m_sc, l_sc, acc_sc)
            return carry

        lax.fori_loop(0, qi, body, 0)
        s, v = scores(qi)
        _softmax_step(jnp.where(col <= row, s, -jnp.inf), v, m_sc, l_sc, acc_sc)
        o = acc_sc[...] / l_sc[:, 0:1]
        out = o if out is None else out + o
    o_ref[0] = out


def _attention(q, k, v, tq):
    b, h, s, _ = q.shape
    qspec = pl.BlockSpec((1, 2, tq, LANES), lambda i, p, j: (i, p, j, 0))
    kvspec = pl.BlockSpec((1, 2, s, LANES), lambda i, p, j: (i, p, 0, 0))
    return pl.pallas_call(
        functools.partial(_attn_kernel, tq=tq),
        grid=(b, h // 2, s // tq),
        in_specs=[qspec, kvspec, kvspec],
        out_specs=pl.BlockSpec((1, tq, LANES), lambda i, p, j: (i, j, p)),
        out_shape=jax.ShapeDtypeStruct((b, s, h * V_HEAD), F32),
        scratch_shapes=[pltpu.VMEM((tq, LANES), F32)] * 3,
        compiler_params=_params("arbitrary", "arbitrary", "arbitrary"),
        name="mla_prompt_attention",
    )(q, k, v)


def _decode_kernel(pt_ref, qlat_ref, q_ref, lat_ref, krn_ref, *rest, n_steps):
    del pt_ref
    npg = PAGES_PER_STEP
    ckv_refs = rest[:npg]
    kr_refs = rest[npg:2 * npg]
    o_ref, m_sc, l_sc, acc_sc = rest[2 * npg:]
    step = pl.program_id(1)

    @pl.when(step == 0)
    def _():
        m_sc[...] = jnp.full(m_sc.shape, -jnp.inf, F32)
        l_sc[...] = jnp.zeros(l_sc.shape, F32)
        acc_sc[...] = jnp.zeros(acc_sc.shape, F32)

    ql = qlat_ref[0]
    qr = q_ref[0][:, QK_NOPE:QK_NOPE + QK_ROPE]
    ckv = jnp.concatenate([r[0, 0] for r in ckv_refs], axis=0).astype(BF16)
    kr = jnp.concatenate([r[0, 0] for r in kr_refs], axis=0).astype(BF16)
    s = (_dot_nt(ql, ckv) + _dot_nt(qr, kr)) * MLA_SCALE
    _softmax_step(s, ckv, m_sc, l_sc, acc_sc)

    @pl.when(step == n_steps - 1)
    def _():
        lat = lat_ref[0]
        s_new = (jnp.sum(ql.astype(F32) * lat, axis=1, keepdims=True)
                 + jnp.sum(qr.astype(F32) * krn_ref[0], axis=1, keepdims=True)) * MLA_SCALE
        m_prev = m_sc[...]
        m_next = jnp.maximum(m_prev, s_new)
        p_new = jnp.exp(s_new - m_next[:, 0:1])
        alpha = jnp.exp(m_prev - m_next)
        l_fin = alpha * l_sc[...] + p_new
        acc = acc_sc[...] * alpha[:, 0:1] + p_new * lat
        o_ref[0] = acc / l_fin[:, 0:1]


def _decode(page_table, qlat, q, lat, krn, cache_ckv, cache_krope):
    bd, n_pages = page_table.shape
    npg = PAGES_PER_STEP
    n_steps = n_pages // npg
    h = MLA_HEADS

    def page_spec(width, i):
        return pl.BlockSpec((1, 1, PAGE_SIZE, width),
                            lambda b, s, pt, i=i: (0, pt[b, s * npg + i], 0, 0))

    per_b = lambda *shape: pl.BlockSpec((1,) + shape, lambda b, s, pt: (b,) + (0,) * len(shape))
    grid_spec = pltpu.PrefetchScalarGridSpec(
        num_scalar_prefetch=1,
        grid=(bd, n_steps),
        in_specs=[per_b(h, KV_LORA), per_b(h, LANES), per_b(1, KV_LORA), per_b(1, QK_ROPE)]
        + [page_spec(KV_LORA, i) for i in range(npg)]
        + [page_spec(QK_ROPE, i) for i in range(npg)],
        out_specs=per_b(h, KV_LORA),
        scratch_shapes=[pltpu.VMEM((h, LANES), F32), pltpu.VMEM((h, LANES), F32),
                        pltpu.VMEM((h, KV_LORA), F32)],
    )
    return pl.pallas_call(
        functools.partial(_decode_kernel, n_steps=n_steps),
        grid_spec=grid_spec,
        out_shape=jax.ShapeDtypeStruct((bd, h, KV_LORA), F32),
        compiler_params=_params("arbitrary", "arbitrary"),
        name="mla_decode",
    )(page_table, qlat, q, lat, krn, *([cache_ckv] * npg), *([cache_krope] * npg))


def _uv_kernel(olat_ref, wuv_ref, ya_ref):
    acc = None
    for h in range(MLA_HEADS):
        part = _dot(olat_ref[h].astype(BF16), wuv_ref[h])
        acc = part if acc is None else acc + part
    ya_ref[...] = acc


def _uv_project(olat, wuv_heads):
    h, n, c = olat.shape
    width = MLA_HEADS * V_HEAD
    return pl.pallas_call(
        _uv_kernel,
        grid=(1,),
        in_specs=[_full((h, n, c)), _full((h, c, width))],
        out_specs=_full((n, width)),
        out_shape=jax.ShapeDtypeStruct((n, width), F32),
        compiler_params=_params("arbitrary"),
        name="decode_uv",
    )(olat, wuv_heads)


def _mixout_kernel(x_ref, ycn_ref, ya_ref, ga_ref, wout_ref, gcross_ref, wmq_ref, h1_ref, qm_ref):
    yan = _rms(ya_ref[...], ga_ref[...]).astype(BF16)
    mixed = jnp.concatenate([ycn_ref[...], yan], axis=1)
    h1 = x_ref[...] + _dot(mixed, wout_ref[...])
    h1_ref[...] = h1
    qm_ref[...] = _dot(_rms(h1, gcross_ref[...]).astype(BF16), wmq_ref[...]).astype(BF16)


def _mixout(x, ycn, ya, w, tm):
    n, d = x.shape
    mix = CONV_WIDTH + MLA_HEADS * V_HEAD
    tok = lambda width: pl.BlockSpec((tm, width), lambda i: (i, 0))
    return pl.pallas_call(
        _mixout_kernel,
        grid=(n // tm,),
        in_specs=[tok(d), tok(CONV_WIDTH), tok(MLA_HEADS * V_HEAD), _full((1, MLA_HEADS * V_HEAD)),
                  _full((mix, d)), _full((1, d)), _full((d, d))],
        out_specs=[tok(d), tok(d)],
        out_shape=[jax.ShapeDtypeStruct((n, d), F32), jax.ShapeDtypeStruct((n, d), BF16)],
        compiler_params=_params("arbitrary"),
        name="mix_out",
    )(x, ycn, ya, w["g_attn_out"], w["w_out"], w["g_cross"], w["w_mq"])


def _cross_prompt_kernel(q_ref, k_ref, v_ref, o_ref):
    q = q_ref[0]
    outs = []
    for h in range(MEM_HEADS):
        sl = slice(h * MEM_HEAD_DIM, (h + 1) * MEM_HEAD_DIM)
        s = _dot_nt(q[:, sl], k_ref[0, :, sl].astype(BF16)) * MEM_SCALE
        e = jnp.exp(s - jnp.max(s, axis=1, keepdims=True))
        p = e / jnp.sum(e, axis=1, keepdims=True)
        outs.append(_dot(p.astype(BF16), v_ref[0, :, sl].astype(BF16)))
    o_ref[0] = jnp.concatenate(outs, axis=1).astype(BF16)


def _cross_prompt(qm, mk, mv, tm):
    b, s, d = qm.shape
    tok = pl.BlockSpec((1, tm, d), lambda i, j: (i, j, 0))
    mem = pl.BlockSpec((1, MEM_TOKENS, d), lambda i, j: (i, 0, 0))
    return pl.pallas_call(
        _cross_prompt_kernel,
        grid=(b, s // tm),
        in_specs=[tok, mem, mem],
        out_specs=tok,
        out_shape=jax.ShapeDtypeStruct((b, s, d), BF16),
        compiler_params=_params("arbitrary", "arbitrary"),
        name="cross_prompt",
    )(qm, mk, mv)


def _cross_sample_kernel(q_ref, k_ref, v_ref, o_ref, *, tb):
    for t in range(tb):
        q = q_ref[t:t + 1, :]
        prod = k_ref[t] * q
        outs = []
        for h in range(MEM_HEADS):
            sl = slice(h * MEM_HEAD_DIM, (h + 1) * MEM_HEAD_DIM)
            s = jnp.sum(prod[:, sl], axis=1, keepdims=True) * MEM_SCALE
            e = jnp.exp(s - jnp.max(s, axis=0, keepdims=True))
            p = e / jnp.sum(e, axis=0, keepdims=True)
            outs.append(jnp.sum(p * v_ref[t, :, sl], axis=0, keepdims=True))
        o_ref[t:t + 1, :] = jnp.concatenate(outs, axis=1)


def _cross_sample(qm, mk, mv, tb):
    n, d = qm.shape
    tok = pl.BlockSpec((tb, d), lambda i: (i, 0))
    mem = pl.BlockSpec((tb, MEM_TOKENS, d), lambda i: (i, 0, 0))
    return pl.pallas_call(
        functools.partial(_cross_sample_kernel, tb=tb),
        grid=(n // tb,),
        in_specs=[tok, mem, mem],
        out_specs=tok,
        out_shape=jax.ShapeDtypeStruct((n, d), F32),
        compiler_params=_params("arbitrary"),
        name="cross_sample",
    )(qm, mk, mv)


def _router_kernel(h1_ref, o_ref, wmo_ref, gffn_ref, wr_ref, br_ref, h2_ref, xt_ref, gate_ref):
    h2 = h1_ref[...] + _dot(o_ref[...], wmo_ref[...])
    h2_ref[...] = h2
    xt = _rms(h2, gffn_ref[...]).astype(BF16)
    xt_ref[...] = xt
    logits = _dot(xt, wr_ref[...]) + br_ref[...]
    lane = lax.broadcasted_iota(jnp.int32, logits.shape, 1)
    logits = jnp.where(lane < N_EXPERTS, logits, -jnp.inf)
    work = logits
    sel = lane < 0
    top = None
    for k in range(TOP_K):
        m = jnp.max(work, axis=1, keepdims=True)
        if k == 0:
            top = m
        idx = jnp.min(jnp.where(work == m, lane, LANES), axis=1, keepdims=True)
        hit = lane == idx
        sel = jnp.logical_or(sel, hit)
        work = jnp.where(hit, -jnp.inf, work)
    e = jnp.where(sel, jnp.exp(logits - top), 0.0)
    gate_ref[...] = e / jnp.sum(e, axis=1, keepdims=True)


def _router(h1, o, w, tm):
    n, d = h1.shape
    tok = lambda width: pl.BlockSpec((tm, width), lambda i: (i, 0))
    return pl.pallas_call(
        _router_kernel,
        grid=(n // tm,),
        in_specs=[tok(d), tok(d), _full((d, d)), _full((1, d)), _full((d, LANES)), _full((1, LANES))],
        out_specs=[tok(d), tok(d), tok(LANES)],
        out_shape=[jax.ShapeDtypeStruct((n, d), F32), jax.ShapeDtypeStruct((n, d), BF16),
                   jax.ShapeDtypeStruct((n, LANES), F32)],
        compiler_params=_params("arbitrary"),
        name="router",
    )(h1, o, w["w_mo"], w["g_ffn"], w["w_router"], w["b_router"])


def _moe_kernel(xt_ref, gate_ref, h2_ref, wgu_ref, bgu_ref, wd_ref, bd_ref, gfin_ref, y_ref, acc_sc):
    e = pl.program_id(1)

    @pl.when(e == 0)
    def _():
        acc_sc[...] = jnp.zeros(acc_sc.shape, F32)

    gu = _dot(xt_ref[...], wgu_ref[0]) + bgu_ref[0]
    g = jnp.minimum(gu[:, :D_FF], SWIGLU_LIMIT)
    u = jnp.clip(gu[:, D_FF:], -SWIGLU_LIMIT, SWIGLU_LIMIT)
    hdn = (u + 1.0) * (g * (1.0 / (1.0 + jnp.exp(-SWIGLU_ALPHA * g))))
    down = _dot(hdn.astype(BF16), wd_ref[0]) + bd_ref[0]
    gate = gate_ref[...]
    lane = lax.broadcasted_iota(jnp.int32, gate.shape, 1)
    ge = jnp.sum(jnp.where(lane == e, gate, 0.0), axis=1, keepdims=True)
    acc_sc[...] += ge * down

    @pl.when(e == N_EXPERTS - 1)
    def _():
        y_ref[...] = _rms(h2_ref[...] + acc_sc[...], gfin_ref[...])


def _moe(xt, gate, h2, w, tm):
    n, d = xt.shape
    tok = lambda width: pl.BlockSpec((tm, width), lambda i, e: (i, 0))
    return pl.pallas_call(
        _moe_kernel,
        grid=(n // tm, N_EXPERTS),
        in_specs=[tok(d), tok(LANES), tok(d),
                  pl.BlockSpec((1, d, 2 * D_FF), lambda i, e: (e, 0, 0)),
                  pl.BlockSpec((1, 1, 2 * D_FF), lambda i, e: (e, 0, 0)),
                  pl.BlockSpec((1, D_FF, d), lambda i, e: (e, 0, 0)),
                  pl.BlockSpec((1, 1, d), lambda i, e: (e, 0, 0)),
                  _full((1, d))],
        out_specs=tok(d),
        out_shape=jax.ShapeDtypeStruct((n, d), F32),
        scratch_shapes=[pltpu.VMEM((tm, d), F32)],
        compiler_params=_params("arbitrary", "arbitrary"),
        name="moe",
    )(xt, gate, h2, w["w_gate_up"], w["b_gate_up"], w["w_down"], w["b_down"], w["g_final"])


def _rope_tables(pos):
    inv = ROPE_THETA ** (-jnp.arange(HALF_ROPE, dtype=F32) / HALF_ROPE)
    ang = pos.astype(F32)[:, None] * inv[None, :]
    cos, sin = jnp.cos(ang), jnp.sin(ang)
    n = pos.shape[0]
    pad = LANES - QK_NOPE - QK_ROPE
    rc = jnp.concatenate([jnp.ones((n, QK_NOPE), F32), cos, cos, jnp.zeros((n, pad), F32)], axis=1)
    ra = jnp.concatenate([jnp.zeros((n, QK_NOPE), F32), -sin, jnp.zeros((n, HALF_ROPE + pad), F32)], axis=1)
    rb = jnp.concatenate([jnp.zeros((n, QK_NOPE + HALF_ROPE), F32), sin, jnp.zeros((n, pad), F32)], axis=1)
    return rc, ra, rb


def _prepare(l, s_len, n_sample, past, g_mix, w_in, w_conv, g_q, w_uq, g_kv, w_uk, w_uv, g_conv_out, g_attn_out,
             w_out, g_cross, g_mem, w_mq, w_mk, w_mv, w_mo, g_ffn, w_router, b_router, w_gate_up, b_gate_up,
             w_down, b_down, g_final):
    h = MLA_HEADS
    row = lambda g: g.reshape(1, -1).astype(F32)
    head_pad = LANES - QK_NOPE - QK_ROPE
    win = w_in[l]
    win = jnp.concatenate([win[:, :OFF_KR], jnp.zeros((D_MODEL, QK_NOPE), F32), win[:, OFF_KR:],
                           jnp.zeros((D_MODEL, head_pad), F32)], axis=1)
    wuq = jnp.pad(w_uq[l].reshape(Q_LORA, h, QK_NOPE + QK_ROPE), ((0, 0), (0, 0), (0, head_pad)))
    wuk = jnp.pad(w_uk[l], ((0, 0), (0, 0), (0, LANES - QK_NOPE)))
    wuv_pair = w_uv[l].reshape(KV_LORA, h // 2, 2, V_HEAD)
    zeros = jnp.zeros((KV_LORA, h // 2, V_HEAD), F32)
    wuv = jnp.stack([jnp.concatenate([wuv_pair[:, :, 0], zeros], axis=-1),
                     jnp.concatenate([zeros, wuv_pair[:, :, 1]], axis=-1)], axis=2)
    wabs = jnp.pad(jnp.transpose(w_uk[l], (1, 2, 0)), ((0, 0), (0, LANES - QK_NOPE), (0, 0)))
    eye = jnp.eye(h, dtype=F32)
    wuv_heads = jnp.einsum("chd,hg->hcgd", w_uv[l], eye).reshape(h, KV_LORA, h * V_HEAD)
    wr = jnp.pad(w_router[l], ((0, 0), (0, LANES - N_EXPERTS)))
    br = jnp.pad(b_router[l], (0, LANES - N_EXPERTS)).reshape(1, LANES)
    return {
        "g_mix": row(g_mix[l]), "w_in": win.astype(BF16), "w_conv": w_conv[l].astype(F32),
        "g_q": row(g_q[l]), "w_uq": wuq.reshape(Q_LORA, h * LANES).astype(BF16),
        "g_kv": row(g_kv[l]), "w_uk": wuk.reshape(KV_LORA, h * LANES).astype(BF16),
        "w_uv": wuv.reshape(KV_LORA, h * LANES).astype(BF16), "w_abs": wabs.astype(BF16),
        "w_uv_heads": wuv_heads.astype(BF16),
        "g_conv_out": row(g_conv_out[l]), "g_attn_out": row(g_attn_out[l]),
        "w_out": w_out[l].astype(BF16), "g_cross": row(g_cross[l]), "g_mem": row(g_mem[l]),
        "w_mq": w_mq[l].astype(BF16), "w_mk": w_mk[l].astype(BF16), "w_mv": w_mv[l].astype(BF16),
        "w_mo": w_mo[l].astype(BF16), "g_ffn": row(g_ffn[l]),
        "w_router": wr.astype(BF16), "b_router": br.astype(F32),
        "w_gate_up": w_gate_up[l].astype(BF16), "b_gate_up": b_gate_up[l].reshape(N_EXPERTS, 1, 2 * D_FF),
        "w_down": w_down[l].astype(BF16), "b_down": b_down[l].reshape(N_EXPERTS, 1, D_MODEL),
        "g_final": row(g_final),
        "rope_prompt": _rope_tables(jnp.arange(s_len)),
        "rope_sample": _rope_tables(jnp.full((n_sample,), past)),
    }


def kernel(x_prompt, x_sample, mem_prompt, cache_conv, cache_ckv, cache_krope, cache_mem_k, cache_mem_v, page_table, g_mix, w_in, w_conv, g_q, w_uq, g_kv, w_uk, w_uv, g_conv_out, g_attn_out, w_out, g_cross, g_mem, w_mq, w_mk, w_mv, w_mo, g_ffn, w_router, b_router, w_gate_up, b_gate_up, w_down, b_down, g_final):
    bp, s_len, d = x_prompt.shape
    bd, t_len, _ = x_sample.shape
    depth = g_mix.shape[0]
    assert depth == 1 and t_len == 1, "kernel is written for one layer and one decode token per sequence"
    n_pages = page_table.shape[1]
    past = n_pages * PAGE_SIZE
    n_p = bp * s_len
    l = 0
    w = _prepare(l, s_len, bd, past, g_mix, w_in, w_conv, g_q, w_uq, g_kv, w_uk, w_uv, g_conv_out, g_attn_out,
                 w_out, g_cross, g_mem, w_mq, w_mk, w_mv, w_mo, g_ffn, w_router, b_router, w_gate_up,
                 b_gate_up, w_down, b_down, g_final)

    q, k, v, lat_p, kr_p, ycn_p, conv_p = _inproj_prompt(x_prompt, w, tm=512)
    ya_p = _attention(q, k, v, tq=512)
    mk_p, mv_p = _memory_kv(mem_prompt, w["g_mem"], w["w_mk"], w["w_mv"])
    xp = x_prompt.reshape(n_p, d)
    h1_p, qm_p = _mixout(xp, ycn_p.reshape(n_p, -1), ya_p.reshape(n_p, -1), w, tm=512)
    o_p = _cross_prompt(qm_p.reshape(bp, s_len, d), mk_p, mv_p, tm=512)
    h2_p, xt_p, gate_p = _router(h1_p, o_p.reshape(n_p, d), w, tm=512)
    y_p = _moe(xt_p, gate_p, h2_p, w, tm=512)

    xs = x_sample.reshape(bd, d)
    prev0 = cache_conv[l, :, 0, :]
    prev1 = cache_conv[l, :, 1, :]
    q_s, qlat_s, lat_s, kr_s, ycn_s, u_s = _inproj_sample(xs, prev0, prev1, w)
    olat = _decode(page_table, jnp.transpose(qlat_s, (1, 0, 2)), jnp.transpose(q_s, (1, 0, 2)),
                   lat_s.reshape(bd, 1, KV_LORA), kr_s.reshape(bd, 1, QK_ROPE), cache_ckv, cache_krope)
    ya_s = _uv_project(jnp.transpose(olat, (1, 0, 2)), w["w_uv_heads"])
    h1_s, qm_s = _mixout(xs, ycn_s, ya_s, w, tm=bd)
    o_s = _cross_sample(qm_s.astype(F32), cache_mem_k[l].reshape(bd, MEM_TOKENS, d),
                        cache_mem_v[l].reshape(bd, MEM_TOKENS, d), tb=8)
    h2_s, xt_s, gate_s = _router(h1_s, o_s.astype(BF16), w, tm=bd)
    y_s = _moe(xt_s, gate_s, h2_s, w, tm=bd)

    mem_shape = (1, bp, MEM_TOKENS, MEM_HEADS, MEM_HEAD_DIM)
    return (y_p.reshape(bp, s_len, d), y_s.reshape(bd, 1, d),
            conv_p[None], lat_p[None], kr_p[None], mk_p.reshape(mem_shape), mv_p.reshape(mem_shape),
            jnp.stack([prev1, u_s], axis=1)[None], lat_s.reshape(1, bd, 1, KV_LORA),
            kr_s.reshape(1, bd, 1, QK_ROPE))
```

```python
import functools

import jax
import jax.numpy as jnp
from jax import lax
from jax.experimental import pallas as pl
from jax.experimental.pallas import tpu as pltpu
from jax.experimental.pallas import tpu_sc as plsc

D_MODEL = 1024
CONV_WIDTH = 512
CONV_K = 3
MLA_HEADS = 8
QK_NOPE = 64
QK_ROPE = 32
V_HEAD = 64
Q_LORA = 384
KV_LORA = 256
ROPE_THETA = 10000.0
PAGE_SIZE = 128
MEM_TOKENS = 256
MEM_HEADS = 4
MEM_HEAD_DIM = D_MODEL // MEM_HEADS
N_EXPERTS = 32
TOP_K = 4
D_FF = D_MODEL
SWIGLU_LIMIT = 7.0
SWIGLU_ALPHA = 1.702
NORM_EPS = 1e-6

LANES = 128
HALF_ROPE = QK_ROPE // 2
IN_WIDTH = 3 * CONV_WIDTH + Q_LORA + KV_LORA + QK_ROPE
IN_WIDTH_PAD = 3 * CONV_WIDTH + Q_LORA + KV_LORA + LANES
OFF_CQ = 3 * CONV_WIDTH
OFF_CKV = OFF_CQ + Q_LORA
OFF_KR = OFF_CKV + KV_LORA
MLA_SCALE = (QK_NOPE + QK_ROPE) ** -0.5
MEM_SCALE = MEM_HEAD_DIM ** -0.5
VMEM_LIMIT = 48 * 1024 * 1024
PAGES_PER_STEP = 16

BF16 = jnp.bfloat16
F32 = jnp.float32
NT_DIMS = (((1,), (1,)), ((), ()))


def _params(*sem):
    return pltpu.CompilerParams(dimension_semantics=sem, vmem_limit_bytes=VMEM_LIMIT)


def _rms(x, g):
    return x * lax.rsqrt(jnp.mean(x * x, axis=-1, keepdims=True) + NORM_EPS) * g


def _dot(a, b):
    return jnp.dot(a, b, preferred_element_type=F32)


def _dot_nt(a, b):
    return lax.dot_general(a, b, NT_DIMS, preferred_element_type=F32)


def _rope_group(x, rc, ra, rb):
    return x * rc + pltpu.roll(x, LANES - HALF_ROPE, 1) * ra + pltpu.roll(x, HALF_ROPE, 1) * rb


def _full(shape):
    return pl.BlockSpec(shape, lambda *_: (0,) * len(shape))


def _memkv_kernel(mem_ref, g_ref, wk_ref, wv_ref, k_ref, v_ref):
    m = _rms(mem_ref[0], g_ref[...]).astype(BF16)
    k_ref[0] = _dot(m, wk_ref[...])
    v_ref[0] = _dot(m, wv_ref[...])


def _memory_kv(mem, g_mem, w_mk, w_mv):
    b, n, d = mem.shape
    blk = pl.BlockSpec((1, n, d), lambda i: (i, 0, 0))
    return pl.pallas_call(
        _memkv_kernel,
        grid=(b,),
        in_specs=[blk, _full((1, d)), _full((d, d)), _full((d, d))],
        out_specs=[blk, blk],
        out_shape=[jax.ShapeDtypeStruct((b, n, d), F32)] * 2,
        compiler_params=_params("arbitrary"),
        name="memory_kv",
    )(mem, g_mem, w_mk, w_mv)


def _inproj_common(x, gmix, win, gq, wuq, gkv, rc, ra, rb):
    a = _rms(x, gmix).astype(BF16)
    z = _dot(a, win)
    b_g = z[:, 0:CONV_WIDTH]
    u = z[:, CONV_WIDTH:2 * CONV_WIDTH] * z[:, 2 * CONV_WIDTH:3 * CONV_WIDTH]
    cq = _rms(z[:, OFF_CQ:OFF_CKV], gq).astype(BF16)
    q = _dot(cq, wuq)
    q_heads = [_rope_group(q[:, h * LANES:(h + 1) * LANES], rc, ra, rb) for h in range(MLA_HEADS)]
    lat = _rms(z[:, OFF_CKV:OFF_KR], gkv)
    kr = _rope_group(z[:, OFF_KR:OFF_KR + LANES], rc, ra, rb)
    return b_g, u, q_heads, lat, kr


def _inproj_prompt_kernel(x_ref, gmix_ref, win_ref, wconv_ref, gq_ref, wuq_ref, gkv_ref, wuk_ref, wuv_ref,
                          gco_ref, rc_ref, ra_ref, rb_ref,
                          q_ref, k_ref, v_ref, lat_ref, kr_ref, ycn_ref, conv_ref, ubuf, *, tm):
    j = pl.program_id(1)
    b_g, u, q_heads, lat, kr = _inproj_common(
        x_ref[0], gmix_ref[...], win_ref[...], gq_ref[...], wuq_ref[...], gkv_ref[...],
        rc_ref[...], ra_ref[...], rb_ref[...])
    for h in range(MLA_HEADS):
        q_ref[0, h] = q_heads[h].astype(BF16)
    lat_ref[0] = lat
    kr_ref[0] = kr[:, QK_NOPE:QK_NOPE + QK_ROPE]
    lat_b = lat.astype(BF16)
    kn = _dot(lat_b, wuk_ref[...])
    v = _dot(lat_b, wuv_ref[...])
    for h in range(MLA_HEADS):
        k_ref[0, h] = (kn[:, h * LANES:(h + 1) * LANES] + kr).astype(BF16)
        v_ref[0, h] = v[:, h * LANES:(h + 1) * LANES].astype(BF16)

    @pl.when(j == 0)
    def _():
        ubuf[0:8, :] = jnp.zeros((8, CONV_WIDTH), F32)

    ubuf[8:8 + tm, :] = u
    u1 = ubuf[7:7 + tm, :]
    u2 = ubuf[6:6 + tm, :]
    wc = wconv_ref[...]
    yc = b_g * (wc[0:1, :] * u2 + wc[1:2, :] * u1 + wc[2:3, :] * u)
    ycn_ref[0] = _rms(yc, gco_ref[...]).astype(BF16)
    ubuf[0:8, :] = ubuf[tm:tm + 8, :]
    conv_ref[0] = u[tm - (CONV_K - 1):tm, :]


def _inproj_prompt(x, w, tm):
    b, s, d = x.shape
    h = MLA_HEADS
    tok = lambda n: pl.BlockSpec((1, tm, n), lambda i, j: (i, j, 0))
    head = pl.BlockSpec((1, h, tm, LANES), lambda i, j: (i, 0, j, 0))
    rope = pl.BlockSpec((tm, LANES), lambda i, j: (j, 0))
    qkv_shape = jax.ShapeDtypeStruct((b, h, s, LANES), BF16)
    return pl.pallas_call(
        functools.partial(_inproj_prompt_kernel, tm=tm),
        grid=(b, s // tm),
        in_specs=[tok(d), _full((1, d)), _full((d, IN_WIDTH_PAD)), _full((CONV_K, CONV_WIDTH)),
                  _full((1, Q_LORA)), _full((Q_LORA, h * LANES)), _full((1, KV_LORA)),
                  _full((KV_LORA, h * LANES)), _full((KV_LORA, h * LANES)), _full((1, CONV_WIDTH)),
                  rope, rope, rope],
        out_specs=[head, head, head, tok(KV_LORA), tok(QK_ROPE), tok(CONV_WIDTH),
                   pl.BlockSpec((1, CONV_K - 1, CONV_WIDTH), lambda i, j: (i, 0, 0))],
        out_shape=[qkv_shape, qkv_shape, qkv_shape,
                   jax.ShapeDtypeStruct((b, s, KV_LORA), F32),
                   jax.ShapeDtypeStruct((b, s, QK_ROPE), F32),
                   jax.ShapeDtypeStruct((b, s, CONV_WIDTH), BF16),
                   jax.ShapeDtypeStruct((b, CONV_K - 1, CONV_WIDTH), F32)],
        scratch_shapes=[pltpu.VMEM((tm + 8, CONV_WIDTH), F32)],
        compiler_params=_params("arbitrary", "arbitrary"),
        name="inproj_prompt",
    )(x, w["g_mix"], w["w_in"], w["w_conv"], w["g_q"], w["w_uq"], w["g_kv"], w["w_uk"], w["w_uv"],
      w["g_conv_out"], *w["rope_prompt"])


def _inproj_sample_kernel(x_ref, gmix_ref, win_ref, wconv_ref, gq_ref, wuq_ref, gkv_ref, wabs_ref,
                          gco_ref, rc_ref, ra_ref, rb_ref, p0_ref, p1_ref,
                          q_ref, qlat_ref, lat_ref, kr_ref, ycn_ref, u_ref):
    b_g, u, q_heads, lat, kr = _inproj_common(
        x_ref[...], gmix_ref[...], win_ref[...], gq_ref[...], wuq_ref[...], gkv_ref[...],
        rc_ref[...], ra_ref[...], rb_ref[...])
    for h in range(MLA_HEADS):
        qh = q_heads[h].astype(BF16)
        q_ref[h] = qh
        qlat_ref[h] = _dot(qh, wabs_ref[h]).astype(BF16)
    lat_ref[...] = lat
    kr_ref[...] = kr[:, QK_NOPE:QK_NOPE + QK_ROPE]
    wc = wconv_ref[...]
    yc = b_g * (wc[0:1, :] * p0_ref[...] + wc[1:2, :] * p1_ref[...] + wc[2:3, :] * u)
    ycn_ref[...] = _rms(yc, gco_ref[...]).astype(BF16)
    u_ref[...] = u


def _inproj_sample(x, prev0, prev1, w):
    n, d = x.shape
    h = MLA_HEADS
    return pl.pallas_call(
        _inproj_sample_kernel,
        grid=(1,),
        in_specs=[_full((n, d)), _full((1, d)), _full((d, IN_WIDTH_PAD)), _full((CONV_K, CONV_WIDTH)),
                  _full((1, Q_LORA)), _full((Q_LORA, h * LANES)), _full((1, KV_LORA)),
                  _full((h, LANES, KV_LORA)), _full((1, CONV_WIDTH)),
                  _full((n, LANES)), _full((n, LANES)), _full((n, LANES)),
                  _full((n, CONV_WIDTH)), _full((n, CONV_WIDTH))],
        out_specs=[_full((h, n, LANES)), _full((h, n, KV_LORA)), _full((n, KV_LORA)), _full((n, QK_ROPE)),
                   _full((n, CONV_WIDTH)), _full((n, CONV_WIDTH))],
        out_shape=[jax.ShapeDtypeStruct((h, n, LANES), BF16),
                   jax.ShapeDtypeStruct((h, n, KV_LORA), BF16),
                   jax.ShapeDtypeStruct((n, KV_LORA), F32),
                   jax.ShapeDtypeStruct((n, QK_ROPE), F32),
                   jax.ShapeDtypeStruct((n, CONV_WIDTH), BF16),
                   jax.ShapeDtypeStruct((n, CONV_WIDTH), F32)],
        compiler_params=_params("arbitrary"),
        name="inproj_sample",
    )(x, w["g_mix"], w["w_in"], w["w_conv"], w["g_q"], w["w_uq"], w["g_kv"], w["w_abs"],
      w["g_conv_out"], *w["rope_sample"], prev0, prev1)


def _softmax_step(s, v, m_sc, l_sc, acc_sc):
    m_prev = m_sc[...]
    m_next = jnp.maximum(m_prev, jnp.max(s, axis=1, keepdims=True))
    p = jnp.exp(s - m_next[:, 0:1])
    alpha = jnp.exp(m_prev - m_next)
    l_sc[...] = alpha * l_sc[...] + jnp.sum(p, axis=1, keepdims=True)
    pv = _dot(p.astype(BF16), v)
    acc_sc[...] = acc_sc[...] * alpha[:, 0:1] + pv
    m_sc[...] = m_next


def _attn_kernel(q_ref, k_ref, v_ref, o_ref, m_sc, l_sc, acc_sc, *, tq):
    qi = pl.program_id(2)
    row = lax.broadcasted_iota(jnp.int32, (tq, tq), 0)
    col = lax.broadcasted_iota(jnp.int32, (tq, tq), 1)
    out = None
    for hh in range(2):
        q = q_ref[0, hh]
        m_sc[...] = jnp.full(m_sc.shape, -jnp.inf, F32)
        l_sc[...] = jnp.zeros(l_sc.shape, F32)
        acc_sc[...] = jnp.zeros(acc_sc.shape, F32)

        def scores(j):
            start = pl.multiple_of(j * tq, tq)
            k = k_ref[0, hh, pl.ds(start, tq), :]
            v = v_ref[0, hh, pl.ds(start, tq), :]
            return _dot_nt(q, k) * MLA_SCALE, v

        def body(j, carry):
            s, v = scores(j)
            _softmax_step(s, v, m_sc, l_sc, acc_sc)
            return carry

        lax.fori_loop(0, qi, body, 0)
        s, v = scores(qi)
        _softmax_step(jnp.where(col <= row, s, -jnp.inf), v, m_sc, l_sc, acc_sc)
        o = acc_sc[...] / l_sc[:, 0:1]
        out = o if out is None else out + o
    o_ref[0] = out


def _attention(q, k, v, tq):
    b, h, s, _ = q.shape
    qspec = pl.BlockSpec((1, 2, tq, LANES), lambda i, p, j: (i, p, j, 0))
    kvspec = pl.BlockSpec((1, 2, s, LANES), lambda i, p, j: (i, p, 0, 0))
    return pl.pallas_call(
        functools.partial(_attn_kernel, tq=tq),
        grid=(b, h // 2, s // tq),
        in_specs=[qspec, kvspec, kvspec],
        out_specs=pl.BlockSpec((1, tq, LANES), lambda i, p, j: (i, j, p)),
        out_shape=jax.ShapeDtypeStruct((b, s, h * V_HEAD), F32),
        scratch_shapes=[pltpu.VMEM((tq, LANES), F32)] * 3,
        compiler_params=_params("arbitrary", "arbitrary", "arbitrary"),
        name="mla_prompt_attention",
    )(q, k, v)


def _decode_kernel(pt_ref, qlat_ref, q_ref, lat_ref, krn_ref, *rest, n_steps):
    del pt_ref
    npg = PAGES_PER_STEP
    ckv_refs = rest[:npg]
    kr_refs = rest[npg:2 * npg]
    o_ref, m_sc, l_sc, acc_sc = rest[2 * npg:]
    step = pl.program_id(1)

    @pl.when(step == 0)
    def _():
        m_sc[...] = jnp.full(m_sc.shape, -jnp.inf, F32)
        l_sc[...] = jnp.zeros(l_sc.shape, F32)
        acc_sc[...] = jnp.zeros(acc_sc.shape, F32)

    ql = qlat_ref[0]
    qr = q_ref[0][:, QK_NOPE:QK_NOPE + QK_ROPE]
    ckv = jnp.concatenate([r[0, 0] for r in ckv_refs], axis=0).astype(BF16)
    kr_t = jnp.concatenate([r[0, 0] for r in kr_refs], axis=1).astype(BF16)
    s = (_dot_nt(ql, ckv) + _dot(qr, kr_t)) * MLA_SCALE
    _softmax_step(s, ckv, m_sc, l_sc, acc_sc)

    @pl.when(step == n_steps - 1)
    def _():
        lat = lat_ref[0]
        s_new = (jnp.sum(ql.astype(F32) * lat, axis=1, keepdims=True)
                 + jnp.sum(qr.astype(F32) * krn_ref[0], axis=1, keepdims=True)) * MLA_SCALE
        m_prev = m_sc[...]
        m_next = jnp.maximum(m_prev, s_new)
        p_new = jnp.exp(s_new - m_next[:, 0:1])
        alpha = jnp.exp(m_prev - m_next)
        l_fin = alpha * l_sc[...] + p_new
        acc = acc_sc[...] * alpha[:, 0:1] + p_new * lat
        o_ref[0] = acc / l_fin[:, 0:1]


def _decode(page_table, qlat, q, lat, krn, cache_ckv, cache_krope):
    bd, n_pages = page_table.shape
    npg = PAGES_PER_STEP
    n_steps = n_pages // npg
    h = MLA_HEADS

    def page_spec(shape, i):
        return pl.BlockSpec((1, 1) + shape, lambda b, s, pt, i=i: (0, pt[b, s * npg + i], 0, 0))

    per_b = lambda *shape: pl.BlockSpec((1,) + shape, lambda b, s, pt: (b,) + (0,) * len(shape))
    grid_spec = pltpu.PrefetchScalarGridSpec(
        num_scalar_prefetch=1,
        grid=(bd, n_steps),
        in_specs=[per_b(h, KV_LORA), per_b(h, LANES), per_b(1, KV_LORA), per_b(1, QK_ROPE)]
        + [page_spec((PAGE_SIZE, KV_LORA), i) for i in range(npg)]
        + [page_spec((QK_ROPE, PAGE_SIZE), i) for i in range(npg)],
        out_specs=per_b(h, KV_LORA),
        scratch_shapes=[pltpu.VMEM((h, LANES), F32), pltpu.VMEM((h, LANES), F32),
                        pltpu.VMEM((h, KV_LORA), F32)],
    )
    return pl.pallas_call(
        functools.partial(_decode_kernel, n_steps=n_steps),
        grid_spec=grid_spec,
        out_shape=jax.ShapeDtypeStruct((bd, h, KV_LORA), F32),
        compiler_params=_params("arbitrary", "arbitrary"),
        name="mla_decode",
    )(page_table, qlat, q, lat, krn, *([cache_ckv] * npg), *([cache_krope] * npg))


def _uv_kernel(olat_ref, wuv_ref, ya_ref):
    acc = None
    for h in range(MLA_HEADS):
        part = _dot(olat_ref[h].astype(BF16), wuv_ref[h])
        acc = part if acc is None else acc + part
    ya_ref[...] = acc


def _uv_project(olat, wuv_heads):
    h, n, c = olat.shape
    width = MLA_HEADS * V_HEAD
    return pl.pallas_call(
        _uv_kernel,
        grid=(1,),
        in_specs=[_full((h, n, c)), _full((h, c, width))],
        out_specs=_full((n, width)),
        out_shape=jax.ShapeDtypeStruct((n, width), F32),
        compiler_params=_params("arbitrary"),
        name="decode_uv",
    )(olat, wuv_heads)


def _mixout_kernel(x_ref, ycn_ref, ya_ref, ga_ref, wout_ref, gcross_ref, wmq_ref, h1_ref, qm_ref):
    yan = _rms(ya_ref[...], ga_ref[...]).astype(BF16)
    mixed = jnp.concatenate([ycn_ref[...], yan], axis=1)
    h1 = x_ref[...] + _dot(mixed, wout_ref[...])
    h1_ref[...] = h1
    qm_ref[...] = _dot(_rms(h1, gcross_ref[...]).astype(BF16), wmq_ref[...]).astype(BF16)


def _mixout(x, ycn, ya, w, tm):
    n, d = x.shape
    mix = CONV_WIDTH + MLA_HEADS * V_HEAD
    tok = lambda width: pl.BlockSpec((tm, width), lambda i: (i, 0))
    return pl.pallas_call(
        _mixout_kernel,
        grid=(n // tm,),
        in_specs=[tok(d), tok(CONV_WIDTH), tok(MLA_HEADS * V_HEAD), _full((1, MLA_HEADS * V_HEAD)),
                  _full((mix, d)), _full((1, d)), _full((d, d))],
        out_specs=[tok(d), tok(d)],
        out_shape=[jax.ShapeDtypeStruct((n, d), F32), jax.ShapeDtypeStruct((n, d), BF16)],
        compiler_params=_params("arbitrary"),
        name="mix_out",
    )(x, ycn, ya, w["g_attn_out"], w["w_out"], w["g_cross"], w["w_mq"])


def _cross_prompt_kernel(q_ref, k_ref, v_ref, o_ref):
    q = q_ref[0]
    outs = []
    for h in range(MEM_HEADS):
        sl = slice(h * MEM_HEAD_DIM, (h + 1) * MEM_HEAD_DIM)
        s = _dot_nt(q[:, sl], k_ref[0, :, sl].astype(BF16)) * MEM_SCALE
        e = jnp.exp(s - jnp.max(s, axis=1, keepdims=True))
        p = e / jnp.sum(e, axis=1, keepdims=True)
        outs.append(_dot(p.astype(BF16), v_ref[0, :, sl].astype(BF16)))
    o_ref[0] = jnp.concatenate(outs, axis=1).astype(BF16)


def _cross_prompt(qm, mk, mv, tm):
    b, s, d = qm.shape
    tok = pl.BlockSpec((1, tm, d), lambda i, j: (i, j, 0))
    mem = pl.BlockSpec((1, MEM_TOKENS, d), lambda i, j: (i, 0, 0))
    return pl.pallas_call(
        _cross_prompt_kernel,
        grid=(b, s // tm),
        in_specs=[tok, mem, mem],
        out_specs=tok,
        out_shape=jax.ShapeDtypeStruct((b, s, d), BF16),
        compiler_params=_params("arbitrary", "arbitrary"),
        name="cross_prompt",
    )(qm, mk, mv)


def _cross_sample_kernel(q_ref, k_ref, v_ref, o_ref, *, tb):
    for t in range(tb):
        s = jnp.sum(k_ref[t] * q_ref[t][None], axis=2, keepdims=True) * MEM_SCALE
        e = jnp.exp(s - jnp.max(s, axis=0, keepdims=True))
        p = e / jnp.sum(e, axis=0, keepdims=True)
        o_ref[t] = jnp.sum(p * v_ref[t], axis=0)


def _cross_sample(qm, mk, mv, tb):
    n = qm.shape[0]
    tok = pl.BlockSpec((tb, MEM_HEADS, MEM_HEAD_DIM), lambda i: (i, 0, 0))
    mem = pl.BlockSpec((tb, MEM_TOKENS, MEM_HEADS, MEM_HEAD_DIM), lambda i: (i, 0, 0, 0))
    return pl.pallas_call(
        functools.partial(_cross_sample_kernel, tb=tb),
        grid=(n // tb,),
        in_specs=[tok, mem, mem],
        out_specs=tok,
        out_shape=jax.ShapeDtypeStruct((n, MEM_HEADS, MEM_HEAD_DIM), F32),
        compiler_params=_params("arbitrary"),
        name="cross_sample",
    )(qm, mk, mv)


def _router_kernel(h1_ref, o_ref, wmo_ref, gffn_ref, wr_ref, br_ref,
                   h2_ref, xt_ref, gate_ref, route_ref, gw_ref, cnt_ref, cnt_sc):
    i = pl.program_id(0)

    @pl.when(i == 0)
    def _():
        cnt_sc[...] = jnp.zeros(cnt_sc.shape, F32)

    h2 = h1_ref[...] + _dot(o_ref[...], wmo_ref[...])
    h2_ref[...] = h2
    xt = _rms(h2, gffn_ref[...]).astype(BF16)
    xt_ref[...] = xt.astype(F32)
    logits = _dot(xt, wr_ref[...]) + br_ref[...]
    tm = logits.shape[0]
    lane = lax.broadcasted_iota(jnp.int32, logits.shape, 1)
    logits = jnp.where(lane < N_EXPERTS, logits, -jnp.inf)
    work = logits
    sel = lane < 0
    picks = []
    for k in range(TOP_K):
        m = jnp.max(work, axis=1, keepdims=True)
        idx = jnp.min(jnp.where(work == m, lane, LANES), axis=1, keepdims=True)
        hit = lane == idx
        sel = jnp.logical_or(sel, hit)
        work = jnp.where(hit, -jnp.inf, work)
        picks.append((m, idx, hit))
    top = picks[0][0]
    e = jnp.where(sel, jnp.exp(logits - top), 0.0)
    denom = jnp.sum(e, axis=1, keepdims=True)
    gate_ref[...] = e / denom

    sel_f = sel.astype(F32)
    earlier = (lax.broadcasted_iota(jnp.int32, (tm, tm), 0) > lax.broadcasted_iota(jnp.int32, (tm, tm), 1))
    rank_all = _dot(earlier.astype(BF16), sel_f.astype(BF16)) + cnt_sc[...]
    route = jnp.zeros(logits.shape, jnp.int32)
    gw = jnp.zeros(logits.shape, F32)
    for k, (m, idx, hit) in enumerate(picks):
        rank = jnp.sum(jnp.where(hit, rank_all, 0.0), axis=1, keepdims=True).astype(jnp.int32)
        route = jnp.where(lane == k, idx, route)
        route = jnp.where(lane == TOP_K + k, rank, route)
        gw = jnp.where(lane == k, jnp.exp(m - top) / denom, gw)
    route_ref[...] = route
    gw_ref[...] = gw
    cnt_sc[...] += jnp.sum(sel_f, axis=0, keepdims=True)
    cnt_ref[...] = cnt_sc[...]


def _router(h1, o, w, tm):
    n, d = h1.shape
    tok = lambda width: pl.BlockSpec((tm, width), lambda i: (i, 0))
    return pl.pallas_call(
        _router_kernel,
        grid=(n // tm,),
        in_specs=[tok(d), tok(d), _full((d, d)), _full((1, d)), _full((d, LANES)), _full((1, LANES))],
        out_specs=[tok(d), tok(d), tok(LANES), tok(LANES), tok(LANES), _full((1, LANES))],
        out_shape=[jax.ShapeDtypeStruct((n, d), F32), jax.ShapeDtypeStruct((n, d), F32),
                   jax.ShapeDtypeStruct((n, LANES), F32), jax.ShapeDtypeStruct((n, LANES), jnp.int32),
                   jax.ShapeDtypeStruct((n, LANES), F32), jax.ShapeDtypeStruct((1, LANES), F32)],
        scratch_shapes=[pltpu.VMEM((1, LANES), F32)],
        compiler_params=_params("arbitrary"),
        name="router",
    )(h1, o, w["w_mo"], w["g_ffn"], w["w_router"], w["b_router"])


SC_CORES = 2
SC_SUBCORES = 16
SC_WORKERS = SC_CORES * SC_SUBCORES
SC_CHUNK = 32
EXPERT_ROWS = 512


def _sc_gather(table, idx):
    b = idx.shape[0]
    d = table.shape[1]
    per_worker = b // SC_WORKERS
    n_chunks = per_worker // SC_CHUNK
    assert per_worker * SC_WORKERS == b and n_chunks * SC_CHUNK == per_worker
    mesh = plsc.VectorSubcoreMesh(core_axis_name="c", subcore_axis_name="s")

    @functools.partial(
        pl.kernel, mesh=mesh,
        out_type=jax.ShapeDtypeStruct((b, d), table.dtype),
        scratch_types=[pltpu.VMEM((SC_CHUNK,), jnp.int32), pltpu.VMEM((SC_CHUNK, d), table.dtype),
                       pltpu.SemaphoreType.DMA],
    )
    def gather(table_hbm, idx_hbm, out_hbm, idx_v, rows_v, sem):
        wid = lax.axis_index("s") * SC_CORES + lax.axis_index("c")
        base = wid * per_worker

        @pl.loop(0, n_chunks)
        def _(c):
            off = pl.multiple_of(base + c * SC_CHUNK, 8)
            pltpu.sync_copy(idx_hbm.at[pl.ds(off, SC_CHUNK)], idx_v)
            pltpu.async_copy(table_hbm.at[idx_v], rows_v, sem).wait()
            pltpu.sync_copy(rows_v, out_hbm.at[pl.ds(off, SC_CHUNK)])

    return gather(table, idx)


def _swiglu_expert(x, wgu, bgu, wd, bd):
    gu = _dot(x, wgu) + bgu
    g = jnp.minimum(gu[:, :D_FF], SWIGLU_LIMIT)
    u = jnp.clip(gu[:, D_FF:], -SWIGLU_LIMIT, SWIGLU_LIMIT)
    hdn = (u + 1.0) * (g * (1.0 / (1.0 + jnp.exp(-SWIGLU_ALPHA * g))))
    return _dot(hdn.astype(BF16), wd) + bd


def _experts_kernel(te_ref, tb_ref, used_ref, xs_ref, wgu_ref, bgu_ref, wd_ref, bd_ref, ys_ref):
    del te_ref, tb_ref

    @pl.when(pl.program_id(0) < used_ref[0])
    def _():
        ys_ref[...] = _swiglu_expert(xs_ref[...].astype(BF16), wgu_ref[0], bgu_ref[0], wd_ref[0], bd_ref[0])


def _experts(tile_expert, tile_block, used, xs, w):
    n_slots, d = xs.shape
    r = EXPERT_ROWS
    rows = pl.BlockSpec((r, d), lambda t, te, tb, u: (tb[t], 0))
    per_e = lambda *shape: pl.BlockSpec((1,) + shape, lambda t, te, tb, u: (te[t],) + (0,) * len(shape))
    grid_spec = pltpu.PrefetchScalarGridSpec(
        num_scalar_prefetch=3,
        grid=(n_slots // r,),
        in_specs=[rows, per_e(d, 2 * D_FF), per_e(1, 2 * D_FF), per_e(D_FF, d), per_e(1, d)],
        out_specs=rows,
    )
    return pl.pallas_call(
        _experts_kernel,
        grid_spec=grid_spec,
        out_shape=jax.ShapeDtypeStruct((n_slots, d), F32),
        compiler_params=_params("arbitrary"),
        name="experts",
    )(tile_expert, tile_block, used, xs, w["w_gate_up"], w["b_gate_up"], w["w_down"], w["b_down"])


def _combine_kernel(h2_ref, gw_ref, y0_ref, y1_ref, y2_ref, y3_ref, gfin_ref, y_ref):
    gw = gw_ref[...]
    moe = None
    for k, yk in enumerate((y0_ref, y1_ref, y2_ref, y3_ref)):
        part = gw[:, k:k + 1] * yk[0]
        moe = part if moe is None else moe + part
    y_ref[...] = _rms(h2_ref[...] + moe, gfin_ref[...])


def _combine(h2, gw, ys4, w, tm):
    n, d = h2.shape
    tok = lambda width: pl.BlockSpec((tm, width), lambda i: (i, 0))
    part = lambda k: pl.BlockSpec((1, tm, d), lambda i, k=k: (k, i, 0))
    return pl.pallas_call(
        _combine_kernel,
        grid=(n // tm,),
        in_specs=[tok(d), tok(LANES)] + [part(k) for k in range(TOP_K)] + [_full((1, d))],
        out_specs=tok(d),
        out_shape=jax.ShapeDtypeStruct((n, d), F32),
        compiler_params=_params("arbitrary"),
        name="moe_combine",
    )(h2, gw, *([ys4] * TOP_K), w["g_final"])


def _routed_moe(h2, xt, route, gw, counts, w):
    n, d = xt.shape
    r = EXPERT_ROWS
    n_tiles = (n * TOP_K) // r + N_EXPERTS
    n_slots = n_tiles * r
    cnt = counts[0, :N_EXPERTS].astype(jnp.int32)
    padded = ((cnt + r - 1) // r) * r
    ends = jnp.cumsum(padded)
    starts = ends - padded
    eid = route[:, 0:TOP_K]
    rank = route[:, TOP_K:2 * TOP_K]
    onehot = eid[:, :, None] == jnp.arange(N_EXPERTS, dtype=jnp.int32)[None, None, :]
    slot = jnp.sum(jnp.where(onehot, starts[None, None, :], 0), axis=-1) + rank
    token = jnp.broadcast_to(jnp.arange(n, dtype=jnp.int32)[:, None], (n, TOP_K))
    src = jnp.zeros((n_slots,), jnp.int32).at[slot.reshape(-1)].set(
        token.reshape(-1), unique_indices=True, mode="promise_in_bounds")
    used = (ends[-1] // r).astype(jnp.int32)
    tile_block = jnp.minimum(jnp.arange(n_tiles, dtype=jnp.int32), used - 1)
    tile_expert = jnp.minimum(
        jnp.sum((tile_block[:, None] >= (ends // r)[None, :]).astype(jnp.int32), axis=1), N_EXPERTS - 1)
    xs = _sc_gather(xt, src)
    ys = _experts(tile_expert, tile_block, used.reshape(1), xs, w)
    ys4 = _sc_gather(ys, slot.T.reshape(-1)).reshape(TOP_K, n, d)
    return _combine(h2, gw, ys4, w, tm=512)


def _moe_kernel(xt_ref, gate_ref, h2_ref, wgu_ref, bgu_ref, wd_ref, bd_ref, gfin_ref, y_ref, acc_sc):
    e = pl.program_id(1)

    @pl.when(e == 0)
    def _():
        acc_sc[...] = jnp.zeros(acc_sc.shape, F32)

    down = _swiglu_expert(xt_ref[...].astype(BF16), wgu_ref[0], bgu_ref[0], wd_ref[0], bd_ref[0])
    gate = gate_ref[...]
    lane = lax.broadcasted_iota(jnp.int32, gate.shape, 1)
    ge = jnp.sum(jnp.where(lane == e, gate, 0.0), axis=1, keepdims=True)
    acc_sc[...] += ge * down

    @pl.when(e == N_EXPERTS - 1)
    def _():
        y_ref[...] = _rms(h2_ref[...] + acc_sc[...], gfin_ref[...])


def _moe(xt, gate, h2, w, tm):
    n, d = xt.shape
    tok = lambda width: pl.BlockSpec((tm, width), lambda i, e: (i, 0))
    return pl.pallas_call(
        _moe_kernel,
        grid=(n // tm, N_EXPERTS),
        in_specs=[tok(d), tok(LANES), tok(d),
                  pl.BlockSpec((1, d, 2 * D_FF), lambda i, e: (e, 0, 0)),
                  pl.BlockSpec((1, 1, 2 * D_FF), lambda i, e: (e, 0, 0)),
                  pl.BlockSpec((1, D_FF, d), lambda i, e: (e, 0, 0)),
                  pl.BlockSpec((1, 1, d), lambda i, e: (e, 0, 0)),
                  _full((1, d))],
        out_specs=tok(d),
        out_shape=jax.ShapeDtypeStruct((n, d), F32),
        scratch_shapes=[pltpu.VMEM((tm, d), F32)],
        compiler_params=_params("arbitrary", "arbitrary"),
        name="moe",
    )(xt, gate, h2, w["w_gate_up"], w["b_gate_up"], w["w_down"], w["b_down"], w["g_final"])


def _rope_tables(pos):
    inv = ROPE_THETA ** (-jnp.arange(HALF_ROPE, dtype=F32) / HALF_ROPE)
    ang = pos.astype(F32)[:, None] * inv[None, :]
    cos, sin = jnp.cos(ang), jnp.sin(ang)
    n = pos.shape[0]
    pad = LANES - QK_NOPE - QK_ROPE
    rc = jnp.concatenate([jnp.ones((n, QK_NOPE), F32), cos, cos, jnp.zeros((n, pad), F32)], axis=1)
    ra = jnp.concatenate([jnp.zeros((n, QK_NOPE), F32), -sin, jnp.zeros((n, HALF_ROPE + pad), F32)], axis=1)
    rb = jnp.concatenate([jnp.zeros((n, QK_NOPE + HALF_ROPE), F32), sin, jnp.zeros((n, pad), F32)], axis=1)
    return rc, ra, rb


def _prepare(l, s_len, n_sample, past, g_mix, w_in, w_conv, g_q, w_uq, g_kv, w_uk, w_uv, g_conv_out, g_attn_out,
             w_out, g_cross, g_mem, w_mq, w_mk, w_mv, w_mo, g_ffn, w_router, b_router, w_gate_up, b_gate_up,
             w_down, b_down, g_final):
    h = MLA_HEADS
    row = lambda g: g.reshape(1, -1).astype(F32)
    head_pad = LANES - QK_NOPE - QK_ROPE
    win = w_in[l]
    win = jnp.concatenate([win[:, :OFF_KR], jnp.zeros((D_MODEL, QK_NOPE), F32), win[:, OFF_KR:],
                           jnp.zeros((D_MODEL, head_pad), F32)], axis=1)
    wuq = jnp.pad(w_uq[l].reshape(Q_LORA, h, QK_NOPE + QK_ROPE), ((0, 0), (0, 0), (0, head_pad)))
    wuk = jnp.pad(w_uk[l], ((0, 0), (0, 0), (0, LANES - QK_NOPE)))
    wuv_pair = w_uv[l].reshape(KV_LORA, h // 2, 2, V_HEAD)
    zeros = jnp.zeros((KV_LORA, h // 2, V_HEAD), F32)
    wuv = jnp.stack([jnp.concatenate([wuv_pair[:, :, 0], zeros], axis=-1),
                     jnp.concatenate([zeros, wuv_pair[:, :, 1]], axis=-1)], axis=2)
    wabs = jnp.pad(jnp.transpose(w_uk[l], (1, 2, 0)), ((0, 0), (0, LANES - QK_NOPE), (0, 0)))
    eye = jnp.eye(h, dtype=F32)
    wuv_heads = jnp.einsum("chd,hg->hcgd", w_uv[l], eye).reshape(h, KV_LORA, h * V_HEAD)
    wr = jnp.pad(w_router[l], ((0, 0), (0, LANES - N_EXPERTS)))
    br = jnp.pad(b_router[l], (0, LANES - N_EXPERTS)).reshape(1, LANES)
    return {
        "g_mix": row(g_mix[l]), "w_in": win.astype(BF16), "w_conv": w_conv[l].astype(F32),
        "g_q": row(g_q[l]), "w_uq": wuq.reshape(Q_LORA, h * LANES).astype(BF16),
        "g_kv": row(g_kv[l]), "w_uk": wuk.reshape(KV_LORA, h * LANES).astype(BF16),
        "w_uv": wuv.reshape(KV_LORA, h * LANES).astype(BF16), "w_abs": wabs.astype(BF16),
        "w_uv_heads": wuv_heads.astype(BF16),
        "g_conv_out": row(g_conv_out[l]), "g_attn_out": row(g_attn_out[l]),
        "w_out": w_out[l].astype(BF16), "g_cross": row(g_cross[l]), "g_mem": row(g_mem[l]),
        "w_mq": w_mq[l].astype(BF16), "w_mk": w_mk[l].astype(BF16), "w_mv": w_mv[l].astype(BF16),
        "w_mo": w_mo[l].astype(BF16), "g_ffn": row(g_ffn[l]),
        "w_router": wr.astype(BF16), "b_router": br.astype(F32),
        "w_gate_up": w_gate_up[l].astype(BF16), "b_gate_up": b_gate_up[l].reshape(N_EXPERTS, 1, 2 * D_FF),
        "w_down": w_down[l].astype(BF16), "b_down": b_down[l].reshape(N_EXPERTS, 1, D_MODEL),
        "g_final": row(g_final),
        "rope_prompt": _rope_tables(jnp.arange(s_len)),
        "rope_sample": _rope_tables(jnp.full((n_sample,), past)),
    }


def kernel(x_prompt, x_sample, mem_prompt, cache_conv, cache_ckv, cache_krope, cache_mem_k, cache_mem_v, page_table, g_mix, w_in, w_conv, g_q, w_uq, g_kv, w_uk, w_uv, g_conv_out, g_attn_out, w_out, g_cross, g_mem, w_mq, w_mk, w_mv, w_mo, g_ffn, w_router, b_router, w_gate_up, b_gate_up, w_down, b_down, g_final):
    bp, s_len, d = x_prompt.shape
    bd, t_len, _ = x_sample.shape
    depth = g_mix.shape[0]
    assert depth == 1 and t_len == 1, "kernel is written for one layer and one decode token per sequence"
    n_pages = page_table.shape[1]
    past = n_pages * PAGE_SIZE
    n_p = bp * s_len
    l = 0
    w = _prepare(l, s_len, bd, past, g_mix, w_in, w_conv, g_q, w_uq, g_kv, w_uk, w_uv, g_conv_out, g_attn_out,
                 w_out, g_cross, g_mem, w_mq, w_mk, w_mv, w_mo, g_ffn, w_router, b_router, w_gate_up,
                 b_gate_up, w_down, b_down, g_final)

    q, k, v, lat_p, kr_p, ycn_p, conv_p = _inproj_prompt(x_prompt, w, tm=512)
    ya_p = _attention(q, k, v, tq=512)
    mk_p, mv_p = _memory_kv(mem_prompt, w["g_mem"], w["w_mk"], w["w_mv"])
    xp = x_prompt.reshape(n_p, d)
    h1_p, qm_p = _mixout(xp, ycn_p.reshape(n_p, -1), ya_p.reshape(n_p, -1), w, tm=512)
    o_p = _cross_prompt(qm_p.reshape(bp, s_len, d), mk_p, mv_p, tm=512)
    h2_p, xt_p, _, route_p, gw_p, cnt_p = _router(h1_p, o_p.reshape(n_p, d), w, tm=512)
    y_p = _routed_moe(h2_p, xt_p, route_p, gw_p, cnt_p, w)

    xs = x_sample.reshape(bd, d)
    prev0 = cache_conv[l, :, 0, :]
    prev1 = cache_conv[l, :, 1, :]
    q_s, qlat_s, lat_s, kr_s, ycn_s, u_s = _inproj_sample(xs, prev0, prev1, w)
    olat = _decode(page_table, jnp.transpose(qlat_s, (1, 0, 2)), jnp.transpose(q_s, (1, 0, 2)),
                   lat_s.reshape(bd, 1, KV_LORA), kr_s.reshape(bd, 1, QK_ROPE), cache_ckv,
                   jnp.swapaxes(cache_krope, 2, 3))
    ya_s = _uv_project(jnp.transpose(olat, (1, 0, 2)), w["w_uv_heads"])
    h1_s, qm_s = _mixout(xs, ycn_s, ya_s, w, tm=bd)
    o_s = _cross_sample(qm_s.astype(F32).reshape(bd, MEM_HEADS, MEM_HEAD_DIM), cache_mem_k[l], cache_mem_v[l], tb=4)
    h2_s, xt_s, gate_s, _, _, _ = _router(h1_s, o_s.reshape(bd, d).astype(BF16), w, tm=bd)
    y_s = _moe(xt_s, gate_s, h2_s, w, tm=bd)

    mem_shape = (1, bp, MEM_TOKENS, MEM_HEADS, MEM_HEAD_DIM)
    return (y_p.reshape(bp, s_len, d), y_s.reshape(bd, 1, d),
            conv_p[None], lat_p[None], kr_p[None], mk_p.reshape(mem_shape), mv_p.reshape(mem_shape),
            jnp.stack([prev1, u_s], axis=1)[None], lat_s.reshape(1, bd, 1, KV_LORA),
            kr_s.reshape(1, bd, 1, QK_ROPE))
```

```python
import functools

import jax
import jax.numpy as jnp
from jax import lax
from jax.experimental import pallas as pl
from jax.experimental.pallas import tpu as pltpu
from jax.experimental.pallas import tpu_sc as plsc

D_MODEL = 1024
CONV_WIDTH = 512
CONV_K = 3
MLA_HEADS = 8
QK_NOPE = 64
QK_ROPE = 32
V_HEAD = 64
Q_LORA = 384
KV_LORA = 256
ROPE_THETA = 10000.0
PAGE_SIZE = 128
MEM_TOKENS = 256
MEM_HEADS = 4
MEM_HEAD_DIM = D_MODEL // MEM_HEADS
N_EXPERTS = 32
TOP_K = 4
D_FF = D_MODEL
SWIGLU_LIMIT = 7.0
SWIGLU_ALPHA = 1.702
NORM_EPS = 1e-6

LANES = 128
HALF_ROPE = QK_ROPE // 2
IN_WIDTH = 3 * CONV_WIDTH + Q_LORA + KV_LORA + QK_ROPE
IN_WIDTH_PAD = 3 * CONV_WIDTH + Q_LORA + KV_LORA + LANES
OFF_CQ = 3 * CONV_WIDTH
OFF_CKV = OFF_CQ + Q_LORA
OFF_KR = OFF_CKV + KV_LORA
MLA_SCALE = (QK_NOPE + QK_ROPE) ** -0.5
MEM_SCALE = MEM_HEAD_DIM ** -0.5
LOG2_E = 1.4426950408889634
VMEM_LIMIT = 48 * 1024 * 1024
PAGES_PER_STEP = 16
DECODE_SEQS = 1

BF16 = jnp.bfloat16
F32 = jnp.float32
NT_DIMS = (((1,), (1,)), ((), ()))


def _params(*sem):
    return pltpu.CompilerParams(dimension_semantics=sem, vmem_limit_bytes=VMEM_LIMIT)


def _rms(x, g):
    return x * lax.rsqrt(jnp.mean(x * x, axis=-1, keepdims=True) + NORM_EPS) * g


def _dot(a, b):
    return jnp.dot(a, b, preferred_element_type=F32)


def _dot_nt(a, b):
    return lax.dot_general(a, b, NT_DIMS, preferred_element_type=F32)


def _rope_group(x, rc, ra, rb):
    return x * rc + pltpu.roll(x, LANES - HALF_ROPE, 1) * ra + pltpu.roll(x, HALF_ROPE, 1) * rb


def _full(shape):
    return pl.BlockSpec(shape, lambda *_: (0,) * len(shape))


def _memkv_kernel(mem_ref, g_ref, wk_ref, wv_ref, k_ref, v_ref):
    m = _rms(mem_ref[0], g_ref[...]).astype(BF16)
    k_ref[0] = _dot(m, wk_ref[...])
    v_ref[0] = _dot(m, wv_ref[...])


def _memory_kv(mem, g_mem, w_mk, w_mv):
    b, n, d = mem.shape
    blk = pl.BlockSpec((1, n, d), lambda i: (i, 0, 0))
    return pl.pallas_call(
        _memkv_kernel,
        grid=(b,),
        in_specs=[blk, _full((1, d)), _full((d, d)), _full((d, d))],
        out_specs=[blk, blk],
        out_shape=[jax.ShapeDtypeStruct((b, n, d), F32)] * 2,
        compiler_params=_params("arbitrary"),
        name="memory_kv",
    )(mem, g_mem, w_mk, w_mv)


def _inproj_common(x, gmix, win, gq, wuq, gkv, rc, ra, rb):
    a = _rms(x, gmix).astype(BF16)
    z = _dot(a, win)
    b_g = z[:, 0:CONV_WIDTH]
    u = z[:, CONV_WIDTH:2 * CONV_WIDTH] * z[:, 2 * CONV_WIDTH:3 * CONV_WIDTH]
    cq = _rms(z[:, OFF_CQ:OFF_CKV], gq).astype(BF16)
    q = _dot(cq, wuq)
    q_heads = [_rope_group(q[:, h * LANES:(h + 1) * LANES], rc, ra, rb) for h in range(MLA_HEADS)]
    lat = _rms(z[:, OFF_CKV:OFF_KR], gkv)
    kr = _rope_group(z[:, OFF_KR:OFF_KR + LANES], rc, ra, rb)
    return b_g, u, q_heads, lat, kr


def _inproj_prompt_kernel(x_ref, gmix_ref, win_ref, wconv_ref, gq_ref, wuq_ref, gkv_ref, wuk_ref, wuv_ref,
                          gco_ref, rc_ref, ra_ref, rb_ref,
                          q_ref, k_ref, v_ref, lat_ref, kr_ref, ycn_ref, conv_ref, ubuf, *, tm):
    j = pl.program_id(1)
    b_g, u, q_heads, lat, kr = _inproj_common(
        x_ref[0], gmix_ref[...], win_ref[...], gq_ref[...], wuq_ref[...], gkv_ref[...],
        rc_ref[...], ra_ref[...], rb_ref[...])
    for h in range(MLA_HEADS):
        q_ref[0, h] = (q_heads[h] * (MLA_SCALE * LOG2_E)).astype(BF16)
    lat_ref[0] = lat
    kr_ref[0] = kr[:, QK_NOPE:QK_NOPE + QK_ROPE]
    lat_b = lat.astype(BF16)
    kn = _dot(lat_b, wuk_ref[...])
    v_t = _dot_nt(wuv_ref[...], lat_b)
    for h in range(MLA_HEADS):
        k_ref[0, h] = (kn[:, h * LANES:(h + 1) * LANES] + kr).astype(BF16)
        v_ref[0, h, 0] = v_t[h * LANES:(h + 1) * LANES, :].astype(BF16)

    @pl.when(j == 0)
    def _():
        ubuf[0:8, :] = jnp.zeros((8, CONV_WIDTH), F32)

    ubuf[8:8 + tm, :] = u
    u1 = ubuf[7:7 + tm, :]
    u2 = ubuf[6:6 + tm, :]
    wc = wconv_ref[...]
    yc = b_g * (wc[0:1, :] * u2 + wc[1:2, :] * u1 + wc[2:3, :] * u)
    ycn_ref[0] = _rms(yc, gco_ref[...]).astype(BF16)
    ubuf[0:8, :] = ubuf[tm:tm + 8, :]
    conv_ref[0] = u[tm - (CONV_K - 1):tm, :]


def _inproj_prompt(x, w, tm):
    b, s, d = x.shape
    h = MLA_HEADS
    tok = lambda n: pl.BlockSpec((1, tm, n), lambda i, j: (i, j, 0))
    head = pl.BlockSpec((1, h, tm, LANES), lambda i, j: (i, 0, j, 0))
    rope = pl.BlockSpec((tm, LANES), lambda i, j: (j, 0))
    qkv_shape = jax.ShapeDtypeStruct((b, h, s, LANES), BF16)
    return pl.pallas_call(
        functools.partial(_inproj_prompt_kernel, tm=tm),
        grid=(b, s // tm),
        in_specs=[tok(d), _full((1, d)), _full((d, IN_WIDTH_PAD)), _full((CONV_K, CONV_WIDTH)),
                  _full((1, Q_LORA)), _full((Q_LORA, h * LANES)), _full((1, KV_LORA)),
                  _full((KV_LORA, h * LANES)), _full((h * LANES, KV_LORA)), _full((1, CONV_WIDTH)),
                  rope, rope, rope],
        out_specs=[head, head, pl.BlockSpec((1, h, 1, LANES, tm), lambda i, j: (i, 0, j, 0, 0)),
                   tok(KV_LORA), tok(QK_ROPE), tok(CONV_WIDTH),
                   pl.BlockSpec((1, CONV_K - 1, CONV_WIDTH), lambda i, j: (i, 0, 0))],
        out_shape=[qkv_shape, qkv_shape, jax.ShapeDtypeStruct((b, h, s // tm, LANES, tm), BF16),
                   jax.ShapeDtypeStruct((b, s, KV_LORA), F32),
                   jax.ShapeDtypeStruct((b, s, QK_ROPE), F32),
                   jax.ShapeDtypeStruct((b, s, CONV_WIDTH), BF16),
                   jax.ShapeDtypeStruct((b, CONV_K - 1, CONV_WIDTH), F32)],
        scratch_shapes=[pltpu.VMEM((tm + 8, CONV_WIDTH), F32)],
        compiler_params=_params("arbitrary", "arbitrary"),
        name="inproj_prompt",
    )(x, w["g_mix"], w["w_in"], w["w_conv"], w["g_q"], w["w_uq"], w["g_kv"], w["w_uk"], w["w_uv"],
      w["g_conv_out"], *w["rope_prompt"])


def _inproj_sample_kernel(x_ref, gmix_ref, win_ref, wconv_ref, gq_ref, wuq_ref, gkv_ref, wabs_ref,
                          gco_ref, rc_ref, ra_ref, rb_ref, p0_ref, p1_ref,
                          q_ref, qlat_ref, lat_ref, kr_ref, ycn_ref, u_ref):
    b_g, u, q_heads, lat, kr = _inproj_common(
        x_ref[...], gmix_ref[...], win_ref[...], gq_ref[...], wuq_ref[...], gkv_ref[...],
        rc_ref[...], ra_ref[...], rb_ref[...])
    for h in range(MLA_HEADS):
        qh = q_heads[h].astype(BF16)
        q_ref[h] = qh
        qlat_ref[h] = _dot(qh, wabs_ref[h]).astype(BF16)
    lat_ref[...] = lat
    kr_ref[...] = kr[:, QK_NOPE:QK_NOPE + QK_ROPE]
    wc = wconv_ref[...]
    yc = b_g * (wc[0:1, :] * p0_ref[...] + wc[1:2, :] * p1_ref[...] + wc[2:3, :] * u)
    ycn_ref[...] = _rms(yc, gco_ref[...]).astype(BF16)
    u_ref[...] = u


def _inproj_sample(x, prev0, prev1, w):
    n, d = x.shape
    h = MLA_HEADS
    return pl.pallas_call(
        _inproj_sample_kernel,
        grid=(1,),
        in_specs=[_full((n, d)), _full((1, d)), _full((d, IN_WIDTH_PAD)), _full((CONV_K, CONV_WIDTH)),
                  _full((1, Q_LORA)), _full((Q_LORA, h * LANES)), _full((1, KV_LORA)),
                  _full((h, LANES, KV_LORA)), _full((1, CONV_WIDTH)),
                  _full((n, LANES)), _full((n, LANES)), _full((n, LANES)),
                  _full((n, CONV_WIDTH)), _full((n, CONV_WIDTH))],
        out_specs=[_full((h, n, LANES)), _full((h, n, KV_LORA)), _full((n, KV_LORA)), _full((n, QK_ROPE)),
                   _full((n, CONV_WIDTH)), _full((n, CONV_WIDTH))],
        out_shape=[jax.ShapeDtypeStruct((h, n, LANES), BF16),
                   jax.ShapeDtypeStruct((h, n, KV_LORA), BF16),
                   jax.ShapeDtypeStruct((n, KV_LORA), F32),
                   jax.ShapeDtypeStruct((n, QK_ROPE), F32),
                   jax.ShapeDtypeStruct((n, CONV_WIDTH), BF16),
                   jax.ShapeDtypeStruct((n, CONV_WIDTH), F32)],
        compiler_params=_params("arbitrary"),
        name="inproj_sample",
    )(x, w["g_mix"], w["w_in"], w["w_conv"], w["g_q"], w["w_uq"], w["g_kv"], w["w_abs"],
      w["g_conv_out"], *w["rope_sample"], prev0, prev1)


def _softmax_step(s, v, m_sc, l_sc, acc_sc):
    m_prev = m_sc[...]
    m_next = jnp.maximum(m_prev, jnp.max(s, axis=1, keepdims=True))
    p = jnp.exp(s - m_next[:, 0:1])
    alpha = jnp.exp(m_prev - m_next)
    l_sc[...] = alpha * l_sc[...] + jnp.sum(p, axis=1, keepdims=True)
    pv = _dot(p.astype(BF16), v)
    acc_sc[...] = acc_sc[...] * alpha[:, 0:1] + pv
    m_sc[...] = m_next


def _attn_kernel(q_ref, k_ref, v_ref, o_ref, m_sc, l_sc, acc_sc, *, tq):
    qi = pl.program_id(2)
    key = lax.broadcasted_iota(jnp.int32, (tq, tq), 0)
    qry = lax.broadcasted_iota(jnp.int32, (tq, tq), 1)
    m_sc[...] = jnp.full(m_sc.shape, -jnp.inf, F32)
    l_sc[...] = jnp.zeros(l_sc.shape, F32)
    acc_sc[...] = jnp.zeros(acc_sc.shape, F32)

    def step(j, masked):
        for hh in range(2):
            k = k_ref[0, hh, pl.ds(pl.multiple_of(j * tq, tq), tq), :]
            s = _dot_nt(k, q_ref[0, hh])
            if masked:
                s = jnp.where(key <= qry, s, -jnp.inf)
            m_prev = m_sc[hh]
            m_next = jnp.maximum(m_prev, jnp.max(s, axis=0, keepdims=True))
            p = jnp.exp2(s - m_next)
            alpha = jnp.exp2(m_prev - m_next)
            l_sc[hh] = alpha * l_sc[hh] + jnp.sum(p, axis=0, keepdims=True)
            acc_sc[hh] = acc_sc[hh] * alpha + _dot(v_ref[0, hh, j], p.astype(BF16))
            m_sc[hh] = m_next

    def body(j, carry):
        step(j, False)
        return carry

    lax.fori_loop(0, qi, body, 0)
    step(qi, True)
    o_ref[0] = acc_sc[0] / l_sc[0] + acc_sc[1] / l_sc[1]


def _attention(q, k, v_t, tq):
    b, h, s, _ = q.shape
    qspec = pl.BlockSpec((1, 2, tq, LANES), lambda i, p, j: (i, p, j, 0))
    kspec = pl.BlockSpec((1, 2, s, LANES), lambda i, p, j: (i, p, 0, 0))
    vspec = pl.BlockSpec((1, 2, s // tq, LANES, tq), lambda i, p, j: (i, p, 0, 0, 0))
    return pl.pallas_call(
        functools.partial(_attn_kernel, tq=tq),
        grid=(b, h // 2, s // tq),
        in_specs=[qspec, kspec, vspec],
        out_specs=pl.BlockSpec((1, LANES, tq), lambda i, p, j: (i, p, j)),
        out_shape=jax.ShapeDtypeStruct((b, h * V_HEAD, s), F32),
        scratch_shapes=[pltpu.VMEM((2, 1, tq), F32), pltpu.VMEM((2, 1, tq), F32), pltpu.VMEM((2, LANES, tq), F32)],
        compiler_params=_params("arbitrary", "arbitrary", "arbitrary"),
        name="mla_prompt_attention",
    )(q, k, v_t)


def _decode_kernel(pt_ref, qlat_ref, q_ref, lat_ref, krn_ref, *rest, n_steps):
    del pt_ref
    npg = PAGES_PER_STEP
    n_ops = DECODE_SEQS * npg
    ckv_refs = rest[:n_ops]
    kr_refs = rest[n_ops:2 * n_ops]
    o_ref, m_sc, l_sc, acc_sc = rest[2 * n_ops:]
    step = pl.program_id(1)

    @pl.when(step == 0)
    def _():
        m_sc[...] = jnp.full(m_sc.shape, -jnp.inf, F32)
        l_sc[...] = jnp.zeros(l_sc.shape, F32)
        acc_sc[...] = jnp.zeros(acc_sc.shape, F32)

    for e in range(DECODE_SEQS):
        ql = qlat_ref[e]
        qr = q_ref[e][:, QK_NOPE:QK_NOPE + QK_ROPE]
        ckv = jnp.concatenate([r[0, 0] for r in ckv_refs[e * npg:(e + 1) * npg]], axis=0).astype(BF16)
        kr_t = jnp.concatenate([r[0, 0] for r in kr_refs[e * npg:(e + 1) * npg]], axis=1).astype(BF16)
        s = (_dot_nt(ql, ckv) + _dot(qr, kr_t)) * MLA_SCALE
        _softmax_step(s, ckv, m_sc.at[e], l_sc.at[e], acc_sc.at[e])

    @pl.when(step == n_steps - 1)
    def _():
        for e in range(DECODE_SEQS):
            ql = qlat_ref[e]
            qr = q_ref[e][:, QK_NOPE:QK_NOPE + QK_ROPE]
            lat = lat_ref[e]
            s_new = (jnp.sum(ql.astype(F32) * lat, axis=1, keepdims=True)
                     + jnp.sum(qr.astype(F32) * krn_ref[e], axis=1, keepdims=True)) * MLA_SCALE
            m_prev = m_sc[e]
            m_next = jnp.maximum(m_prev, s_new)
            p_new = jnp.exp(s_new - m_next[:, 0:1])
            alpha = jnp.exp(m_prev - m_next)
            l_fin = alpha * l_sc[e] + p_new
            acc = acc_sc[e] * alpha[:, 0:1] + p_new * lat
            o_ref[e] = acc / l_fin[:, 0:1]


def _decode(page_table, qlat, q, lat, krn, cache_ckv, cache_krope):
    bd, n_pages = page_table.shape
    npg = PAGES_PER_STEP
    nseq = DECODE_SEQS
    n_steps = n_pages // npg
    h = MLA_HEADS

    def page_spec(shape, e, i):
        return pl.BlockSpec((1, 1) + shape, lambda b, s, pt, e=e, i=i: (0, pt[b * nseq + e, s * npg + i], 0, 0))

    per_b = lambda *shape: pl.BlockSpec((nseq,) + shape, lambda b, s, pt: (b,) + (0,) * len(shape))
    pages = [(e, i) for e in range(nseq) for i in range(npg)]
    grid_spec = pltpu.PrefetchScalarGridSpec(
        num_scalar_prefetch=1,
        grid=(bd // nseq, n_steps),
        in_specs=[per_b(h, KV_LORA), per_b(h, LANES), per_b(1, KV_LORA), per_b(1, QK_ROPE)]
        + [page_spec((PAGE_SIZE, KV_LORA), e, i) for e, i in pages]
        + [page_spec((QK_ROPE, PAGE_SIZE), e, i) for e, i in pages],
        out_specs=per_b(h, KV_LORA),
        scratch_shapes=[pltpu.VMEM((nseq, h, LANES), F32), pltpu.VMEM((nseq, h, LANES), F32),
                        pltpu.VMEM((nseq, h, KV_LORA), F32)],
    )
    return pl.pallas_call(
        functools.partial(_decode_kernel, n_steps=n_steps),
        grid_spec=grid_spec,
        out_shape=jax.ShapeDtypeStruct((bd, h, KV_LORA), F32),
        compiler_params=_params("arbitrary", "arbitrary"),
        name="mla_decode",
    )(page_table, qlat, q, lat, krn, *([cache_ckv] * len(pages)), *([cache_krope] * len(pages)))


def _uv_kernel(olat_ref, wuv_ref, ya_ref):
    acc = None
    for h in range(MLA_HEADS):
        part = _dot(olat_ref[h].astype(BF16), wuv_ref[h])
        acc = part if acc is None else acc + part
    ya_ref[...] = acc


def _uv_project(olat, wuv_heads):
    h, n, c = olat.shape
    width = MLA_HEADS * V_HEAD
    return pl.pallas_call(
        _uv_kernel,
        grid=(1,),
        in_specs=[_full((h, n, c)), _full((h, c, width))],
        out_specs=_full((n, width)),
        out_shape=jax.ShapeDtypeStruct((n, width), F32),
        compiler_params=_params("arbitrary"),
        name="decode_uv",
    )(olat, wuv_heads)


def _mixout_kernel(x_ref, ycn_ref, ya_ref, ga_ref, wout_ref, gcross_ref, wmq_ref, h1_ref, qm_ref, *, ya_transposed):
    ya = ya_ref[0].T if ya_transposed else ya_ref[...]
    yan = _rms(ya, ga_ref[...]).astype(BF16)
    mixed = jnp.concatenate([ycn_ref[...], yan], axis=1)
    h1 = x_ref[...] + _dot(mixed, wout_ref[...])
    h1_ref[...] = h1
    qm_ref[...] = _dot(_rms(h1, gcross_ref[...]).astype(BF16), wmq_ref[...]).astype(BF16)


def _mixout(x, ycn, ya, w, tm):
    n, d = x.shape
    width = MLA_HEADS * V_HEAD
    mix = CONV_WIDTH + width
    tok = lambda width: pl.BlockSpec((tm, width), lambda i: (i, 0))
    ya_transposed = ya.ndim == 3
    if ya_transposed:
        tiles = ya.shape[2] // tm
        ya_spec = pl.BlockSpec((1, width, tm), lambda i: (i // tiles, 0, i % tiles))
    else:
        ya_spec = tok(width)
    return pl.pallas_call(
        functools.partial(_mixout_kernel, ya_transposed=ya_transposed),
        grid=(n // tm,),
        in_specs=[tok(d), tok(CONV_WIDTH), ya_spec, _full((1, width)),
                  _full((mix, d)), _full((1, d)), _full((d, d))],
        out_specs=[tok(d), tok(d)],
        out_shape=[jax.ShapeDtypeStruct((n, d), F32), jax.ShapeDtypeStruct((n, d), BF16)],
        compiler_params=_params("arbitrary"),
        name="mix_out",
    )(x, ycn, ya, w["g_attn_out"], w["w_out"], w["g_cross"], w["w_mq"])


def _cross_prompt_kernel(q_ref, k_ref, v_ref, o_ref):
    q = q_ref[0]
    outs = []
    for h in range(MEM_HEADS):
        sl = slice(h * MEM_HEAD_DIM, (h + 1) * MEM_HEAD_DIM)
        s = _dot_nt(q[:, sl], k_ref[0, :, sl].astype(BF16)) * MEM_SCALE
        e = jnp.exp(s - jnp.max(s, axis=1, keepdims=True))
        p = e / jnp.sum(e, axis=1, keepdims=True)
        outs.append(_dot(p.astype(BF16), v_ref[0, :, sl].astype(BF16)))
    o_ref[0] = jnp.concatenate(outs, axis=1).astype(BF16)


def _cross_prompt(qm, mk, mv, tm):
    b, s, d = qm.shape
    tok = pl.BlockSpec((1, tm, d), lambda i, j: (i, j, 0))
    mem = pl.BlockSpec((1, MEM_TOKENS, d), lambda i, j: (i, 0, 0))
    return pl.pallas_call(
        _cross_prompt_kernel,
        grid=(b, s // tm),
        in_specs=[tok, mem, mem],
        out_specs=tok,
        out_shape=jax.ShapeDtypeStruct((b, s, d), BF16),
        compiler_params=_params("arbitrary", "arbitrary"),
        name="cross_prompt",
    )(qm, mk, mv)


def _cross_sample_kernel(q_ref, k_ref, v_ref, o_ref, *, tb):
    for t in range(tb):
        s = jnp.sum(k_ref[t] * q_ref[t][None], axis=2, keepdims=True) * MEM_SCALE
        e = jnp.exp(s - jnp.max(s, axis=0, keepdims=True))
        p = e / jnp.sum(e, axis=0, keepdims=True)
        o_ref[t] = jnp.sum(p * v_ref[t], axis=0)


def _cross_sample(qm, mk, mv, tb):
    n = qm.shape[0]
    tok = pl.BlockSpec((tb, MEM_HEADS, MEM_HEAD_DIM), lambda i: (i, 0, 0))
    mem = pl.BlockSpec((tb, MEM_TOKENS, MEM_HEADS, MEM_HEAD_DIM), lambda i: (i, 0, 0, 0))
    return pl.pallas_call(
        functools.partial(_cross_sample_kernel, tb=tb),
        grid=(n // tb,),
        in_specs=[tok, mem, mem],
        out_specs=tok,
        out_shape=jax.ShapeDtypeStruct((n, MEM_HEADS, MEM_HEAD_DIM), F32),
        compiler_params=_params("arbitrary"),
        name="cross_sample",
    )(qm, mk, mv)


def _router_kernel(h1_ref, o_ref, wmo_ref, gffn_ref, wr_ref, br_ref,
                   h2_ref, xt_ref, gate_ref, route_ref, gw_ref, cnt_ref, cnt_sc):
    i = pl.program_id(0)

    @pl.when(i == 0)
    def _():
        cnt_sc[...] = jnp.zeros(cnt_sc.shape, F32)

    h2 = h1_ref[...] + _dot(o_ref[...], wmo_ref[...])
    h2_ref[...] = h2
    xt = _rms(h2, gffn_ref[...]).astype(BF16)
    xt_ref[...] = _pack_rows(xt.astype(F32))
    logits = _dot(xt, wr_ref[...]) + br_ref[...]
    tm = logits.shape[0]
    lane = lax.broadcasted_iota(jnp.int32, logits.shape, 1)
    logits = jnp.where(lane < N_EXPERTS, logits, -jnp.inf)
    work = logits
    sel = lane < 0
    picks = []
    for k in range(TOP_K):
        m = jnp.max(work, axis=1, keepdims=True)
        idx = jnp.min(jnp.where(work == m, lane, LANES), axis=1, keepdims=True)
        hit = lane == idx
        sel = jnp.logical_or(sel, hit)
        work = jnp.where(hit, -jnp.inf, work)
        picks.append((m, idx, hit))
    top = picks[0][0]
    e = jnp.where(sel, jnp.exp(logits - top), 0.0)
    denom = jnp.sum(e, axis=1, keepdims=True)
    gate_ref[...] = e / denom

    sel_f = sel.astype(F32)
    earlier = (lax.broadcasted_iota(jnp.int32, (tm, tm), 0) > lax.broadcasted_iota(jnp.int32, (tm, tm), 1))
    rank_all = _dot(earlier.astype(BF16), sel_f.astype(BF16)) + cnt_sc[...]
    route = jnp.zeros(logits.shape, jnp.int32)
    gw = jnp.zeros(logits.shape, F32)
    for k, (m, idx, hit) in enumerate(picks):
        rank = jnp.sum(jnp.where(hit, rank_all, 0.0), axis=1, keepdims=True).astype(jnp.int32)
        route = jnp.where(lane == k, idx, route)
        route = jnp.where(lane == TOP_K + k, rank, route)
        gw = jnp.where(lane == k, jnp.exp(m - top) / denom, gw)
    route_ref[...] = route
    gw_ref[...] = gw
    cnt_sc[...] += jnp.sum(sel_f, axis=0, keepdims=True)
    cnt_ref[...] = cnt_sc[...]


def _router(h1, o, w, tm):
    n, d = h1.shape
    tok = lambda width: pl.BlockSpec((tm, width), lambda i: (i, 0))
    return pl.pallas_call(
        _router_kernel,
        grid=(n // tm,),
        in_specs=[tok(d), tok(d), _full((d, d)), _full((1, d)), _full((d, LANES)), _full((1, LANES))],
        out_specs=[tok(d), tok(d // 2), tok(LANES), tok(LANES), tok(LANES), _full((1, LANES))],
        out_shape=[jax.ShapeDtypeStruct((n, d), F32), jax.ShapeDtypeStruct((n, d // 2), jnp.uint32),
                   jax.ShapeDtypeStruct((n, LANES), F32), jax.ShapeDtypeStruct((n, LANES), jnp.int32),
                   jax.ShapeDtypeStruct((n, LANES), F32), jax.ShapeDtypeStruct((1, LANES), F32)],
        scratch_shapes=[pltpu.VMEM((1, LANES), F32)],
        compiler_params=_params("arbitrary"),
        name="router",
    )(h1, o, w["w_mo"], w["g_ffn"], w["w_router"], w["b_router"])


SC_CORES = 2
SC_SUBCORES = 16
SC_WORKERS = SC_CORES * SC_SUBCORES
SC_CHUNK = 32
EXPERT_ROWS = 512
EXPERTS_VMEM_LIMIT = 56 * 1024 * 1024


def _sc_gather(table, idx):
    b = idx.shape[0]
    d = table.shape[1]
    per_worker = b // SC_WORKERS
    n_chunks = per_worker // SC_CHUNK
    assert per_worker * SC_WORKERS == b and n_chunks * SC_CHUNK == per_worker
    mesh = plsc.VectorSubcoreMesh(core_axis_name="c", subcore_axis_name="s")

    assert n_chunks % 2 == 0
    row_buf = pltpu.VMEM((SC_CHUNK, d), table.dtype)

    @functools.partial(
        pl.kernel, mesh=mesh,
        out_type=jax.ShapeDtypeStruct((b, d), table.dtype),
        scratch_types=[pltpu.VMEM((per_worker,), jnp.int32), row_buf, row_buf] + [pltpu.SemaphoreType.DMA] * 4,
    )
    def gather(table_hbm, idx_hbm, out_hbm, idx_v, rows0, rows1, g0, g1, w0, w1):
        wid = lax.axis_index("s") * SC_CORES + lax.axis_index("c")
        base = pl.multiple_of(wid * per_worker, 8)
        rows, gsem, wsem = (rows0, rows1), (g0, g1), (w0, w1)
        pltpu.sync_copy(idx_hbm.at[pl.ds(base, per_worker)], idx_v)

        def fetch(c, buf):
            ids = idx_v.at[pl.ds(pl.multiple_of(c * SC_CHUNK, 8), SC_CHUNK)]
            return pltpu.make_async_copy(table_hbm.at[ids], rows[buf], gsem[buf])

        def flush(c, buf):
            dst = out_hbm.at[pl.ds(pl.multiple_of(base + c * SC_CHUNK, 8), SC_CHUNK)]
            return pltpu.make_async_copy(rows[buf], dst, wsem[buf])

        fetch(0, 0).start()

        @pl.loop(0, n_chunks, step=2)
        def _(c0):
            for buf in (0, 1):
                c = c0 + buf
                fetch(c, buf).wait()

                @pl.when(c + 1 < n_chunks)
                def _():
                    @pl.when(c >= 1)
                    def _():
                        flush(c - 1, 1 - buf).wait()

                    fetch(c + 1, 1 - buf).start()

                flush(c, buf).start()

        flush(n_chunks - 2, 0).wait()
        flush(n_chunks - 1, 1).wait()

    return gather(table, idx)


def _sc_scatter_rows(x, slot_rows, n_out):
    n, d = x.shape
    per_worker = n // SC_WORKERS
    n_chunks = per_worker // SC_CHUNK
    assert per_worker * SC_WORKERS == n and n_chunks * SC_CHUNK == per_worker and n_chunks % 2 == 0
    assert slot_rows.shape == (n // SC_CHUNK * TOP_K, SC_CHUNK)
    idx_rows = n_chunks * TOP_K
    mesh = plsc.VectorSubcoreMesh(core_axis_name="c", subcore_axis_name="s")
    row_buf = pltpu.VMEM((SC_CHUNK, d), x.dtype)

    @functools.partial(
        pl.kernel, mesh=mesh,
        out_type=jax.ShapeDtypeStruct((n_out, d), x.dtype),
        scratch_types=[pltpu.VMEM((idx_rows, SC_CHUNK), jnp.int32), row_buf, row_buf]
        + [pltpu.SemaphoreType.DMA] * 4,
    )
    def scatter(x_hbm, idx_hbm, out_hbm, idx_v, rows0, rows1, r0, r1, s0, s1):
        wid = lax.axis_index("s") * SC_CORES + lax.axis_index("c")
        base = pl.multiple_of(wid * per_worker, 8)
        rows, rsem, ssem = (rows0, rows1), (r0, r1), (s0, s1)
        pltpu.sync_copy(idx_hbm.at[pl.ds(pl.multiple_of(wid * idx_rows, 8), idx_rows)], idx_v)

        def fetch(c, buf):
            src = x_hbm.at[pl.ds(pl.multiple_of(base + c * SC_CHUNK, 8), SC_CHUNK)]
            return pltpu.make_async_copy(src, rows[buf], rsem[buf])

        def spread(c, k, buf):
            return pltpu.make_async_copy(rows[buf], out_hbm.at[idx_v.at[c * TOP_K + k]], ssem[buf])

        fetch(0, 0).start()

        @pl.loop(0, n_chunks, step=2)
        def _(c0):
            for buf in (0, 1):
                c = c0 + buf
                fetch(c, buf).wait()

                @pl.when(c + 1 < n_chunks)
                def _():
                    @pl.when(c >= 1)
                    def _():
                        for k in range(TOP_K):
                            spread(c - 1, k, 1 - buf).wait()

                    fetch(c + 1, 1 - buf).start()

                for k in range(TOP_K):
                    spread(c, k, buf).start()

        for k in range(TOP_K):
            spread(n_chunks - 2, k, 0).wait()
        for k in range(TOP_K):
            spread(n_chunks - 1, k, 1).wait()

    return scatter(x, slot_rows)


def _swiglu_expert(x, wgu, bgu, wd, bd):
    gu = _dot(x, wgu) + bgu
    g = jnp.minimum(gu[:, :D_FF], SWIGLU_LIMIT)
    u = jnp.clip(gu[:, D_FF:], -SWIGLU_LIMIT, SWIGLU_LIMIT)
    hdn = (u + 1.0) * (g * (1.0 / (1.0 + jnp.exp(-SWIGLU_ALPHA * g))))
    return _dot(hdn.astype(BF16), wd) + bd


def _pack_rows(x):
    bits = lax.bitcast_convert_type(x, jnp.uint32)
    n = x.shape[1] // 2
    return bits[:, :n] | (bits[:, n:] >> 16)


def _unpack_rows(p):
    hi = lax.bitcast_convert_type(p & jnp.uint32(0xFFFF0000), F32)
    lo = lax.bitcast_convert_type(p << 16, F32)
    return jnp.concatenate([hi, lo], axis=1).astype(BF16)


def _experts_kernel(te_ref, tb_ref, first_ref, used_ref, xs_ref, wgu_ref, bgu_ref, wd_ref, bd_ref, ys_ref,
                    wgu_sc, wd_sc):
    del te_ref, tb_ref
    t = pl.program_id(0)

    @pl.when(t < used_ref[0])
    def _():
        @pl.when(first_ref[t] == 1)
        def _():
            wgu_sc[...] = wgu_ref[0].astype(BF16)
            wd_sc[...] = wd_ref[0].astype(BF16)

        ys_ref[...] = _swiglu_expert(_unpack_rows(xs_ref[...]), wgu_sc[...], bgu_ref[0], wd_sc[...], bd_ref[0])


def _experts(tile_expert, tile_block, tile_first, used, xs, w):
    n_slots = xs.shape[0]
    d = D_MODEL
    r = EXPERT_ROWS
    rows = lambda width: pl.BlockSpec((r, width), lambda t, te, tb, tf, u: (tb[t], 0))
    per_e = lambda *shape: pl.BlockSpec((1,) + shape, lambda t, te, tb, tf, u: (te[t],) + (0,) * len(shape))
    grid_spec = pltpu.PrefetchScalarGridSpec(
        num_scalar_prefetch=4,
        grid=(n_slots // r,),
        in_specs=[rows(d // 2), per_e(d, 2 * D_FF), per_e(1, 2 * D_FF), per_e(D_FF, d), per_e(1, d)],
        out_specs=rows(d),
        scratch_shapes=[pltpu.VMEM((d, 2 * D_FF), BF16), pltpu.VMEM((D_FF, d), BF16)],
    )
    return pl.pallas_call(
        _experts_kernel,
        grid_spec=grid_spec,
        out_shape=jax.ShapeDtypeStruct((n_slots, d), F32),
        compiler_params=pltpu.CompilerParams(dimension_semantics=("arbitrary",),
                                             vmem_limit_bytes=EXPERTS_VMEM_LIMIT),
        name="experts",
    )(tile_expert, tile_block, tile_first, used, xs, w["w_gate_up"], w["b_gate_up"], w["w_down"], w["b_down"])


def _combine_kernel(h2_ref, gw_ref, y0_ref, y1_ref, y2_ref, y3_ref, gfin_ref, y_ref):
    gw = gw_ref[...]
    moe = None
    for k, yk in enumerate((y0_ref, y1_ref, y2_ref, y3_ref)):
        part = gw[:, k:k + 1] * yk[0]
        moe = part if moe is None else moe + part
    y_ref[...] = _rms(h2_ref[...] + moe, gfin_ref[...])


def _combine(h2, gw, ys4, w, tm):
    n, d = h2.shape
    tok = lambda width: pl.BlockSpec((tm, width), lambda i: (i, 0))
    part = lambda k: pl.BlockSpec((1, tm, d), lambda i, k=k: (k, i, 0))
    return pl.pallas_call(
        _combine_kernel,
        grid=(n // tm,),
        in_specs=[tok(d), tok(LANES)] + [part(k) for k in range(TOP_K)] + [_full((1, d))],
        out_specs=tok(d),
        out_shape=jax.ShapeDtypeStruct((n, d), F32),
        compiler_params=_params("arbitrary"),
        name="moe_combine",
    )(h2, gw, *([ys4] * TOP_K), w["g_final"])


def _routed_moe(h2, xt, route, gw, counts, w):
    n, d = h2.shape
    r = EXPERT_ROWS
    n_tiles = (n * TOP_K) // r + N_EXPERTS
    n_slots = n_tiles * r
    cnt = counts[0, :N_EXPERTS].astype(jnp.int32)
    padded = ((cnt + r - 1) // r) * r
    ends = jnp.cumsum(padded)
    starts = ends - padded
    eid = route[:, 0:TOP_K]
    rank = route[:, TOP_K:2 * TOP_K]
    onehot = eid[:, :, None] == jnp.arange(N_EXPERTS, dtype=jnp.int32)[None, None, :]
    slot = jnp.sum(jnp.where(onehot, starts[None, None, :], 0), axis=-1) + rank
    used = (ends[-1] // r).astype(jnp.int32)
    tile_block = jnp.minimum(jnp.arange(n_tiles, dtype=jnp.int32), used - 1)
    tile_expert = jnp.minimum(
        jnp.sum((tile_block[:, None] >= (ends // r)[None, :]).astype(jnp.int32), axis=1), N_EXPERTS - 1)
    tile_first = jnp.concatenate([jnp.ones((1,), jnp.int32),
                                  (tile_expert[1:] != tile_expert[:-1]).astype(jnp.int32)])
    slot_rows = slot.reshape(n // SC_CHUNK, SC_CHUNK, TOP_K).transpose(0, 2, 1).reshape(-1, SC_CHUNK)
    xs = _sc_scatter_rows(xt, slot_rows, n_slots)
    ys = _experts(tile_expert, tile_block, tile_first, used.reshape(1), xs, w)
    ys4 = _sc_gather(ys, slot.T.reshape(-1)).reshape(TOP_K, n, d)
    return _combine(h2, gw, ys4, w, tm=512)


def _moe_kernel(xt_ref, gate_ref, h2_ref, wgu_ref, bgu_ref, wd_ref, bd_ref, gfin_ref, y_ref, acc_sc):
    e = pl.program_id(1)

    @pl.when(e == 0)
    def _():
        acc_sc[...] = jnp.zeros(acc_sc.shape, F32)

    down = _swiglu_expert(_unpack_rows(xt_ref[...]), wgu_ref[0].astype(BF16), bgu_ref[0],
                          wd_ref[0].astype(BF16), bd_ref[0])
    gate = gate_ref[...]
    lane = lax.broadcasted_iota(jnp.int32, gate.shape, 1)
    ge = jnp.sum(jnp.where(lane == e, gate, 0.0), axis=1, keepdims=True)
    acc_sc[...] += ge * down

    @pl.when(e == N_EXPERTS - 1)
    def _():
        y_ref[...] = _rms(h2_ref[...] + acc_sc[...], gfin_ref[...])


def _moe(xt, gate, h2, w, tm):
    n, d = h2.shape
    tok = lambda width: pl.BlockSpec((tm, width), lambda i, e: (i, 0))
    return pl.pallas_call(
        _moe_kernel,
        grid=(n // tm, N_EXPERTS),
        in_specs=[tok(d // 2), tok(LANES), tok(d),
                  pl.BlockSpec((1, d, 2 * D_FF), lambda i, e: (e, 0, 0)),
                  pl.BlockSpec((1, 1, 2 * D_FF), lambda i, e: (e, 0, 0)),
                  pl.BlockSpec((1, D_FF, d), lambda i, e: (e, 0, 0)),
                  pl.BlockSpec((1, 1, d), lambda i, e: (e, 0, 0)),
                  _full((1, d))],
        out_specs=tok(d),
        out_shape=jax.ShapeDtypeStruct((n, d), F32),
        scratch_shapes=[pltpu.VMEM((tm, d), F32)],
        compiler_params=_params("arbitrary", "arbitrary"),
        name="moe",
    )(xt, gate, h2, w["w_gate_up"], w["b_gate_up"], w["w_down"], w["b_down"], w["g_final"])


def _rope_tables(pos):
    inv = ROPE_THETA ** (-jnp.arange(HALF_ROPE, dtype=F32) / HALF_ROPE)
    ang = pos.astype(F32)[:, None] * inv[None, :]
    cos, sin = jnp.cos(ang), jnp.sin(ang)
    n = pos.shape[0]
    pad = LANES - QK_NOPE - QK_ROPE
    rc = jnp.concatenate([jnp.ones((n, QK_NOPE), F32), cos, cos, jnp.zeros((n, pad), F32)], axis=1)
    ra = jnp.concatenate([jnp.zeros((n, QK_NOPE), F32), -sin, jnp.zeros((n, HALF_ROPE + pad), F32)], axis=1)
    rb = jnp.concatenate([jnp.zeros((n, QK_NOPE + HALF_ROPE), F32), sin, jnp.zeros((n, pad), F32)], axis=1)
    return rc, ra, rb


def _prepare(l, s_len, n_sample, past, g_mix, w_in, w_conv, g_q, w_uq, g_kv, w_uk, w_uv, g_conv_out, g_attn_out,
             w_out, g_cross, g_mem, w_mq, w_mk, w_mv, w_mo, g_ffn, w_router, b_router, w_gate_up, b_gate_up,
             w_down, b_down, g_final):
    h = MLA_HEADS
    row = lambda g: g.reshape(1, -1).astype(F32)
    head_pad = LANES - QK_NOPE - QK_ROPE
    win = w_in[l]
    win = jnp.concatenate([win[:, :OFF_KR], jnp.zeros((D_MODEL, QK_NOPE), F32), win[:, OFF_KR:],
                           jnp.zeros((D_MODEL, head_pad), F32)], axis=1)
    wuq = jnp.pad(w_uq[l].reshape(Q_LORA, h, QK_NOPE + QK_ROPE), ((0, 0), (0, 0), (0, head_pad)))
    wuk = jnp.pad(w_uk[l], ((0, 0), (0, 0), (0, LANES - QK_NOPE)))
    wuv_pair = w_uv[l].reshape(KV_LORA, h // 2, 2, V_HEAD)
    zeros = jnp.zeros((KV_LORA, h // 2, V_HEAD), F32)
    wuv = jnp.stack([jnp.concatenate([wuv_pair[:, :, 0], zeros], axis=-1),
                     jnp.concatenate([zeros, wuv_pair[:, :, 1]], axis=-1)], axis=2)
    wabs = jnp.pad(jnp.transpose(w_uk[l], (1, 2, 0)), ((0, 0), (0, LANES - QK_NOPE), (0, 0)))
    eye = jnp.eye(h, dtype=F32)
    wuv_heads = jnp.einsum("chd,hg->hcgd", w_uv[l], eye).reshape(h, KV_LORA, h * V_HEAD)
    wr = jnp.pad(w_router[l], ((0, 0), (0, LANES - N_EXPERTS)))
    br = jnp.pad(b_router[l], (0, LANES - N_EXPERTS)).reshape(1, LANES)
    return {
        "g_mix": row(g_mix[l]), "w_in": win.astype(BF16), "w_conv": w_conv[l].astype(F32),
        "g_q": row(g_q[l]), "w_uq": wuq.reshape(Q_LORA, h * LANES).astype(BF16),
        "g_kv": row(g_kv[l]), "w_uk": wuk.reshape(KV_LORA, h * LANES).astype(BF16),
        "w_uv": wuv.reshape(KV_LORA, h * LANES).T.astype(BF16), "w_abs": wabs.astype(BF16),
        "w_uv_heads": wuv_heads.astype(BF16),
        "g_conv_out": row(g_conv_out[l]), "g_attn_out": row(g_attn_out[l]),
        "w_out": w_out[l].astype(BF16), "g_cross": row(g_cross[l]), "g_mem": row(g_mem[l]),
        "w_mq": w_mq[l].astype(BF16), "w_mk": w_mk[l].astype(BF16), "w_mv": w_mv[l].astype(BF16),
        "w_mo": w_mo[l].astype(BF16), "g_ffn": row(g_ffn[l]),
        "w_router": wr.astype(BF16), "b_router": br.astype(F32),
        "w_gate_up": w_gate_up[l], "b_gate_up": b_gate_up[l].reshape(N_EXPERTS, 1, 2 * D_FF),
        "w_down": w_down[l], "b_down": b_down[l].reshape(N_EXPERTS, 1, D_MODEL),
        "g_final": row(g_final),
        "rope_prompt": _rope_tables(jnp.arange(s_len)),
        "rope_sample": _rope_tables(jnp.full((n_sample,), past)),
    }


def kernel(x_prompt, x_sample, mem_prompt, cache_conv, cache_ckv, cache_krope, cache_mem_k, cache_mem_v, page_table, g_mix, w_in, w_conv, g_q, w_uq, g_kv, w_uk, w_uv, g_conv_out, g_attn_out, w_out, g_cross, g_mem, w_mq, w_mk, w_mv, w_mo, g_ffn, w_router, b_router, w_gate_up, b_gate_up, w_down, b_down, g_final):
    bp, s_len, d = x_prompt.shape
    bd, t_len, _ = x_sample.shape
    depth = g_mix.shape[0]
    assert depth == 1 and t_len == 1, "kernel is written for one layer and one decode token per sequence"
    n_pages = page_table.shape[1]
    past = n_pages * PAGE_SIZE
    n_p = bp * s_len
    l = 0
    w = _prepare(l, s_len, bd, past, g_mix, w_in, w_conv, g_q, w_uq, g_kv, w_uk, w_uv, g_conv_out, g_attn_out,
                 w_out, g_cross, g_mem, w_mq, w_mk, w_mv, w_mo, g_ffn, w_router, b_router, w_gate_up,
                 b_gate_up, w_down, b_down, g_final)

    q, k, v, lat_p, kr_p, ycn_p, conv_p = _inproj_prompt(x_prompt, w, tm=512)
    ya_p = _attention(q, k, v, tq=512)
    mk_p, mv_p = _memory_kv(mem_prompt, w["g_mem"], w["w_mk"], w["w_mv"])
    xp = x_prompt.reshape(n_p, d)
    h1_p, qm_p = _mixout(xp, ycn_p.reshape(n_p, -1), ya_p, w, tm=512)
    o_p = _cross_prompt(qm_p.reshape(bp, s_len, d), mk_p, mv_p, tm=512)
    h2_p, xt_p, _, route_p, gw_p, cnt_p = _router(h1_p, o_p.reshape(n_p, d), w, tm=512)
    y_p = _routed_moe(h2_p, xt_p, route_p, gw_p, cnt_p, w)

    xs = x_sample.reshape(bd, d)
    prev0 = cache_conv[l, :, 0, :]
    prev1 = cache_conv[l, :, 1, :]
    q_s, qlat_s, lat_s, kr_s, ycn_s, u_s = _inproj_sample(xs, prev0, prev1, w)
    olat = _decode(page_table, jnp.transpose(qlat_s, (1, 0, 2)), jnp.transpose(q_s, (1, 0, 2)),
                   lat_s.reshape(bd, 1, KV_LORA), kr_s.reshape(bd, 1, QK_ROPE), cache_ckv,
                   jnp.swapaxes(cache_krope, 2, 3))
    ya_s = _uv_project(jnp.transpose(olat, (1, 0, 2)), w["w_uv_heads"])
    h1_s, qm_s = _mixout(xs, ycn_s, ya_s, w, tm=bd)
    o_s = _cross_sample(qm_s.astype(F32).reshape(bd, MEM_HEADS, MEM_HEAD_DIM), cache_mem_k[l], cache_mem_v[l], tb=4)
    h2_s, xt_s, gate_s, _, _, _ = _router(h1_s, o_s.reshape(bd, d).astype(BF16), w, tm=bd)
    y_s = _moe(xt_s, gate_s, h2_s, w, tm=bd)

    mem_shape = (1, bp, MEM_TOKENS, MEM_HEADS, MEM_HEAD_DIM)
    return (y_p.reshape(bp, s_len, d), y_s.reshape(bd, 1, d),
            conv_p[None], lat_p[None], kr_p[None], mk_p.reshape(mem_shape), mv_p.reshape(mem_shape),
            jnp.stack([prev1, u_s], axis=1)[None], lat_s.reshape(1, bd, 1, KV_LORA),
            kr_s.reshape(1, bd, 1, QK_ROPE))
```

```python
import functools

import jax
import jax.numpy as jnp
from jax import lax
from jax.experimental import pallas as pl
from jax.experimental.pallas import tpu as pltpu
from jax.experimental.pallas import tpu_sc as plsc

D_MODEL = 1024
CONV_WIDTH = 512
CONV_K = 3
MLA_HEADS = 8
QK_NOPE = 64
QK_ROPE = 32
V_HEAD = 64
Q_LORA = 384
KV_LORA = 256
ROPE_THETA = 10000.0
PAGE_SIZE = 128
MEM_TOKENS = 256
MEM_HEADS = 4
MEM_HEAD_DIM = D_MODEL // MEM_HEADS
N_EXPERTS = 32
TOP_K = 4
D_FF = D_MODEL
SWIGLU_LIMIT = 7.0
SWIGLU_ALPHA = 1.702
NORM_EPS = 1e-6

LANES = 128
HALF_ROPE = QK_ROPE // 2
IN_WIDTH = 3 * CONV_WIDTH + Q_LORA + KV_LORA + QK_ROPE
IN_WIDTH_PAD = 3 * CONV_WIDTH + Q_LORA + KV_LORA + LANES
OFF_CQ = 3 * CONV_WIDTH
OFF_CKV = OFF_CQ + Q_LORA
OFF_KR = OFF_CKV + KV_LORA
MLA_SCALE = (QK_NOPE + QK_ROPE) ** -0.5
MEM_SCALE = MEM_HEAD_DIM ** -0.5
LOG2_E = 1.4426950408889634
VMEM_LIMIT = 48 * 1024 * 1024
PAGES_PER_STEP = 16

BF16 = jnp.bfloat16
F32 = jnp.float32
NT_DIMS = (((1,), (1,)), ((), ()))


def _params(*sem):
    return pltpu.CompilerParams(dimension_semantics=sem, vmem_limit_bytes=VMEM_LIMIT)


def _rms(x, g):
    return x * lax.rsqrt(jnp.mean(x * x, axis=-1, keepdims=True) + NORM_EPS) * g


def _dot(a, b):
    return jnp.dot(a, b, preferred_element_type=F32)


def _dot_nt(a, b):
    return lax.dot_general(a, b, NT_DIMS, preferred_element_type=F32)


def _rope_group(x, rc, ra, rb):
    return x * rc + pltpu.roll(x, LANES - HALF_ROPE, 1) * ra + pltpu.roll(x, HALF_ROPE, 1) * rb


def _full(shape):
    return pl.BlockSpec(shape, lambda *_: (0,) * len(shape))


def _memkv_kernel(mem_ref, g_ref, wk_ref, wv_ref, k_ref, v_ref):
    m = _rms(mem_ref[0], g_ref[...]).astype(BF16)
    k_ref[0] = _dot(m, wk_ref[...])
    v_ref[0] = _dot(m, wv_ref[...])


def _memory_kv(mem, g_mem, w_mk, w_mv):
    b, n, d = mem.shape
    blk = pl.BlockSpec((1, n, d), lambda i: (i, 0, 0))
    return pl.pallas_call(
        _memkv_kernel,
        grid=(b,),
        in_specs=[blk, _full((1, d)), _full((d, d)), _full((d, d))],
        out_specs=[blk, blk],
        out_shape=[jax.ShapeDtypeStruct((b, n, d), F32)] * 2,
        compiler_params=_params("arbitrary"),
        name="memory_kv",
    )(mem, g_mem, w_mk, w_mv)


def _inproj_common(x, gmix, win, gq, wuq, gkv, rc, ra, rb):
    a = _rms(x, gmix).astype(BF16)
    z = _dot(a, win)
    b_g = z[:, 0:CONV_WIDTH]
    u = z[:, CONV_WIDTH:2 * CONV_WIDTH] * z[:, 2 * CONV_WIDTH:3 * CONV_WIDTH]
    cq = _rms(z[:, OFF_CQ:OFF_CKV], gq).astype(BF16)
    q = _dot(cq, wuq)
    q_heads = [_rope_group(q[:, h * LANES:(h + 1) * LANES], rc, ra, rb) for h in range(MLA_HEADS)]
    lat = _rms(z[:, OFF_CKV:OFF_KR], gkv)
    kr = _rope_group(z[:, OFF_KR:OFF_KR + LANES], rc, ra, rb)
    return b_g, u, q_heads, lat, kr


def _inproj_prompt_kernel(x_ref, gmix_ref, win_ref, wconv_ref, gq_ref, wuq_ref, gkv_ref, wuk_ref, wuv_ref,
                          gco_ref, rc_ref, ra_ref, rb_ref,
                          q_ref, k_ref, v_ref, lat_ref, kr_ref, ycn_ref, conv_ref, ubuf, *, tm):
    j = pl.program_id(1)
    b_g, u, q_heads, lat, kr = _inproj_common(
        x_ref[0], gmix_ref[...], win_ref[...], gq_ref[...], wuq_ref[...], gkv_ref[...],
        rc_ref[...], ra_ref[...], rb_ref[...])
    for h in range(MLA_HEADS):
        q_ref[0, h] = (q_heads[h] * (MLA_SCALE * LOG2_E)).astype(BF16)
    lat_ref[0] = lat
    kr_ref[0] = kr[:, QK_NOPE:QK_NOPE + QK_ROPE]
    lat_b = lat.astype(BF16)
    kn = _dot(lat_b, wuk_ref[...])
    v_t = _dot_nt(wuv_ref[...], lat_b)
    for h in range(MLA_HEADS):
        k_ref[0, h] = (kn[:, h * LANES:(h + 1) * LANES] + kr).astype(BF16)
        v_ref[0, h, 0] = v_t[h * LANES:(h + 1) * LANES, :].astype(BF16)

    @pl.when(j == 0)
    def _():
        ubuf[0:8, :] = jnp.zeros((8, CONV_WIDTH), F32)

    ubuf[8:8 + tm, :] = u
    u1 = ubuf[7:7 + tm, :]
    u2 = ubuf[6:6 + tm, :]
    wc = wconv_ref[...]
    yc = b_g * (wc[0:1, :] * u2 + wc[1:2, :] * u1 + wc[2:3, :] * u)
    ycn_ref[0] = _rms(yc, gco_ref[...]).astype(BF16)
    ubuf[0:8, :] = ubuf[tm:tm + 8, :]
    conv_ref[0] = u[tm - (CONV_K - 1):tm, :]


def _inproj_prompt(x, w, tm):
    b, s, d = x.shape
    h = MLA_HEADS
    tok = lambda n: pl.BlockSpec((1, tm, n), lambda i, j: (i, j, 0))
    head = pl.BlockSpec((1, h, tm, LANES), lambda i, j: (i, 0, j, 0))
    rope = pl.BlockSpec((tm, LANES), lambda i, j: (j, 0))
    qkv_shape = jax.ShapeDtypeStruct((b, h, s, LANES), BF16)
    return pl.pallas_call(
        functools.partial(_inproj_prompt_kernel, tm=tm),
        grid=(b, s // tm),
        in_specs=[tok(d), _full((1, d)), _full((d, IN_WIDTH_PAD)), _full((CONV_K, CONV_WIDTH)),
                  _full((1, Q_LORA)), _full((Q_LORA, h * LANES)), _full((1, KV_LORA)),
                  _full((KV_LORA, h * LANES)), _full((h * LANES, KV_LORA)), _full((1, CONV_WIDTH)),
                  rope, rope, rope],
        out_specs=[head, head, pl.BlockSpec((1, h, 1, LANES, tm), lambda i, j: (i, 0, j, 0, 0)),
                   tok(KV_LORA), tok(QK_ROPE), tok(CONV_WIDTH),
                   pl.BlockSpec((1, CONV_K - 1, CONV_WIDTH), lambda i, j: (i, 0, 0))],
        out_shape=[qkv_shape, qkv_shape, jax.ShapeDtypeStruct((b, h, s // tm, LANES, tm), BF16),
                   jax.ShapeDtypeStruct((b, s, KV_LORA), F32),
                   jax.ShapeDtypeStruct((b, s, QK_ROPE), F32),
                   jax.ShapeDtypeStruct((b, s, CONV_WIDTH), BF16),
                   jax.ShapeDtypeStruct((b, CONV_K - 1, CONV_WIDTH), F32)],
        scratch_shapes=[pltpu.VMEM((tm + 8, CONV_WIDTH), F32)],
        compiler_params=_params("arbitrary", "arbitrary"),
        name="inproj_prompt",
    )(x, w["g_mix"], w["w_in"], w["w_conv"], w["g_q"], w["w_uq"], w["g_kv"], w["w_uk"], w["w_uv"],
      w["g_conv_out"], *w["rope_prompt"])


def _inproj_sample_kernel(x_ref, gmix_ref, win_ref, wconv_ref, gq_ref, wuq_ref, gkv_ref, wabs_ref,
                          gco_ref, rc_ref, ra_ref, rb_ref, p0_ref, p1_ref,
                          q_ref, qlat_ref, lat_ref, kr_ref, ycn_ref, u_ref):
    b_g, u, q_heads, lat, kr = _inproj_common(
        x_ref[...], gmix_ref[...], win_ref[...], gq_ref[...], wuq_ref[...], gkv_ref[...],
        rc_ref[...], ra_ref[...], rb_ref[...])
    for h in range(MLA_HEADS):
        qh = q_heads[h].astype(BF16)
        q_ref[h] = qh
        qlat_ref[h] = _dot(qh, wabs_ref[h]).astype(BF16)
    lat_ref[...] = lat
    kr_ref[...] = kr[:, QK_NOPE:QK_NOPE + QK_ROPE]
    wc = wconv_ref[...]
    yc = b_g * (wc[0:1, :] * p0_ref[...] + wc[1:2, :] * p1_ref[...] + wc[2:3, :] * u)
    ycn_ref[...] = _rms(yc, gco_ref[...]).astype(BF16)
    u_ref[...] = u


def _inproj_sample(x, prev0, prev1, w):
    n, d = x.shape
    h = MLA_HEADS
    return pl.pallas_call(
        _inproj_sample_kernel,
        grid=(1,),
        in_specs=[_full((n, d)), _full((1, d)), _full((d, IN_WIDTH_PAD)), _full((CONV_K, CONV_WIDTH)),
                  _full((1, Q_LORA)), _full((Q_LORA, h * LANES)), _full((1, KV_LORA)),
                  _full((h, LANES, KV_LORA)), _full((1, CONV_WIDTH)),
                  _full((n, LANES)), _full((n, LANES)), _full((n, LANES)),
                  _full((n, CONV_WIDTH)), _full((n, CONV_WIDTH))],
        out_specs=[_full((h, n, LANES)), _full((h, n, KV_LORA)), _full((n, KV_LORA)), _full((n, QK_ROPE)),
                   _full((n, CONV_WIDTH)), _full((n, CONV_WIDTH))],
        out_shape=[jax.ShapeDtypeStruct((h, n, LANES), BF16),
                   jax.ShapeDtypeStruct((h, n, KV_LORA), BF16),
                   jax.ShapeDtypeStruct((n, KV_LORA), F32),
                   jax.ShapeDtypeStruct((n, QK_ROPE), F32),
                   jax.ShapeDtypeStruct((n, CONV_WIDTH), BF16),
                   jax.ShapeDtypeStruct((n, CONV_WIDTH), F32)],
        compiler_params=_params("arbitrary"),
        name="inproj_sample",
    )(x, w["g_mix"], w["w_in"], w["w_conv"], w["g_q"], w["w_uq"], w["g_kv"], w["w_abs"],
      w["g_conv_out"], *w["rope_sample"], prev0, prev1)


def _softmax_step(s, v, m_sc, l_sc, acc_sc):
    m_prev = m_sc[...]
    m_next = jnp.maximum(m_prev, jnp.max(s, axis=1, keepdims=True))
    p = jnp.exp(s - m_next[:, 0:1])
    alpha = jnp.exp(m_prev - m_next)
    l_sc[...] = alpha * l_sc[...] + jnp.sum(p, axis=1, keepdims=True)
    pv = _dot(p.astype(BF16), v)
    acc_sc[...] = acc_sc[...] * alpha[:, 0:1] + pv
    m_sc[...] = m_next


def _attn_kernel(q_ref, k_ref, v_ref, o_ref, m_sc, l_sc, acc_sc, *, tq):
    qi = pl.program_id(2)
    key = lax.broadcasted_iota(jnp.int32, (tq, tq), 0)
    qry = lax.broadcasted_iota(jnp.int32, (tq, tq), 1)
    m_sc[...] = jnp.full(m_sc.shape, -jnp.inf, F32)
    l_sc[...] = jnp.zeros(l_sc.shape, F32)
    acc_sc[...] = jnp.zeros(acc_sc.shape, F32)

    def step(j, masked):
        for hh in range(2):
            k = k_ref[0, hh, pl.ds(pl.multiple_of(j * tq, tq), tq), :]
            s = _dot_nt(k, q_ref[0, hh])
            if masked:
                s = jnp.where(key <= qry, s, -jnp.inf)
            m_prev = m_sc[hh]
            m_next = jnp.maximum(m_prev, jnp.max(s, axis=0, keepdims=True))
            p = jnp.exp2(s - m_next)
            alpha = jnp.exp2(m_prev - m_next)
            l_sc[hh] = alpha * l_sc[hh] + jnp.sum(p, axis=0, keepdims=True)
            acc_sc[hh] = acc_sc[hh] * alpha + _dot(v_ref[0, hh, j], p.astype(BF16))
            m_sc[hh] = m_next

    def body(j, carry):
        step(j, False)
        return carry

    lax.fori_loop(0, qi, body, 0)
    step(qi, True)
    o_ref[0] = acc_sc[0] / l_sc[0] + acc_sc[1] / l_sc[1]


def _attention(q, k, v_t, tq):
    b, h, s, _ = q.shape
    qspec = pl.BlockSpec((1, 2, tq, LANES), lambda i, p, j: (i, p, j, 0))
    kspec = pl.BlockSpec((1, 2, s, LANES), lambda i, p, j: (i, p, 0, 0))
    vspec = pl.BlockSpec((1, 2, s // tq, LANES, tq), lambda i, p, j: (i, p, 0, 0, 0))
    return pl.pallas_call(
        functools.partial(_attn_kernel, tq=tq),
        grid=(b, h // 2, s // tq),
        in_specs=[qspec, kspec, vspec],
        out_specs=pl.BlockSpec((1, LANES, tq), lambda i, p, j: (i, p, j)),
        out_shape=jax.ShapeDtypeStruct((b, h * V_HEAD, s), F32),
        scratch_shapes=[pltpu.VMEM((2, 1, tq), F32), pltpu.VMEM((2, 1, tq), F32), pltpu.VMEM((2, LANES, tq), F32)],
        compiler_params=_params("arbitrary", "arbitrary", "arbitrary"),
        name="mla_prompt_attention",
    )(q, k, v_t)


def _decode_kernel(pt_ref, qlat_ref, q_ref, lat_ref, krn_ref, ckv_hbm, kr_hbm, o_ref,
                   ckv_buf, kr_buf, sems, m_sc, l_sc, acc_sc, *, n_steps, n_seqs):
    npg = PAGES_PER_STEP
    b = pl.program_id(0)
    step = pl.program_id(1)
    t = b * n_steps + step
    slot = t % 2

    def start_pages(seq, st, sl):
        for i in range(npg):
            page = pt_ref[seq, st * npg + i]
            pltpu.make_async_copy(ckv_hbm.at[0, page], ckv_buf.at[sl, i], sems.at[0, sl]).start()
            pltpu.make_async_copy(kr_hbm.at[0, page], kr_buf.at[sl, i], sems.at[1, sl]).start()

    def wait_pages(sl):
        pltpu.make_async_copy(ckv_hbm.at[0, pl.ds(0, npg)], ckv_buf.at[sl], sems.at[0, sl]).wait()
        pltpu.make_async_copy(kr_hbm.at[0, pl.ds(0, npg)], kr_buf.at[sl], sems.at[1, sl]).wait()

    @pl.when(t == 0)
    def _():
        start_pages(0, 0, 0)

    @pl.when(t + 1 < n_seqs * n_steps)
    def _():
        nxt = t + 1
        start_pages(nxt // n_steps, nxt % n_steps, 1 - slot)

    @pl.when(step == 0)
    def _():
        m_sc[...] = jnp.full(m_sc.shape, -jnp.inf, F32)
        l_sc[...] = jnp.zeros(l_sc.shape, F32)
        acc_sc[...] = jnp.zeros(acc_sc.shape, F32)

    ql = qlat_ref[0]
    qr = q_ref[0][:, QK_NOPE:QK_NOPE + QK_ROPE]
    wait_pages(slot)
    ckv = ckv_buf[slot].reshape(npg * PAGE_SIZE, KV_LORA).astype(BF16)
    kr_t = jnp.concatenate([kr_buf[slot, i] for i in range(npg)], axis=1).astype(BF16)
    s = (_dot_nt(ql, ckv) + _dot(qr, kr_t)) * MLA_SCALE
    _softmax_step(s, ckv, m_sc, l_sc, acc_sc)

    @pl.when(step == n_steps - 1)
    def _():
        lat = lat_ref[0]
        s_new = (jnp.sum(ql.astype(F32) * lat, axis=1, keepdims=True)
                 + jnp.sum(qr.astype(F32) * krn_ref[0], axis=1, keepdims=True)) * MLA_SCALE
        m_prev = m_sc[...]
        m_next = jnp.maximum(m_prev, s_new)
        p_new = jnp.exp(s_new - m_next[:, 0:1])
        alpha = jnp.exp(m_prev - m_next)
        l_fin = alpha * l_sc[...] + p_new
        acc = acc_sc[...] * alpha[:, 0:1] + p_new * lat
        o_ref[0] = acc / l_fin[:, 0:1]


def _decode(page_table, qlat, q, lat, krn, cache_ckv, cache_krope):
    bd, n_pages = page_table.shape
    npg = PAGES_PER_STEP
    n_steps = n_pages // npg
    h = MLA_HEADS
    per_b = lambda *shape: pl.BlockSpec((1,) + shape, lambda b, s, pt: (b,) + (0,) * len(shape))
    hbm = pl.BlockSpec(memory_space=pl.ANY)
    grid_spec = pltpu.PrefetchScalarGridSpec(
        num_scalar_prefetch=1,
        grid=(bd, n_steps),
        in_specs=[per_b(h, KV_LORA), per_b(h, LANES), per_b(1, KV_LORA), per_b(1, QK_ROPE), hbm, hbm],
        out_specs=per_b(h, KV_LORA),
        scratch_shapes=[pltpu.VMEM((2, npg, PAGE_SIZE, KV_LORA), F32), pltpu.VMEM((2, npg, QK_ROPE, PAGE_SIZE), F32),
                        pltpu.SemaphoreType.DMA((2, 2)),
                        pltpu.VMEM((h, LANES), F32), pltpu.VMEM((h, LANES), F32), pltpu.VMEM((h, KV_LORA), F32)],
    )
    return pl.pallas_call(
        functools.partial(_decode_kernel, n_steps=n_steps, n_seqs=bd),
        grid_spec=grid_spec,
        out_shape=jax.ShapeDtypeStruct((bd, h, KV_LORA), F32),
        compiler_params=_params("arbitrary", "arbitrary"),
        name="mla_decode",
    )(page_table, qlat, q, lat, krn, cache_ckv, cache_krope)


def _uv_kernel(olat_ref, wuv_ref, ya_ref):
    acc = None
    for h in range(MLA_HEADS):
        part = _dot(olat_ref[h].astype(BF16), wuv_ref[h])
        acc = part if acc is None else acc + part
    ya_ref[...] = acc


def _uv_project(olat, wuv_heads):
    h, n, c = olat.shape
    width = MLA_HEADS * V_HEAD
    return pl.pallas_call(
        _uv_kernel,
        grid=(1,),
        in_specs=[_full((h, n, c)), _full((h, c, width))],
        out_specs=_full((n, width)),
        out_shape=jax.ShapeDtypeStruct((n, width), F32),
        compiler_params=_params("arbitrary"),
        name="decode_uv",
    )(olat, wuv_heads)


def _mixout_math(x, ycn, ya, ga, wout, gcross, wmq):
    yan = _rms(ya, ga).astype(BF16)
    mixed = jnp.concatenate([ycn, yan], axis=1)
    h1 = x + _dot(mixed, wout)
    return h1, _dot(_rms(h1, gcross).astype(BF16), wmq).astype(BF16)


def _mixout_kernel(x_ref, ycn_ref, ya_ref, ga_ref, wout_ref, gcross_ref, wmq_ref, h1_ref, qm_ref):
    h1_ref[...], qm_ref[...] = _mixout_math(x_ref[...], ycn_ref[...], ya_ref[...], ga_ref[...], wout_ref[...],
                                            gcross_ref[...], wmq_ref[...])


def _mixout(x, ycn, ya, w, tm):
    n, d = x.shape
    width = MLA_HEADS * V_HEAD
    mix = CONV_WIDTH + width
    tok = lambda width: pl.BlockSpec((tm, width), lambda i: (i, 0))
    return pl.pallas_call(
        _mixout_kernel,
        grid=(n // tm,),
        in_specs=[tok(d), tok(CONV_WIDTH), tok(width), _full((1, width)),
                  _full((mix, d)), _full((1, d)), _full((d, d))],
        out_specs=[tok(d), tok(d)],
        out_shape=[jax.ShapeDtypeStruct((n, d), F32), jax.ShapeDtypeStruct((n, d), BF16)],
        compiler_params=_params("arbitrary"),
        name="mix_out",
    )(x, ycn, ya, w["g_attn_out"], w["w_out"], w["g_cross"], w["w_mq"])


def _cross_math(q, k_ref, v_ref):
    outs = []
    for h in range(MEM_HEADS):
        sl = slice(h * MEM_HEAD_DIM, (h + 1) * MEM_HEAD_DIM)
        s = _dot_nt(q[:, sl], k_ref[0, :, sl].astype(BF16)) * MEM_SCALE
        e = jnp.exp(s - jnp.max(s, axis=1, keepdims=True))
        p = e / jnp.sum(e, axis=1, keepdims=True)
        outs.append(_dot(p.astype(BF16), v_ref[0, :, sl].astype(BF16)))
    return jnp.concatenate(outs, axis=1).astype(BF16)


def _cross_sample_kernel(q_ref, k_ref, v_ref, o_ref, *, tb):
    for t in range(tb):
        s = jnp.sum(k_ref[t] * q_ref[t][None], axis=2, keepdims=True) * MEM_SCALE
        e = jnp.exp(s - jnp.max(s, axis=0, keepdims=True))
        p = e / jnp.sum(e, axis=0, keepdims=True)
        o_ref[t] = jnp.sum(p * v_ref[t], axis=0)


def _cross_sample(qm, mk, mv, tb):
    n = qm.shape[0]
    tok = pl.BlockSpec((tb, MEM_HEADS, MEM_HEAD_DIM), lambda i: (i, 0, 0))
    mem = pl.BlockSpec((tb, MEM_TOKENS, MEM_HEADS, MEM_HEAD_DIM), lambda i: (i, 0, 0, 0))
    return pl.pallas_call(
        functools.partial(_cross_sample_kernel, tb=tb),
        grid=(n // tb,),
        in_specs=[tok, mem, mem],
        out_specs=tok,
        out_shape=jax.ShapeDtypeStruct((n, MEM_HEADS, MEM_HEAD_DIM), F32),
        compiler_params=_params("arbitrary"),
        name="cross_sample",
    )(qm, mk, mv)


def _router_math(h1, o, wmo, gffn, wr, br, cnt_prev):
    h2 = h1 + _dot(o, wmo)
    xt = _rms(h2, gffn).astype(BF16)
    logits = _dot(xt, wr) + br
    tm = logits.shape[0]
    lane = lax.broadcasted_iota(jnp.int32, logits.shape, 1)
    logits = jnp.where(lane < N_EXPERTS, logits, -jnp.inf)
    work = logits
    sel = lane < 0
    picks = []
    for k in range(TOP_K):
        m = jnp.max(work, axis=1, keepdims=True)
        idx = jnp.min(jnp.where(work == m, lane, LANES), axis=1, keepdims=True)
        hit = lane == idx
        sel = jnp.logical_or(sel, hit)
        work = jnp.where(hit, -jnp.inf, work)
        picks.append((m, idx, hit))
    top = picks[0][0]
    e = jnp.where(sel, jnp.exp(logits - top), 0.0)
    denom = jnp.sum(e, axis=1, keepdims=True)

    sel_f = sel.astype(F32)
    earlier = (lax.broadcasted_iota(jnp.int32, (tm, tm), 0) > lax.broadcasted_iota(jnp.int32, (tm, tm), 1))
    rank_all = _dot(earlier.astype(BF16), sel_f.astype(BF16)) + cnt_prev
    route = jnp.zeros(logits.shape, jnp.int32)
    gw = jnp.zeros(logits.shape, F32)
    for k, (m, idx, hit) in enumerate(picks):
        rank = jnp.sum(jnp.where(hit, rank_all, 0.0), axis=1, keepdims=True).astype(jnp.int32)
        route = jnp.where(lane == k, idx, route)
        route = jnp.where(lane == TOP_K + k, rank, route)
        gw = jnp.where(lane == k, jnp.exp(m - top) / denom, gw)
    cnt = cnt_prev + jnp.sum(sel_f, axis=0, keepdims=True)
    return h2, _pack_rows(xt.astype(F32)), e / denom, route, gw, cnt


def _router_kernel(h1_ref, o_ref, wmo_ref, gffn_ref, wr_ref, br_ref,
                   h2_ref, xt_ref, gate_ref, route_ref, gw_ref, cnt_ref, cnt_sc):
    @pl.when(pl.program_id(0) == 0)
    def _():
        cnt_sc[...] = jnp.zeros(cnt_sc.shape, F32)

    h2_ref[...], xt_ref[...], gate_ref[...], route_ref[...], gw_ref[...], cnt = _router_math(
        h1_ref[...], o_ref[...], wmo_ref[...], gffn_ref[...], wr_ref[...], br_ref[...], cnt_sc[...])
    cnt_sc[...] = cnt
    cnt_ref[...] = cnt


def _post_prompt_kernel(x_ref, ycn_ref, yat_ref, k_ref, v_ref, ga_ref, wout_ref, gcross_ref, wmq_ref,
                        wmo_ref, gffn_ref, wr_ref, br_ref, h2_ref, xt_ref, route_ref, gw_ref, cnt_ref, cnt_sc):
    @pl.when(pl.program_id(0) == 0)
    def _():
        cnt_sc[...] = jnp.zeros(cnt_sc.shape, F32)

    h1, qm = _mixout_math(x_ref[...], ycn_ref[...], yat_ref[0].T, ga_ref[...], wout_ref[...], gcross_ref[...],
                          wmq_ref[...])
    o = _cross_math(qm, k_ref, v_ref)
    h2_ref[...], xt_ref[...], _, route_ref[...], gw_ref[...], cnt = _router_math(
        h1, o, wmo_ref[...], gffn_ref[...], wr_ref[...], br_ref[...], cnt_sc[...])
    cnt_sc[...] = cnt
    cnt_ref[...] = cnt


def _post_prompt(x, ycn, ya_t, mk, mv, w, tm):
    n, d = x.shape
    width = MLA_HEADS * V_HEAD
    tiles = ya_t.shape[2] // tm
    tok = lambda width: pl.BlockSpec((tm, width), lambda i: (i, 0))
    mem = pl.BlockSpec((1, MEM_TOKENS, d), lambda i: (i // tiles, 0, 0))
    return pl.pallas_call(
        _post_prompt_kernel,
        grid=(n // tm,),
        in_specs=[tok(d), tok(CONV_WIDTH), pl.BlockSpec((1, width, tm), lambda i: (i // tiles, 0, i % tiles)),
                  mem, mem, _full((1, width)), _full((CONV_WIDTH + width, d)), _full((1, d)), _full((d, d)),
                  _full((d, d)), _full((1, d)), _full((d, LANES)), _full((1, LANES))],
        out_specs=[tok(d), tok(d // 2), tok(LANES), tok(LANES), _full((1, LANES))],
        out_shape=[jax.ShapeDtypeStruct((n, d), F32), jax.ShapeDtypeStruct((n, d // 2), jnp.uint32),
                   jax.ShapeDtypeStruct((n, LANES), jnp.int32), jax.ShapeDtypeStruct((n, LANES), F32),
                   jax.ShapeDtypeStruct((1, LANES), F32)],
        scratch_shapes=[pltpu.VMEM((1, LANES), F32)],
        compiler_params=_params("arbitrary"),
        name="post_prompt",
    )(x, ycn, ya_t, mk, mv, w["g_attn_out"], w["w_out"], w["g_cross"], w["w_mq"], w["w_mo"], w["g_ffn"],
      w["w_router"], w["b_router"])


def _router(h1, o, w, tm):
    n, d = h1.shape
    tok = lambda width: pl.BlockSpec((tm, width), lambda i: (i, 0))
    return pl.pallas_call(
        _router_kernel,
        grid=(n // tm,),
        in_specs=[tok(d), tok(d), _full((d, d)), _full((1, d)), _full((d, LANES)), _full((1, LANES))],
        out_specs=[tok(d), tok(d // 2), tok(LANES), tok(LANES), tok(LANES), _full((1, LANES))],
        out_shape=[jax.ShapeDtypeStruct((n, d), F32), jax.ShapeDtypeStruct((n, d // 2), jnp.uint32),
                   jax.ShapeDtypeStruct((n, LANES), F32), jax.ShapeDtypeStruct((n, LANES), jnp.int32),
                   jax.ShapeDtypeStruct((n, LANES), F32), jax.ShapeDtypeStruct((1, LANES), F32)],
        scratch_shapes=[pltpu.VMEM((1, LANES), F32)],
        compiler_params=_params("arbitrary"),
        name="router",
    )(h1, o, w["w_mo"], w["g_ffn"], w["w_router"], w["b_router"])


SC_CORES = 2
SC_SUBCORES = 16
SC_WORKERS = SC_CORES * SC_SUBCORES
SC_CHUNK = 32
EXPERT_ROWS = 512
EXPERTS_VMEM_LIMIT = 56 * 1024 * 1024


def _sc_gather(table, idx):
    b = idx.shape[0]
    d = table.shape[1]
    per_worker = b // SC_WORKERS
    n_chunks = per_worker // SC_CHUNK
    assert per_worker * SC_WORKERS == b and n_chunks * SC_CHUNK == per_worker
    mesh = plsc.VectorSubcoreMesh(core_axis_name="c", subcore_axis_name="s")

    assert n_chunks % 2 == 0
    row_buf = pltpu.VMEM((SC_CHUNK, d), table.dtype)

    @functools.partial(
        pl.kernel, mesh=mesh,
        out_type=jax.ShapeDtypeStruct((b, d), table.dtype),
        scratch_types=[pltpu.VMEM((per_worker,), jnp.int32), row_buf, row_buf] + [pltpu.SemaphoreType.DMA] * 4,
    )
    def gather(table_hbm, idx_hbm, out_hbm, idx_v, rows0, rows1, g0, g1, w0, w1):
        wid = lax.axis_index("s") * SC_CORES + lax.axis_index("c")
        base = pl.multiple_of(wid * per_worker, 8)
        rows, gsem, wsem = (rows0, rows1), (g0, g1), (w0, w1)
        pltpu.sync_copy(idx_hbm.at[pl.ds(base, per_worker)], idx_v)

        def fetch(c, buf):
            ids = idx_v.at[pl.ds(pl.multiple_of(c * SC_CHUNK, 8), SC_CHUNK)]
            return pltpu.make_async_copy(table_hbm.at[ids], rows[buf], gsem[buf])

        def flush(c, buf):
            dst = out_hbm.at[pl.ds(pl.multiple_of(base + c * SC_CHUNK, 8), SC_CHUNK)]
            return pltpu.make_async_copy(rows[buf], dst, wsem[buf])

        fetch(0, 0).start()

        @pl.loop(0, n_chunks, step=2)
        def _(c0):
            for buf in (0, 1):
                c = c0 + buf
                fetch(c, buf).wait()

                @pl.when(c + 1 < n_chunks)
                def _():
                    @pl.when(c >= 1)
                    def _():
                        flush(c - 1, 1 - buf).wait()

                    fetch(c + 1, 1 - buf).start()

                flush(c, buf).start()

        flush(n_chunks - 2, 0).wait()
        flush(n_chunks - 1, 1).wait()

    return gather(table, idx)


def _sc_scatter_rows(x, slot_rows, n_out):
    n, d = x.shape
    per_worker = n // SC_WORKERS
    n_chunks = per_worker // SC_CHUNK
    assert per_worker * SC_WORKERS == n and n_chunks * SC_CHUNK == per_worker and n_chunks % 2 == 0
    assert slot_rows.shape == (n // SC_CHUNK * TOP_K, SC_CHUNK)
    idx_rows = n_chunks * TOP_K
    mesh = plsc.VectorSubcoreMesh(core_axis_name="c", subcore_axis_name="s")
    row_buf = pltpu.VMEM((SC_CHUNK, d), x.dtype)

    @functools.partial(
        pl.kernel, mesh=mesh,
        out_type=jax.ShapeDtypeStruct((n_out, d), x.dtype),
        scratch_types=[pltpu.VMEM((idx_rows, SC_CHUNK), jnp.int32), row_buf, row_buf]
        + [pltpu.SemaphoreType.DMA] * 4,
    )
    def scatter(x_hbm, idx_hbm, out_hbm, idx_v, rows0, rows1, r0, r1, s0, s1):
        wid = lax.axis_index("s") * SC_CORES + lax.axis_index("c")
        base = pl.multiple_of(wid * per_worker, 8)
        rows, rsem, ssem = (rows0, rows1), (r0, r1), (s0, s1)
        pltpu.sync_copy(idx_hbm.at[pl.ds(pl.multiple_of(wid * idx_rows, 8), idx_rows)], idx_v)

        def fetch(c, buf):
            src = x_hbm.at[pl.ds(pl.multiple_of(base + c * SC_CHUNK, 8), SC_CHUNK)]
            return pltpu.make_async_copy(src, rows[buf], rsem[buf])

        def spread(c, k, buf):
            return pltpu.make_async_copy(rows[buf], out_hbm.at[idx_v.at[c * TOP_K + k]], ssem[buf])

        fetch(0, 0).start()

        @pl.loop(0, n_chunks, step=2)
        def _(c0):
            for buf in (0, 1):
                c = c0 + buf
                fetch(c, buf).wait()

                @pl.when(c + 1 < n_chunks)
                def _():
                    @pl.when(c >= 1)
                    def _():
                        for k in range(TOP_K):
                            spread(c - 1, k, 1 - buf).wait()

                    fetch(c + 1, 1 - buf).start()

                for k in range(TOP_K):
                    spread(c, k, buf).start()

        for k in range(TOP_K):
            spread(n_chunks - 2, k, 0).wait()
        for k in range(TOP_K):
            spread(n_chunks - 1, k, 1).wait()

    return scatter(x, slot_rows)


def _swiglu_expert(x, wgu, bgu, wd, bd):
    gu = _dot(x, wgu) + bgu
    g = jnp.minimum(gu[:, :D_FF], SWIGLU_LIMIT)
    u = jnp.clip(gu[:, D_FF:], -SWIGLU_LIMIT, SWIGLU_LIMIT)
    hdn = (u + 1.0) * (g * (1.0 / (1.0 + jnp.exp(-SWIGLU_ALPHA * g))))
    return _dot(hdn.astype(BF16), wd) + bd


def _pack_rows(x):
    bits = lax.bitcast_convert_type(x, jnp.uint32)
    n = x.shape[1] // 2
    return bits[:, :n] | (bits[:, n:] >> 16)


def _unpack_rows(p):
    hi = lax.bitcast_convert_type(p & jnp.uint32(0xFFFF0000), F32)
    lo = lax.bitcast_convert_type(p << 16, F32)
    return jnp.concatenate([hi, lo], axis=1).astype(BF16)


def _experts_kernel(te_ref, tb_ref, first_ref, used_ref, xs_ref, wgu_ref, bgu_ref, wd_ref, bd_ref, ys_ref,
                    wgu_sc, wd_sc):
    del te_ref, tb_ref
    t = pl.program_id(0)

    @pl.when(t < used_ref[0])
    def _():
        @pl.when(first_ref[t] == 1)
        def _():
            wgu_sc[...] = wgu_ref[0].astype(BF16)
            wd_sc[...] = wd_ref[0].astype(BF16)

        ys_ref[...] = _swiglu_expert(_unpack_rows(xs_ref[...]), wgu_sc[...], bgu_ref[0], wd_sc[...], bd_ref[0])


def _experts(tile_expert, tile_block, tile_first, used, xs, w):
    n_slots = xs.shape[0]
    d = D_MODEL
    r = EXPERT_ROWS
    rows = lambda width: pl.BlockSpec((r, width), lambda t, te, tb, tf, u: (tb[t], 0))
    per_e = lambda *shape: pl.BlockSpec((1,) + shape, lambda t, te, tb, tf, u: (te[t],) + (0,) * len(shape))
    grid_spec = pltpu.PrefetchScalarGridSpec(
        num_scalar_prefetch=4,
        grid=(n_slots // r,),
        in_specs=[rows(d // 2), per_e(d, 2 * D_FF), per_e(1, 2 * D_FF), per_e(D_FF, d), per_e(1, d)],
        out_specs=rows(d),
        scratch_shapes=[pltpu.VMEM((d, 2 * D_FF), BF16), pltpu.VMEM((D_FF, d), BF16)],
    )
    return pl.pallas_call(
        _experts_kernel,
        grid_spec=grid_spec,
        out_shape=jax.ShapeDtypeStruct((n_slots, d), F32),
        compiler_params=pltpu.CompilerParams(dimension_semantics=("arbitrary",),
                                             vmem_limit_bytes=EXPERTS_VMEM_LIMIT),
        name="experts",
    )(tile_expert, tile_block, tile_first, used, xs, w["w_gate_up"], w["b_gate_up"], w["w_down"], w["b_down"])


def _combine_kernel(h2_ref, gw_ref, y0_ref, y1_ref, y2_ref, y3_ref, gfin_ref, y_ref):
    gw = gw_ref[...]
    moe = None
    for k, yk in enumerate((y0_ref, y1_ref, y2_ref, y3_ref)):
        part = gw[:, k:k + 1] * yk[0]
        moe = part if moe is None else moe + part
    y_ref[...] = _rms(h2_ref[...] + moe, gfin_ref[...])


def _combine(h2, gw, ys4, w, tm):
    n, d = h2.shape
    tok = lambda width: pl.BlockSpec((tm, width), lambda i: (i, 0))
    part = lambda k: pl.BlockSpec((1, tm, d), lambda i, k=k: (k, i, 0))
    return pl.pallas_call(
        _combine_kernel,
        grid=(n // tm,),
        in_specs=[tok(d), tok(LANES)] + [part(k) for k in range(TOP_K)] + [_full((1, d))],
        out_specs=tok(d),
        out_shape=jax.ShapeDtypeStruct((n, d), F32),
        compiler_params=_params("arbitrary"),
        name="moe_combine",
    )(h2, gw, *([ys4] * TOP_K), w["g_final"])


def _routed_moe(h2, xt, route, gw, counts, w):
    n, d = h2.shape
    r = EXPERT_ROWS
    n_tiles = (n * TOP_K) // r + N_EXPERTS
    n_slots = n_tiles * r
    cnt = counts[0, :N_EXPERTS].astype(jnp.int32)
    padded = ((cnt + r - 1) // r) * r
    ends = jnp.cumsum(padded)
    starts = ends - padded
    eid = route[:, 0:TOP_K]
    rank = route[:, TOP_K:2 * TOP_K]
    onehot = eid[:, :, None] == jnp.arange(N_EXPERTS, dtype=jnp.int32)[None, None, :]
    slot = jnp.sum(jnp.where(onehot, starts[None, None, :], 0), axis=-1) + rank
    used = (ends[-1] // r).astype(jnp.int32)
    tile_block = jnp.minimum(jnp.arange(n_tiles, dtype=jnp.int32), used - 1)
    tile_expert = jnp.minimum(
        jnp.sum((tile_block[:, None] >= (ends // r)[None, :]).astype(jnp.int32), axis=1), N_EXPERTS - 1)
    tile_first = jnp.concatenate([jnp.ones((1,), jnp.int32),
                                  (tile_expert[1:] != tile_expert[:-1]).astype(jnp.int32)])
    slot_rows = slot.reshape(n // SC_CHUNK, SC_CHUNK, TOP_K).transpose(0, 2, 1).reshape(-1, SC_CHUNK)
    xs = _sc_scatter_rows(xt, slot_rows, n_slots)
    ys = _experts(tile_expert, tile_block, tile_first, used.reshape(1), xs, w)
    ys4 = _sc_gather(ys, slot.T.reshape(-1)).reshape(TOP_K, n, d)
    return _combine(h2, gw, ys4, w, tm=512)


def _moe_kernel(xt_ref, gate_ref, h2_ref, wgu_ref, bgu_ref, wd_ref, bd_ref, gfin_ref, y_ref, acc_sc):
    e = pl.program_id(1)

    @pl.when(e == 0)
    def _():
        acc_sc[...] = jnp.zeros(acc_sc.shape, F32)

    down = _swiglu_expert(_unpack_rows(xt_ref[...]), wgu_ref[0].astype(BF16), bgu_ref[0],
                          wd_ref[0].astype(BF16), bd_ref[0])
    gate = gate_ref[...]
    lane = lax.broadcasted_iota(jnp.int32, gate.shape, 1)
    ge = jnp.sum(jnp.where(lane == e, gate, 0.0), axis=1, keepdims=True)
    acc_sc[...] += ge * down

    @pl.when(e == N_EXPERTS - 1)
    def _():
        y_ref[...] = _rms(h2_ref[...] + acc_sc[...], gfin_ref[...])


def _moe(xt, gate, h2, w, tm):
    n, d = h2.shape
    tok = lambda width: pl.BlockSpec((tm, width), lambda i, e: (i, 0))
    return pl.pallas_call(
        _moe_kernel,
        grid=(n // tm, N_EXPERTS),
        in_specs=[tok(d // 2), tok(LANES), tok(d),
                  pl.BlockSpec((1, d, 2 * D_FF), lambda i, e: (e, 0, 0)),
                  pl.BlockSpec((1, 1, 2 * D_FF), lambda i, e: (e, 0, 0)),
                  pl.BlockSpec((1, D_FF, d), lambda i, e: (e, 0, 0)),
                  pl.BlockSpec((1, 1, d), lambda i, e: (e, 0, 0)),
                  _full((1, d))],
        out_specs=tok(d),
        out_shape=jax.ShapeDtypeStruct((n, d), F32),
        scratch_shapes=[pltpu.VMEM((tm, d), F32)],
        compiler_params=_params("arbitrary", "arbitrary"),
        name="moe",
    )(xt, gate, h2, w["w_gate_up"], w["b_gate_up"], w["w_down"], w["b_down"], w["g_final"])


def _rope_tables(pos):
    inv = ROPE_THETA ** (-jnp.arange(HALF_ROPE, dtype=F32) / HALF_ROPE)
    ang = pos.astype(F32)[:, None] * inv[None, :]
    cos, sin = jnp.cos(ang), jnp.sin(ang)
    n = pos.shape[0]
    pad = LANES - QK_NOPE - QK_ROPE
    rc = jnp.concatenate([jnp.ones((n, QK_NOPE), F32), cos, cos, jnp.zeros((n, pad), F32)], axis=1)
    ra = jnp.concatenate([jnp.zeros((n, QK_NOPE), F32), -sin, jnp.zeros((n, HALF_ROPE + pad), F32)], axis=1)
    rb = jnp.concatenate([jnp.zeros((n, QK_NOPE + HALF_ROPE), F32), sin, jnp.zeros((n, pad), F32)], axis=1)
    return rc, ra, rb


def _prepare(l, s_len, n_sample, past, g_mix, w_in, w_conv, g_q, w_uq, g_kv, w_uk, w_uv, g_conv_out, g_attn_out,
             w_out, g_cross, g_mem, w_mq, w_mk, w_mv, w_mo, g_ffn, w_router, b_router, w_gate_up, b_gate_up,
             w_down, b_down, g_final):
    h = MLA_HEADS
    row = lambda g: g.reshape(1, -1).astype(F32)
    head_pad = LANES - QK_NOPE - QK_ROPE
    win = w_in[l]
    win = jnp.concatenate([win[:, :OFF_KR], jnp.zeros((D_MODEL, QK_NOPE), F32), win[:, OFF_KR:],
                           jnp.zeros((D_MODEL, head_pad), F32)], axis=1)
    wuq = jnp.pad(w_uq[l].reshape(Q_LORA, h, QK_NOPE + QK_ROPE), ((0, 0), (0, 0), (0, head_pad)))
    wuk = jnp.pad(w_uk[l], ((0, 0), (0, 0), (0, LANES - QK_NOPE)))
    wuv_pair = w_uv[l].reshape(KV_LORA, h // 2, 2, V_HEAD)
    zeros = jnp.zeros((KV_LORA, h // 2, V_HEAD), F32)
    wuv = jnp.stack([jnp.concatenate([wuv_pair[:, :, 0], zeros], axis=-1),
                     jnp.concatenate([zeros, wuv_pair[:, :, 1]], axis=-1)], axis=2)
    wabs = jnp.pad(jnp.transpose(w_uk[l], (1, 2, 0)), ((0, 0), (0, LANES - QK_NOPE), (0, 0)))
    eye = jnp.eye(h, dtype=F32)
    wuv_heads = jnp.einsum("chd,hg->hcgd", w_uv[l], eye).reshape(h, KV_LORA, h * V_HEAD)
    wr = jnp.pad(w_router[l], ((0, 0), (0, LANES - N_EXPERTS)))
    br = jnp.pad(b_router[l], (0, LANES - N_EXPERTS)).reshape(1, LANES)
    return {
        "g_mix": row(g_mix[l]), "w_in": win.astype(BF16), "w_conv": w_conv[l].astype(F32),
        "g_q": row(g_q[l]), "w_uq": wuq.reshape(Q_LORA, h * LANES).astype(BF16),
        "g_kv": row(g_kv[l]), "w_uk": wuk.reshape(KV_LORA, h * LANES).astype(BF16),
        "w_uv": wuv.reshape(KV_LORA, h * LANES).T.astype(BF16), "w_abs": wabs.astype(BF16),
        "w_uv_heads": wuv_heads.astype(BF16),
        "g_conv_out": row(g_conv_out[l]), "g_attn_out": row(g_attn_out[l]),
        "w_out": w_out[l].astype(BF16), "g_cross": row(g_cross[l]), "g_mem": row(g_mem[l]),
        "w_mq": w_mq[l].astype(BF16), "w_mk": w_mk[l].astype(BF16), "w_mv": w_mv[l].astype(BF16),
        "w_mo": w_mo[l].astype(BF16), "g_ffn": row(g_ffn[l]),
        "w_router": wr.astype(BF16), "b_router": br.astype(F32),
        "w_gate_up": w_gate_up[l], "b_gate_up": b_gate_up[l].reshape(N_EXPERTS, 1, 2 * D_FF),
        "w_down": w_down[l], "b_down": b_down[l].reshape(N_EXPERTS, 1, D_MODEL),
        "g_final": row(g_final),
        "rope_prompt": _rope_tables(jnp.arange(s_len)),
        "rope_sample": _rope_tables(jnp.full((n_sample,), past)),
    }


def kernel(x_prompt, x_sample, mem_prompt, cache_conv, cache_ckv, cache_krope, cache_mem_k, cache_mem_v, page_table, g_mix, w_in, w_conv, g_q, w_uq, g_kv, w_uk, w_uv, g_conv_out, g_attn_out, w_out, g_cross, g_mem, w_mq, w_mk, w_mv, w_mo, g_ffn, w_router, b_router, w_gate_up, b_gate_up, w_down, b_down, g_final):
    bp, s_len, d = x_prompt.shape
    bd, t_len, _ = x_sample.shape
    depth = g_mix.shape[0]
    assert depth == 1 and t_len == 1, "kernel is written for one layer and one decode token per sequence"
    n_pages = page_table.shape[1]
    past = n_pages * PAGE_SIZE
    n_p = bp * s_len
    l = 0
    w = _prepare(l, s_len, bd, past, g_mix, w_in, w_conv, g_q, w_uq, g_kv, w_uk, w_uv, g_conv_out, g_attn_out,
                 w_out, g_cross, g_mem, w_mq, w_mk, w_mv, w_mo, g_ffn, w_router, b_router, w_gate_up,
                 b_gate_up, w_down, b_down, g_final)

    xs = x_sample.reshape(bd, d)
    prev0 = cache_conv[l, :, 0, :]
    prev1 = cache_conv[l, :, 1, :]
    q_s, qlat_s, lat_s, kr_s, ycn_s, u_s = _inproj_sample(xs, prev0, prev1, w)
    olat = _decode(page_table, jnp.transpose(qlat_s, (1, 0, 2)), jnp.transpose(q_s, (1, 0, 2)),
                   lat_s.reshape(bd, 1, KV_LORA), kr_s.reshape(bd, 1, QK_ROPE), cache_ckv,
                   jnp.swapaxes(cache_krope, 2, 3))
    ya_s = _uv_project(jnp.transpose(olat, (1, 0, 2)), w["w_uv_heads"])
    h1_s, qm_s = _mixout(xs, ycn_s, ya_s, w, tm=bd)
    o_s = _cross_sample(qm_s.astype(F32).reshape(bd, MEM_HEADS, MEM_HEAD_DIM), cache_mem_k[l], cache_mem_v[l], tb=4)
    h2_s, xt_s, gate_s, _, _, _ = _router(h1_s, o_s.reshape(bd, d).astype(BF16), w, tm=bd)
    y_s = _moe(xt_s, gate_s, h2_s, w, tm=bd)

    q, k, v, lat_p, kr_p, ycn_p, conv_p = _inproj_prompt(x_prompt, w, tm=512)
    ya_p = _attention(q, k, v, tq=512)
    mk_p, mv_p = _memory_kv(mem_prompt, w["g_mem"], w["w_mk"], w["w_mv"])
    xp = x_prompt.reshape(n_p, d)
    h2_p, xt_p, route_p, gw_p, cnt_p = _post_prompt(xp, ycn_p.reshape(n_p, -1), ya_p, mk_p, mv_p, w, tm=512)
    y_p = _routed_moe(h2_p, xt_p, route_p, gw_p, cnt_p, w)

    mem_shape = (1, bp, MEM_TOKENS, MEM_HEADS, MEM_HEAD_DIM)
    return (y_p.reshape(bp, s_len, d), y_s.reshape(bd, 1, d),
            conv_p[None], lat_p[None], kr_p[None], mk_p.reshape(mem_shape), mv_p.reshape(mem_shape),
            jnp.stack([prev1, u_s], axis=1)[None], lat_s.reshape(1, bd, 1, KV_LORA),
            kr_s.reshape(1, bd, 1, QK_ROPE))
```

```python
import functools

import jax
import jax.numpy as jnp
from jax import lax
from jax.experimental import pallas as pl
from jax.experimental.pallas import tpu as pltpu
from jax.experimental.pallas import tpu_sc as plsc

D_MODEL = 1024
CONV_WIDTH = 512
CONV_K = 3
MLA_HEADS = 8
QK_NOPE = 64
QK_ROPE = 32
V_HEAD = 64
Q_LORA = 384
KV_LORA = 256
ROPE_THETA = 10000.0
PAGE_SIZE = 128
MEM_TOKENS = 256
MEM_HEADS = 4
MEM_HEAD_DIM = D_MODEL // MEM_HEADS
N_EXPERTS = 32
TOP_K = 4
D_FF = D_MODEL
SWIGLU_LIMIT = 7.0
SWIGLU_ALPHA = 1.702
NORM_EPS = 1e-6

LANES = 128
HALF_ROPE = QK_ROPE // 2
IN_WIDTH = 3 * CONV_WIDTH + Q_LORA + KV_LORA + QK_ROPE
IN_WIDTH_PAD = 3 * CONV_WIDTH + Q_LORA + KV_LORA + LANES
OFF_CQ = 3 * CONV_WIDTH
OFF_CKV = OFF_CQ + Q_LORA
OFF_KR = OFF_CKV + KV_LORA
MLA_SCALE = (QK_NOPE + QK_ROPE) ** -0.5
MEM_SCALE = MEM_HEAD_DIM ** -0.5
LOG2_E = 1.4426950408889634
VMEM_LIMIT = 48 * 1024 * 1024
PAGES_PER_STEP = 32

BF16 = jnp.bfloat16
F32 = jnp.float32
NT_DIMS = (((1,), (1,)), ((), ()))


def _params(*sem):
    return pltpu.CompilerParams(dimension_semantics=sem, vmem_limit_bytes=VMEM_LIMIT)


def _rms(x, g):
    return x * lax.rsqrt(jnp.mean(x * x, axis=-1, keepdims=True) + NORM_EPS) * g


def _dot(a, b):
    return jnp.dot(a, b, preferred_element_type=F32)


def _dot_nt(a, b):
    return lax.dot_general(a, b, NT_DIMS, preferred_element_type=F32)


def _rope_group(x, rc, ra, rb):
    return x * rc + pltpu.roll(x, LANES - HALF_ROPE, 1) * ra + pltpu.roll(x, HALF_ROPE, 1) * rb


def _full(shape):
    return pl.BlockSpec(shape, lambda *_: (0,) * len(shape))


def _memkv_kernel(mem_ref, g_ref, wk_ref, wv_ref, k_ref, v_ref):
    m = _rms(mem_ref[0], g_ref[...]).astype(BF16)
    k_ref[0] = _dot(m, wk_ref[...])
    v_ref[0] = _dot(m, wv_ref[...])


def _memory_kv(mem, g_mem, w_mk, w_mv):
    b, n, d = mem.shape
    blk = pl.BlockSpec((1, n, d), lambda i: (i, 0, 0))
    return pl.pallas_call(
        _memkv_kernel,
        grid=(b,),
        in_specs=[blk, _full((1, d)), _full((d, d)), _full((d, d))],
        out_specs=[blk, blk],
        out_shape=[jax.ShapeDtypeStruct((b, n, d), F32)] * 2,
        compiler_params=_params("arbitrary"),
        name="memory_kv",
    )(mem, g_mem, w_mk, w_mv)


def _inproj_common(x, gmix, win, gq, wuq, gkv, rc, ra, rb):
    a = _rms(x, gmix).astype(BF16)
    z = _dot(a, win)
    b_g = z[:, 0:CONV_WIDTH]
    u = z[:, CONV_WIDTH:2 * CONV_WIDTH] * z[:, 2 * CONV_WIDTH:3 * CONV_WIDTH]
    cq = _rms(z[:, OFF_CQ:OFF_CKV], gq).astype(BF16)
    q = _dot(cq, wuq)
    q_heads = [_rope_group(q[:, h * LANES:(h + 1) * LANES], rc, ra, rb) for h in range(MLA_HEADS)]
    lat = _rms(z[:, OFF_CKV:OFF_KR], gkv)
    kr = _rope_group(z[:, OFF_KR:OFF_KR + LANES], rc, ra, rb)
    return b_g, u, q_heads, lat, kr


def _inproj_prompt_kernel(x_ref, gmix_ref, win_ref, wconv_ref, gq_ref, wuq_ref, gkv_ref, wuk_ref, wuv_ref,
                          gco_ref, rc_ref, ra_ref, rb_ref,
                          q_ref, k_ref, v_ref, lat_ref, kr_ref, ycn_ref, conv_ref, ubuf, *, tm):
    j = pl.program_id(1)
    b_g, u, q_heads, lat, kr = _inproj_common(
        x_ref[0], gmix_ref[...], win_ref[...], gq_ref[...], wuq_ref[...], gkv_ref[...],
        rc_ref[...], ra_ref[...], rb_ref[...])
    for h in range(MLA_HEADS):
        q_ref[0, h] = (q_heads[h] * (MLA_SCALE * LOG2_E)).astype(BF16)
    lat_ref[0] = lat
    kr_ref[0] = kr[:, QK_NOPE:QK_NOPE + QK_ROPE]
    lat_b = lat.astype(BF16)
    kn = _dot(lat_b, wuk_ref[...])
    v_t = _dot_nt(wuv_ref[...], lat_b)
    for h in range(MLA_HEADS):
        k_ref[0, h] = (kn[:, h * LANES:(h + 1) * LANES] + kr).astype(BF16)
        v_ref[0, h, 0] = v_t[h * LANES:(h + 1) * LANES, :].astype(BF16)

    @pl.when(j == 0)
    def _():
        ubuf[0:8, :] = jnp.zeros((8, CONV_WIDTH), F32)

    ubuf[8:8 + tm, :] = u
    u1 = ubuf[7:7 + tm, :]
    u2 = ubuf[6:6 + tm, :]
    wc = wconv_ref[...]
    yc = b_g * (wc[0:1, :] * u2 + wc[1:2, :] * u1 + wc[2:3, :] * u)
    ycn_ref[0] = _rms(yc, gco_ref[...]).astype(BF16)
    ubuf[0:8, :] = ubuf[tm:tm + 8, :]
    conv_ref[0] = u[tm - (CONV_K - 1):tm, :]


def _inproj_prompt(x, w, tm):
    b, s, d = x.shape
    h = MLA_HEADS
    tok = lambda n: pl.BlockSpec((1, tm, n), lambda i, j: (i, j, 0))
    head = pl.BlockSpec((1, h, tm, LANES), lambda i, j: (i, 0, j, 0))
    rope = pl.BlockSpec((tm, LANES), lambda i, j: (j, 0))
    qkv_shape = jax.ShapeDtypeStruct((b, h, s, LANES), BF16)
    return pl.pallas_call(
        functools.partial(_inproj_prompt_kernel, tm=tm),
        grid=(b, s // tm),
        in_specs=[tok(d), _full((1, d)), _full((d, IN_WIDTH_PAD)), _full((CONV_K, CONV_WIDTH)),
                  _full((1, Q_LORA)), _full((Q_LORA, h * LANES)), _full((1, KV_LORA)),
                  _full((KV_LORA, h * LANES)), _full((h * LANES, KV_LORA)), _full((1, CONV_WIDTH)),
                  rope, rope, rope],
        out_specs=[head, head, pl.BlockSpec((1, h, 1, LANES, tm), lambda i, j: (i, 0, j, 0, 0)),
                   tok(KV_LORA), tok(QK_ROPE), tok(CONV_WIDTH),
                   pl.BlockSpec((1, CONV_K - 1, CONV_WIDTH), lambda i, j: (i, 0, 0))],
        out_shape=[qkv_shape, qkv_shape, jax.ShapeDtypeStruct((b, h, s // tm, LANES, tm), BF16),
                   jax.ShapeDtypeStruct((b, s, KV_LORA), F32),
                   jax.ShapeDtypeStruct((b, s, QK_ROPE), F32),
                   jax.ShapeDtypeStruct((b, s, CONV_WIDTH), BF16),
                   jax.ShapeDtypeStruct((b, CONV_K - 1, CONV_WIDTH), F32)],
        scratch_shapes=[pltpu.VMEM((tm + 8, CONV_WIDTH), F32)],
        compiler_params=_params("arbitrary", "arbitrary"),
        name="inproj_prompt",
    )(x, w["g_mix"], w["w_in"], w["w_conv"], w["g_q"], w["w_uq"], w["g_kv"], w["w_uk"], w["w_uv"],
      w["g_conv_out"], *w["rope_prompt"])


def _inproj_sample_kernel(x_ref, gmix_ref, win_ref, wconv_ref, gq_ref, wuq_ref, gkv_ref, wabs_ref,
                          gco_ref, rc_ref, ra_ref, rb_ref, p0_ref, p1_ref,
                          q_ref, qlat_ref, lat_ref, kr_ref, ycn_ref, u_ref):
    b_g, u, q_heads, lat, kr = _inproj_common(
        x_ref[...], gmix_ref[...], win_ref[...], gq_ref[...], wuq_ref[...], gkv_ref[...],
        rc_ref[...], ra_ref[...], rb_ref[...])
    for h in range(MLA_HEADS):
        qh = q_heads[h].astype(BF16)
        q_ref[h] = qh
        qlat_ref[h] = _dot(qh, wabs_ref[h]).astype(BF16)
    lat_ref[...] = lat
    kr_ref[...] = kr[:, QK_NOPE:QK_NOPE + QK_ROPE]
    wc = wconv_ref[...]
    yc = b_g * (wc[0:1, :] * p0_ref[...] + wc[1:2, :] * p1_ref[...] + wc[2:3, :] * u)
    ycn_ref[...] = _rms(yc, gco_ref[...]).astype(BF16)
    u_ref[...] = u


def _inproj_sample(x, prev0, prev1, w):
    n, d = x.shape
    h = MLA_HEADS
    return pl.pallas_call(
        _inproj_sample_kernel,
        grid=(1,),
        in_specs=[_full((n, d)), _full((1, d)), _full((d, IN_WIDTH_PAD)), _full((CONV_K, CONV_WIDTH)),
                  _full((1, Q_LORA)), _full((Q_LORA, h * LANES)), _full((1, KV_LORA)),
                  _full((h, LANES, KV_LORA)), _full((1, CONV_WIDTH)),
                  _full((n, LANES)), _full((n, LANES)), _full((n, LANES)),
                  _full((n, CONV_WIDTH)), _full((n, CONV_WIDTH))],
        out_specs=[_full((h, n, LANES)), _full((h, n, KV_LORA)), _full((n, KV_LORA)), _full((n, QK_ROPE)),
                   _full((n, CONV_WIDTH)), _full((n, CONV_WIDTH))],
        out_shape=[jax.ShapeDtypeStruct((h, n, LANES), BF16),
                   jax.ShapeDtypeStruct((h, n, KV_LORA), BF16),
                   jax.ShapeDtypeStruct((n, KV_LORA), F32),
                   jax.ShapeDtypeStruct((n, QK_ROPE), F32),
                   jax.ShapeDtypeStruct((n, CONV_WIDTH), BF16),
                   jax.ShapeDtypeStruct((n, CONV_WIDTH), F32)],
        compiler_params=_params("arbitrary"),
        name="inproj_sample",
    )(x, w["g_mix"], w["w_in"], w["w_conv"], w["g_q"], w["w_uq"], w["g_kv"], w["w_abs"],
      w["g_conv_out"], *w["rope_sample"], prev0, prev1)


def _softmax_step(s, v, m_sc, l_sc, acc_sc):
    m_prev = m_sc[...]
    m_next = jnp.maximum(m_prev, jnp.max(s, axis=1, keepdims=True))
    p = jnp.exp(s - m_next[:, 0:1])
    alpha = jnp.exp(m_prev - m_next)
    l_sc[...] = alpha * l_sc[...] + jnp.sum(p, axis=1, keepdims=True)
    pv = _dot(p.astype(BF16), v)
    acc_sc[...] = acc_sc[...] * alpha[:, 0:1] + pv
    m_sc[...] = m_next


def _attn_kernel(q_ref, k_ref, v_ref, o_ref, m_sc, l_sc, acc_sc, s0_sc, s1_sc, *, tq):
    qi = pl.program_id(2)
    key = lax.broadcasted_iota(jnp.int32, (tq, tq), 0)
    qry = lax.broadcasted_iota(jnp.int32, (tq, tq), 1)
    m_sc[...] = jnp.full(m_sc.shape, -jnp.inf, F32)
    l_sc[...] = jnp.zeros(l_sc.shape, F32)
    acc_sc[...] = jnp.zeros(acc_sc.shape, F32)

    def scores(j, buf):
        for hh in range(2):
            k = k_ref[0, hh, pl.ds(pl.multiple_of(j * tq, tq), tq), :]
            buf[hh] = _dot_nt(k, q_ref[0, hh])

    def consume(j, buf, masked):
        for hh in range(2):
            s = buf[hh]
            if masked:
                s = jnp.where(key <= qry, s, -jnp.inf)
            m_prev = m_sc[hh]
            m_next = jnp.maximum(m_prev, jnp.max(s, axis=0, keepdims=True))
            p = jnp.exp2(s - m_next)
            alpha = jnp.exp2(m_prev - m_next)
            l_sc[hh] = alpha * l_sc[hh] + jnp.sum(p, axis=0, keepdims=True)
            acc_sc[hh] = acc_sc[hh] * alpha + _dot(v_ref[0, hh, j], p.astype(BF16))
            m_sc[hh] = m_next

    def pair(jj, carry):
        j = 2 * jj
        scores(j + 1, s1_sc)
        consume(j, s0_sc, False)
        scores(j + 2, s0_sc)
        consume(j + 1, s1_sc, False)
        return carry

    scores(0, s0_sc)
    lax.fori_loop(0, qi // 2, pair, 0)

    @pl.when(qi % 2 == 0)
    def _():
        consume(qi, s0_sc, True)

    @pl.when(qi % 2 == 1)
    def _():
        scores(qi, s1_sc)
        consume(qi - 1, s0_sc, False)
        consume(qi, s1_sc, True)

    o_ref[0] = acc_sc[0] / l_sc[0] + acc_sc[1] / l_sc[1]


def _attention(q, k, v_t, tq):
    b, h, s, _ = q.shape
    qspec = pl.BlockSpec((1, 2, tq, LANES), lambda i, p, j: (i, p, j, 0))
    kspec = pl.BlockSpec((1, 2, s, LANES), lambda i, p, j: (i, p, 0, 0))
    vspec = pl.BlockSpec((1, 2, s // tq, LANES, tq), lambda i, p, j: (i, p, 0, 0, 0))
    return pl.pallas_call(
        functools.partial(_attn_kernel, tq=tq),
        grid=(b, h // 2, s // tq),
        in_specs=[qspec, kspec, vspec],
        out_specs=pl.BlockSpec((1, LANES, tq), lambda i, p, j: (i, p, j)),
        out_shape=jax.ShapeDtypeStruct((b, h * V_HEAD, s), F32),
        scratch_shapes=[pltpu.VMEM((2, 1, tq), F32), pltpu.VMEM((2, 1, tq), F32), pltpu.VMEM((2, LANES, tq), F32),
                        pltpu.VMEM((2, tq, tq), F32), pltpu.VMEM((2, tq, tq), F32)],
        compiler_params=_params("arbitrary", "arbitrary", "arbitrary"),
        name="mla_prompt_attention",
    )(q, k, v_t)


def _decode_kernel(pt_ref, qlat_ref, q_ref, lat_ref, krn_ref, ckv_hbm, kr_hbm, o_ref,
                   ckv_buf, kr_buf, sems, m_sc, l_sc, acc_sc, *, n_steps, n_seqs):
    npg = PAGES_PER_STEP
    b = pl.program_id(0)
    step = pl.program_id(1)
    t = b * n_steps + step
    slot = t % 2

    def start_pages(seq, st, sl):
        for i in range(npg):
            page = pt_ref[seq, st * npg + i]
            pltpu.make_async_copy(ckv_hbm.at[0, page], ckv_buf.at[sl, i], sems.at[0, sl]).start()
            pltpu.make_async_copy(kr_hbm.at[0, page], kr_buf.at[sl, i], sems.at[1, sl]).start()

    def wait_pages(sl):
        pltpu.make_async_copy(ckv_hbm.at[0, pl.ds(0, npg)], ckv_buf.at[sl], sems.at[0, sl]).wait()
        pltpu.make_async_copy(kr_hbm.at[0, pl.ds(0, npg)], kr_buf.at[sl], sems.at[1, sl]).wait()

    last = n_seqs * n_steps - 1

    @pl.when(t == 0)
    def _():
        start_pages(0, 0, 0)

    @pl.when(step == 0)
    def _():
        m_sc[...] = jnp.full(m_sc.shape, -jnp.inf, F32)
        l_sc[...] = jnp.zeros(l_sc.shape, F32)
        acc_sc[...] = jnp.zeros(acc_sc.shape, F32)

    ql = qlat_ref[0]
    qr = q_ref[0][:, QK_NOPE:QK_NOPE + QK_ROPE]
    wait_pages(slot)
    nxt = jnp.minimum(t + 1, last)
    start_pages(nxt // n_steps, nxt % n_steps, 1 - slot)
    ckv = ckv_buf[slot].reshape(npg * PAGE_SIZE, KV_LORA).astype(BF16)
    kr_t = jnp.concatenate([kr_buf[slot, i] for i in range(npg)], axis=1).astype(BF16)
    s = (_dot_nt(ql, ckv) + _dot(qr, kr_t)) * MLA_SCALE
    _softmax_step(s, ckv, m_sc, l_sc, acc_sc)

    @pl.when(step == n_steps - 1)
    def _():
        lat = lat_ref[0]
        s_new = (jnp.sum(ql.astype(F32) * lat, axis=1, keepdims=True)
                 + jnp.sum(qr.astype(F32) * krn_ref[0], axis=1, keepdims=True)) * MLA_SCALE
        m_prev = m_sc[...]
        m_next = jnp.maximum(m_prev, s_new)
        p_new = jnp.exp(s_new - m_next[:, 0:1])
        alpha = jnp.exp(m_prev - m_next)
        l_fin = alpha * l_sc[...] + p_new
        acc = acc_sc[...] * alpha[:, 0:1] + p_new * lat
        o_ref[0] = acc / l_fin[:, 0:1]

    @pl.when(t == last)
    def _():
        wait_pages(1 - slot)


def _decode(page_table, qlat, q, lat, krn, cache_ckv, cache_krope):
    bd, n_pages = page_table.shape
    npg = PAGES_PER_STEP
    n_steps = n_pages // npg
    h = MLA_HEADS
    per_b = lambda *shape: pl.BlockSpec((1,) + shape, lambda b, s, pt: (b,) + (0,) * len(shape))
    hbm = pl.BlockSpec(memory_space=pl.ANY)
    grid_spec = pltpu.PrefetchScalarGridSpec(
        num_scalar_prefetch=1,
        grid=(bd, n_steps),
        in_specs=[per_b(h, KV_LORA), per_b(h, LANES), per_b(1, KV_LORA), per_b(1, QK_ROPE), hbm, hbm],
        out_specs=per_b(h, KV_LORA),
        scratch_shapes=[pltpu.VMEM((2, npg, PAGE_SIZE, KV_LORA), F32), pltpu.VMEM((2, npg, QK_ROPE, PAGE_SIZE), F32),
                        pltpu.SemaphoreType.DMA((2, 2)),
                        pltpu.VMEM((h, LANES), F32), pltpu.VMEM((h, LANES), F32), pltpu.VMEM((h, KV_LORA), F32)],
    )
    return pl.pallas_call(
        functools.partial(_decode_kernel, n_steps=n_steps, n_seqs=bd),
        grid_spec=grid_spec,
        out_shape=jax.ShapeDtypeStruct((bd, h, KV_LORA), F32),
        compiler_params=_params("arbitrary", "arbitrary"),
        name="mla_decode",
    )(page_table, qlat, q, lat, krn, cache_ckv, cache_krope)


def _uv_kernel(olat_ref, wuv_ref, ya_ref):
    acc = None
    for h in range(MLA_HEADS):
        part = _dot(olat_ref[h].astype(BF16), wuv_ref[h])
        acc = part if acc is None else acc + part
    ya_ref[...] = acc


def _uv_project(olat, wuv_heads):
    h, n, c = olat.shape
    width = MLA_HEADS * V_HEAD
    return pl.pallas_call(
        _uv_kernel,
        grid=(1,),
        in_specs=[_full((h, n, c)), _full((h, c, width))],
        out_specs=_full((n, width)),
        out_shape=jax.ShapeDtypeStruct((n, width), F32),
        compiler_params=_params("arbitrary"),
        name="decode_uv",
    )(olat, wuv_heads)


def _mixout_math(x, ycn, ya, ga, wout, gcross, wmq):
    yan = _rms(ya, ga).astype(BF16)
    mixed = jnp.concatenate([ycn, yan], axis=1)
    h1 = x + _dot(mixed, wout)
    return h1, _dot(_rms(h1, gcross).astype(BF16), wmq).astype(BF16)


def _mixout_kernel(x_ref, ycn_ref, ya_ref, ga_ref, wout_ref, gcross_ref, wmq_ref, h1_ref, qm_ref):
    h1_ref[...], qm_ref[...] = _mixout_math(x_ref[...], ycn_ref[...], ya_ref[...], ga_ref[...], wout_ref[...],
                                            gcross_ref[...], wmq_ref[...])


def _mixout(x, ycn, ya, w, tm):
    n, d = x.shape
    width = MLA_HEADS * V_HEAD
    mix = CONV_WIDTH + width
    tok = lambda width: pl.BlockSpec((tm, width), lambda i: (i, 0))
    return pl.pallas_call(
        _mixout_kernel,
        grid=(n // tm,),
        in_specs=[tok(d), tok(CONV_WIDTH), tok(width), _full((1, width)),
                  _full((mix, d)), _full((1, d)), _full((d, d))],
        out_specs=[tok(d), tok(d)],
        out_shape=[jax.ShapeDtypeStruct((n, d), F32), jax.ShapeDtypeStruct((n, d), BF16)],
        compiler_params=_params("arbitrary"),
        name="mix_out",
    )(x, ycn, ya, w["g_attn_out"], w["w_out"], w["g_cross"], w["w_mq"])


def _cross_math(q, k_ref, v_ref):
    outs = []
    for h in range(MEM_HEADS):
        sl = slice(h * MEM_HEAD_DIM, (h + 1) * MEM_HEAD_DIM)
        s = _dot_nt(q[:, sl], k_ref[0, :, sl].astype(BF16)) * MEM_SCALE
        e = jnp.exp(s - jnp.max(s, axis=1, keepdims=True))
        p = e / jnp.sum(e, axis=1, keepdims=True)
        outs.append(_dot(p.astype(BF16), v_ref[0, :, sl].astype(BF16)))
    return jnp.concatenate(outs, axis=1).astype(BF16)


def _cross_sample_kernel(q_ref, k_ref, v_ref, o_ref, *, tb):
    for t in range(tb):
        s = jnp.sum(k_ref[t] * q_ref[t][None], axis=2, keepdims=True) * MEM_SCALE
        e = jnp.exp(s - jnp.max(s, axis=0, keepdims=True))
        p = e / jnp.sum(e, axis=0, keepdims=True)
        o_ref[t] = jnp.sum(p * v_ref[t], axis=0)


def _cross_sample(qm, mk, mv, tb):
    n = qm.shape[0]
    tok = pl.BlockSpec((tb, MEM_HEADS, MEM_HEAD_DIM), lambda i: (i, 0, 0))
    mem = pl.BlockSpec((tb, MEM_TOKENS, MEM_HEADS, MEM_HEAD_DIM), lambda i: (i, 0, 0, 0))
    return pl.pallas_call(
        functools.partial(_cross_sample_kernel, tb=tb),
        grid=(n // tb,),
        in_specs=[tok, mem, mem],
        out_specs=tok,
        out_shape=jax.ShapeDtypeStruct((n, MEM_HEADS, MEM_HEAD_DIM), F32),
        compiler_params=_params("arbitrary"),
        name="cross_sample",
    )(qm, mk, mv)


def _router_math(h1, o, wmo, gffn, wr, br, cnt_prev):
    h2 = h1 + _dot(o, wmo)
    xt = _rms(h2, gffn).astype(BF16)
    logits = _dot(xt, wr) + br
    tm = logits.shape[0]
    lane = lax.broadcasted_iota(jnp.int32, logits.shape, 1)
    logits = jnp.where(lane < N_EXPERTS, logits, -jnp.inf)
    work = logits
    sel = lane < 0
    picks = []
    for k in range(TOP_K):
        m = jnp.max(work, axis=1, keepdims=True)
        idx = jnp.min(jnp.where(work == m, lane, LANES), axis=1, keepdims=True)
        hit = lane == idx
        sel = jnp.logical_or(sel, hit)
        work = jnp.where(hit, -jnp.inf, work)
        picks.append((m, idx, hit))
    top = picks[0][0]
    e = jnp.where(sel, jnp.exp(logits - top), 0.0)
    denom = jnp.sum(e, axis=1, keepdims=True)

    sel_f = sel.astype(F32)
    earlier = (lax.broadcasted_iota(jnp.int32, (tm, tm), 0) > lax.broadcasted_iota(jnp.int32, (tm, tm), 1))
    rank_all = _dot(earlier.astype(BF16), sel_f.astype(BF16)) + cnt_prev
    route = jnp.zeros(logits.shape, jnp.int32)
    gw = jnp.zeros(logits.shape, F32)
    for k, (m, idx, hit) in enumerate(picks):
        rank = jnp.sum(jnp.where(hit, rank_all, 0.0), axis=1, keepdims=True).astype(jnp.int32)
        route = jnp.where(lane == k, idx, route)
        route = jnp.where(lane == TOP_K + k, rank, route)
        gw = jnp.where(lane == k, jnp.exp(m - top) / denom, gw)
    cnt = cnt_prev + jnp.sum(sel_f, axis=0, keepdims=True)
    return h2, _pack_rows(xt.astype(F32)), e / denom, route, gw, cnt


def _router_kernel(h1_ref, o_ref, wmo_ref, gffn_ref, wr_ref, br_ref,
                   h2_ref, xt_ref, gate_ref, route_ref, gw_ref, cnt_ref, cnt_sc):
    @pl.when(pl.program_id(0) == 0)
    def _():
        cnt_sc[...] = jnp.zeros(cnt_sc.shape, F32)

    h2_ref[...], xt_ref[...], gate_ref[...], route_ref[...], gw_ref[...], cnt = _router_math(
        h1_ref[...], o_ref[...], wmo_ref[...], gffn_ref[...], wr_ref[...], br_ref[...], cnt_sc[...])
    cnt_sc[...] = cnt
    cnt_ref[...] = cnt


def _post_prompt_kernel(x_ref, ycn_ref, yat_ref, k_ref, v_ref, ga_ref, wout_ref, gcross_ref, wmq_ref,
                        wmo_ref, gffn_ref, wr_ref, br_ref, h2_ref, xt_ref, route_ref, gw_ref, cnt_ref, cnt_sc):
    @pl.when(pl.program_id(0) == 0)
    def _():
        cnt_sc[...] = jnp.zeros(cnt_sc.shape, F32)

    h1, qm = _mixout_math(x_ref[...], ycn_ref[...], yat_ref[0].T, ga_ref[...], wout_ref[...], gcross_ref[...],
                          wmq_ref[...])
    o = _cross_math(qm, k_ref, v_ref)
    h2_ref[...], xt_ref[...], _, route_ref[...], gw_ref[...], cnt = _router_math(
        h1, o, wmo_ref[...], gffn_ref[...], wr_ref[...], br_ref[...], cnt_sc[...])
    cnt_sc[...] = cnt
    cnt_ref[...] = cnt


def _post_prompt(x, ycn, ya_t, mk, mv, w, tm):
    n, d = x.shape
    width = MLA_HEADS * V_HEAD
    tiles = ya_t.shape[2] // tm
    tok = lambda width: pl.BlockSpec((tm, width), lambda i: (i, 0))
    mem = pl.BlockSpec((1, MEM_TOKENS, d), lambda i: (i // tiles, 0, 0))
    return pl.pallas_call(
        _post_prompt_kernel,
        grid=(n // tm,),
        in_specs=[tok(d), tok(CONV_WIDTH), pl.BlockSpec((1, width, tm), lambda i: (i // tiles, 0, i % tiles)),
                  mem, mem, _full((1, width)), _full((CONV_WIDTH + width, d)), _full((1, d)), _full((d, d)),
                  _full((d, d)), _full((1, d)), _full((d, LANES)), _full((1, LANES))],
        out_specs=[tok(d), tok(d // 2), tok(LANES), tok(LANES), _full((1, LANES))],
        out_shape=[jax.ShapeDtypeStruct((n, d), F32), jax.ShapeDtypeStruct((n, d // 2), jnp.uint32),
                   jax.ShapeDtypeStruct((n, LANES), jnp.int32), jax.ShapeDtypeStruct((n, LANES), F32),
                   jax.ShapeDtypeStruct((1, LANES), F32)],
        scratch_shapes=[pltpu.VMEM((1, LANES), F32)],
        compiler_params=_params("arbitrary"),
        name="post_prompt",
    )(x, ycn, ya_t, mk, mv, w["g_attn_out"], w["w_out"], w["g_cross"], w["w_mq"], w["w_mo"], w["g_ffn"],
      w["w_router"], w["b_router"])


def _router(h1, o, w, tm):
    n, d = h1.shape
    tok = lambda width: pl.BlockSpec((tm, width), lambda i: (i, 0))
    return pl.pallas_call(
        _router_kernel,
        grid=(n // tm,),
        in_specs=[tok(d), tok(d), _full((d, d)), _full((1, d)), _full((d, LANES)), _full((1, LANES))],
        out_specs=[tok(d), tok(d // 2), tok(LANES), tok(LANES), tok(LANES), _full((1, LANES))],
        out_shape=[jax.ShapeDtypeStruct((n, d), F32), jax.ShapeDtypeStruct((n, d // 2), jnp.uint32),
                   jax.ShapeDtypeStruct((n, LANES), F32), jax.ShapeDtypeStruct((n, LANES), jnp.int32),
                   jax.ShapeDtypeStruct((n, LANES), F32), jax.ShapeDtypeStruct((1, LANES), F32)],
        scratch_shapes=[pltpu.VMEM((1, LANES), F32)],
        compiler_params=_params("arbitrary"),
        name="router",
    )(h1, o, w["w_mo"], w["g_ffn"], w["w_router"], w["b_router"])


SC_CORES = 2
SC_SUBCORES = 16
SC_WORKERS = SC_CORES * SC_SUBCORES
SC_CHUNK = 32
EXPERT_ROWS = 512
EXPERTS_VMEM_LIMIT = 56 * 1024 * 1024


def _sc_gather(table, idx):
    b = idx.shape[0]
    d = table.shape[1]
    per_worker = b // SC_WORKERS
    n_chunks = per_worker // SC_CHUNK
    assert per_worker * SC_WORKERS == b and n_chunks * SC_CHUNK == per_worker
    mesh = plsc.VectorSubcoreMesh(core_axis_name="c", subcore_axis_name="s")

    assert n_chunks % 2 == 0
    row_buf = pltpu.VMEM((SC_CHUNK, d), table.dtype)

    @functools.partial(
        pl.kernel, mesh=mesh,
        out_type=jax.ShapeDtypeStruct((b, d), table.dtype),
        scratch_types=[pltpu.VMEM((per_worker,), jnp.int32), row_buf, row_buf] + [pltpu.SemaphoreType.DMA] * 4,
    )
    def gather(table_hbm, idx_hbm, out_hbm, idx_v, rows0, rows1, g0, g1, w0, w1):
        wid = lax.axis_index("s") * SC_CORES + lax.axis_index("c")
        base = pl.multiple_of(wid * per_worker, 8)
        rows, gsem, wsem = (rows0, rows1), (g0, g1), (w0, w1)
        pltpu.sync_copy(idx_hbm.at[pl.ds(base, per_worker)], idx_v)

        def fetch(c, buf):
            ids = idx_v.at[pl.ds(pl.multiple_of(c * SC_CHUNK, 8), SC_CHUNK)]
            return pltpu.make_async_copy(table_hbm.at[ids], rows[buf], gsem[buf])

        def flush(c, buf):
            dst = out_hbm.at[pl.ds(pl.multiple_of(base + c * SC_CHUNK, 8), SC_CHUNK)]
            return pltpu.make_async_copy(rows[buf], dst, wsem[buf])

        fetch(0, 0).start()

        @pl.loop(0, n_chunks, step=2)
        def _(c0):
            for buf in (0, 1):
                c = c0 + buf
                fetch(c, buf).wait()

                @pl.when(c + 1 < n_chunks)
                def _():
                    @pl.when(c >= 1)
                    def _():
                        flush(c - 1, 1 - buf).wait()

                    fetch(c + 1, 1 - buf).start()

                flush(c, buf).start()

        flush(n_chunks - 2, 0).wait()
        flush(n_chunks - 1, 1).wait()

    return gather(table, idx)


def _sc_scatter_rows(x, slot_rows, n_out):
    n, d = x.shape
    per_worker = n // SC_WORKERS
    n_chunks = per_worker // SC_CHUNK
    assert per_worker * SC_WORKERS == n and n_chunks * SC_CHUNK == per_worker and n_chunks % 2 == 0
    assert slot_rows.shape == (n // SC_CHUNK * TOP_K, SC_CHUNK)
    idx_rows = n_chunks * TOP_K
    mesh = plsc.VectorSubcoreMesh(core_axis_name="c", subcore_axis_name="s")
    row_buf = pltpu.VMEM((SC_CHUNK, d), x.dtype)

    @functools.partial(
        pl.kernel, mesh=mesh,
        out_type=jax.ShapeDtypeStruct((n_out, d), x.dtype),
        scratch_types=[pltpu.VMEM((idx_rows, SC_CHUNK), jnp.int32), row_buf, row_buf]
        + [pltpu.SemaphoreType.DMA] * 4,
    )
    def scatter(x_hbm, idx_hbm, out_hbm, idx_v, rows0, rows1, r0, r1, s0, s1):
        wid = lax.axis_index("s") * SC_CORES + lax.axis_index("c")
        base = pl.multiple_of(wid * per_worker, 8)
        rows, rsem, ssem = (rows0, rows1), (r0, r1), (s0, s1)
        pltpu.sync_copy(idx_hbm.at[pl.ds(pl.multiple_of(wid * idx_rows, 8), idx_rows)], idx_v)

        def fetch(c, buf):
            src = x_hbm.at[pl.ds(pl.multiple_of(base + c * SC_CHUNK, 8), SC_CHUNK)]
            return pltpu.make_async_copy(src, rows[buf], rsem[buf])

        def spread(c, k, buf):
            return pltpu.make_async_copy(rows[buf], out_hbm.at[idx_v.at[c * TOP_K + k]], ssem[buf])

        fetch(0, 0).start()

        @pl.loop(0, n_chunks, step=2)
        def _(c0):
            for buf in (0, 1):
                c = c0 + buf
                fetch(c, buf).wait()

                @pl.when(c + 1 < n_chunks)
                def _():
                    @pl.when(c >= 1)
                    def _():
                        for k in range(TOP_K):
                            spread(c - 1, k, 1 - buf).wait()

                    fetch(c + 1, 1 - buf).start()

                for k in range(TOP_K):
                    spread(c, k, buf).start()

        for k in range(TOP_K):
            spread(n_chunks - 2, k, 0).wait()
        for k in range(TOP_K):
            spread(n_chunks - 1, k, 1).wait()

    return scatter(x, slot_rows)


def _swiglu_expert(x, wgu, bgu, wd, bd):
    gu = _dot(x, wgu) + bgu
    g = jnp.minimum(gu[:, :D_FF], SWIGLU_LIMIT)
    u = jnp.clip(gu[:, D_FF:], -SWIGLU_LIMIT, SWIGLU_LIMIT)
    hdn = (u + 1.0) * (g * (1.0 / (1.0 + jnp.exp(-SWIGLU_ALPHA * g))))
    return _dot(hdn.astype(BF16), wd) + bd


def _pack_rows(x):
    bits = lax.bitcast_convert_type(x, jnp.uint32)
    n = x.shape[1] // 2
    return bits[:, :n] | (bits[:, n:] >> 16)


def _unpack_rows(p):
    hi = lax.bitcast_convert_type(p & jnp.uint32(0xFFFF0000), F32)
    lo = lax.bitcast_convert_type(p << 16, F32)
    return jnp.concatenate([hi, lo], axis=1).astype(BF16)


def _experts_kernel(te_ref, tb_ref, first_ref, used_ref, xs_ref, wgu_ref, bgu_ref, wd_ref, bd_ref, ys_ref,
                    wgu_sc, wd_sc):
    del te_ref, tb_ref
    t = pl.program_id(0)

    @pl.when(t < used_ref[0])
    def _():
        @pl.when(first_ref[t] == 1)
        def _():
            wgu_sc[...] = wgu_ref[0].astype(BF16)
            wd_sc[...] = wd_ref[0].astype(BF16)

        ys_ref[...] = _swiglu_expert(_unpack_rows(xs_ref[...]), wgu_sc[...], bgu_ref[0], wd_sc[...], bd_ref[0])


def _experts(tile_expert, tile_block, tile_first, used, xs, w):
    n_slots = xs.shape[0]
    d = D_MODEL
    r = EXPERT_ROWS
    rows = lambda width: pl.BlockSpec((r, width), lambda t, te, tb, tf, u: (tb[t], 0))
    per_e = lambda *shape: pl.BlockSpec((1,) + shape, lambda t, te, tb, tf, u: (te[t],) + (0,) * len(shape))
    grid_spec = pltpu.PrefetchScalarGridSpec(
        num_scalar_prefetch=4,
        grid=(n_slots // r,),
        in_specs=[rows(d // 2), per_e(d, 2 * D_FF), per_e(1, 2 * D_FF), per_e(D_FF, d), per_e(1, d)],
        out_specs=rows(d),
        scratch_shapes=[pltpu.VMEM((d, 2 * D_FF), BF16), pltpu.VMEM((D_FF, d), BF16)],
    )
    return pl.pallas_call(
        _experts_kernel,
        grid_spec=grid_spec,
        out_shape=jax.ShapeDtypeStruct((n_slots, d), F32),
        compiler_params=pltpu.CompilerParams(dimension_semantics=("arbitrary",),
                                             vmem_limit_bytes=EXPERTS_VMEM_LIMIT),
        name="experts",
    )(tile_expert, tile_block, tile_first, used, xs, w["w_gate_up"], w["b_gate_up"], w["w_down"], w["b_down"])


def _combine_kernel(h2_ref, gw_ref, y0_ref, y1_ref, y2_ref, y3_ref, gfin_ref, y_ref):
    gw = gw_ref[...]
    moe = None
    for k, yk in enumerate((y0_ref, y1_ref, y2_ref, y3_ref)):
        part = gw[:, k:k + 1] * yk[0]
        moe = part if moe is None else moe + part
    y_ref[...] = _rms(h2_ref[...] + moe, gfin_ref[...])


def _combine(h2, gw, ys4, w, tm):
    n, d = h2.shape
    tok = lambda width: pl.BlockSpec((tm, width), lambda i: (i, 0))
    part = lambda k: pl.BlockSpec((1, tm, d), lambda i, k=k: (k, i, 0))
    return pl.pallas_call(
        _combine_kernel,
        grid=(n // tm,),
        in_specs=[tok(d), tok(LANES)] + [part(k) for k in range(TOP_K)] + [_full((1, d))],
        out_specs=tok(d),
        out_shape=jax.ShapeDtypeStruct((n, d), F32),
        compiler_params=_params("arbitrary"),
        name="moe_combine",
    )(h2, gw, *([ys4] * TOP_K), w["g_final"])


def _routed_moe(h2, xt, route, gw, counts, w):
    n, d = h2.shape
    r = EXPERT_ROWS
    n_tiles = (n * TOP_K) // r + N_EXPERTS
    n_slots = n_tiles * r
    cnt = counts[0, :N_EXPERTS].astype(jnp.int32)
    padded = ((cnt + r - 1) // r) * r
    ends = jnp.cumsum(padded)
    starts = ends - padded
    eid = route[:, 0:TOP_K]
    rank = route[:, TOP_K:2 * TOP_K]
    onehot = eid[:, :, None] == jnp.arange(N_EXPERTS, dtype=jnp.int32)[None, None, :]
    slot = jnp.sum(jnp.where(onehot, starts[None, None, :], 0), axis=-1) + rank
    used = (ends[-1] // r).astype(jnp.int32)
    tile_block = jnp.minimum(jnp.arange(n_tiles, dtype=jnp.int32), used - 1)
    tile_expert = jnp.minimum(
        jnp.sum((tile_block[:, None] >= (ends // r)[None, :]).astype(jnp.int32), axis=1), N_EXPERTS - 1)
    tile_first = jnp.concatenate([jnp.ones((1,), jnp.int32),
                                  (tile_expert[1:] != tile_expert[:-1]).astype(jnp.int32)])
    slot_rows = slot.reshape(n // SC_CHUNK, SC_CHUNK, TOP_K).transpose(0, 2, 1).reshape(-1, SC_CHUNK)
    xs = _sc_scatter_rows(xt, slot_rows, n_slots)
    ys = _experts(tile_expert, tile_block, tile_first, used.reshape(1), xs, w)
    ys4 = _sc_gather(ys, slot.T.reshape(-1)).reshape(TOP_K, n, d)
    return _combine(h2, gw, ys4, w, tm=512)


def _moe_kernel(xt_ref, gate_ref, h2_ref, wgu_ref, bgu_ref, wd_ref, bd_ref, gfin_ref, y_ref, acc_sc):
    e = pl.program_id(1)

    @pl.when(e == 0)
    def _():
        acc_sc[...] = jnp.zeros(acc_sc.shape, F32)

    down = _swiglu_expert(_unpack_rows(xt_ref[...]), wgu_ref[0].astype(BF16), bgu_ref[0],
                          wd_ref[0].astype(BF16), bd_ref[0])
    gate = gate_ref[...]
    lane = lax.broadcasted_iota(jnp.int32, gate.shape, 1)
    ge = jnp.sum(jnp.where(lane == e, gate, 0.0), axis=1, keepdims=True)
    acc_sc[...] += ge * down

    @pl.when(e == N_EXPERTS - 1)
    def _():
        y_ref[...] = _rms(h2_ref[...] + acc_sc[...], gfin_ref[...])


def _moe(xt, gate, h2, w, tm):
    n, d = h2.shape
    tok = lambda width: pl.BlockSpec((tm, width), lambda i, e: (i, 0))
    return pl.pallas_call(
        _moe_kernel,
        grid=(n // tm, N_EXPERTS),
        in_specs=[tok(d // 2), tok(LANES), tok(d),
                  pl.BlockSpec((1, d, 2 * D_FF), lambda i, e: (e, 0, 0)),
                  pl.BlockSpec((1, 1, 2 * D_FF), lambda i, e: (e, 0, 0)),
                  pl.BlockSpec((1, D_FF, d), lambda i, e: (e, 0, 0)),
                  pl.BlockSpec((1, 1, d), lambda i, e: (e, 0, 0)),
                  _full((1, d))],
        out_specs=tok(d),
        out_shape=jax.ShapeDtypeStruct((n, d), F32),
        scratch_shapes=[pltpu.VMEM((tm, d), F32)],
        compiler_params=_params("arbitrary", "arbitrary"),
        name="moe",
    )(xt, gate, h2, w["w_gate_up"], w["b_gate_up"], w["w_down"], w["b_down"], w["g_final"])


def _rope_tables(pos):
    inv = ROPE_THETA ** (-jnp.arange(HALF_ROPE, dtype=F32) / HALF_ROPE)
    ang = pos.astype(F32)[:, None] * inv[None, :]
    cos, sin = jnp.cos(ang), jnp.sin(ang)
    n = pos.shape[0]
    pad = LANES - QK_NOPE - QK_ROPE
    rc = jnp.concatenate([jnp.ones((n, QK_NOPE), F32), cos, cos, jnp.zeros((n, pad), F32)], axis=1)
    ra = jnp.concatenate([jnp.zeros((n, QK_NOPE), F32), -sin, jnp.zeros((n, HALF_ROPE + pad), F32)], axis=1)
    rb = jnp.concatenate([jnp.zeros((n, QK_NOPE + HALF_ROPE), F32), sin, jnp.zeros((n, pad), F32)], axis=1)
    return rc, ra, rb


def _prepare(l, s_len, n_sample, past, g_mix, w_in, w_conv, g_q, w_uq, g_kv, w_uk, w_uv, g_conv_out, g_attn_out,
             w_out, g_cross, g_mem, w_mq, w_mk, w_mv, w_mo, g_ffn, w_router, b_router, w_gate_up, b_gate_up,
             w_down, b_down, g_final):
    h = MLA_HEADS
    row = lambda g: g.reshape(1, -1).astype(F32)
    head_pad = LANES - QK_NOPE - QK_ROPE
    win = w_in[l]
    win = jnp.concatenate([win[:, :OFF_KR], jnp.zeros((D_MODEL, QK_NOPE), F32), win[:, OFF_KR:],
                           jnp.zeros((D_MODEL, head_pad), F32)], axis=1)
    wuq = jnp.pad(w_uq[l].reshape(Q_LORA, h, QK_NOPE + QK_ROPE), ((0, 0), (0, 0), (0, head_pad)))
    wuk = jnp.pad(w_uk[l], ((0, 0), (0, 0), (0, LANES - QK_NOPE)))
    wuv_pair = w_uv[l].reshape(KV_LORA, h // 2, 2, V_HEAD)
    zeros = jnp.zeros((KV_LORA, h // 2, V_HEAD), F32)
    wuv = jnp.stack([jnp.concatenate([wuv_pair[:, :, 0], zeros], axis=-1),
                     jnp.concatenate([zeros, wuv_pair[:, :, 1]], axis=-1)], axis=2)
    wabs = jnp.pad(jnp.transpose(w_uk[l], (1, 2, 0)), ((0, 0), (0, LANES - QK_NOPE), (0, 0)))
    eye = jnp.eye(h, dtype=F32)
    wuv_heads = jnp.einsum("chd,hg->hcgd", w_uv[l], eye).reshape(h, KV_LORA, h * V_HEAD)
    wr = jnp.pad(w_router[l], ((0, 0), (0, LANES - N_EXPERTS)))
    br = jnp.pad(b_router[l], (0, LANES - N_EXPERTS)).reshape(1, LANES)
    return {
        "g_mix": row(g_mix[l]), "w_in": win.astype(BF16), "w_conv": w_conv[l].astype(F32),
        "g_q": row(g_q[l]), "w_uq": wuq.reshape(Q_LORA, h * LANES).astype(BF16),
        "g_kv": row(g_kv[l]), "w_uk": wuk.reshape(KV_LORA, h * LANES).astype(BF16),
        "w_uv": wuv.reshape(KV_LORA, h * LANES).T.astype(BF16), "w_abs": wabs.astype(BF16),
        "w_uv_heads": wuv_heads.astype(BF16),
        "g_conv_out": row(g_conv_out[l]), "g_attn_out": row(g_attn_out[l]),
        "w_out": w_out[l].astype(BF16), "g_cross": row(g_cross[l]), "g_mem": row(g_mem[l]),
        "w_mq": w_mq[l].astype(BF16), "w_mk": w_mk[l].astype(BF16), "w_mv": w_mv[l].astype(BF16),
        "w_mo": w_mo[l].astype(BF16), "g_ffn": row(g_ffn[l]),
        "w_router": wr.astype(BF16), "b_router": br.astype(F32),
        "w_gate_up": w_gate_up[l], "b_gate_up": b_gate_up[l].reshape(N_EXPERTS, 1, 2 * D_FF),
        "w_down": w_down[l], "b_down": b_down[l].reshape(N_EXPERTS, 1, D_MODEL),
        "g_final": row(g_final),
        "rope_prompt": _rope_tables(jnp.arange(s_len)),
        "rope_sample": _rope_tables(jnp.full((n_sample,), past)),
    }


def kernel(x_prompt, x_sample, mem_prompt, cache_conv, cache_ckv, cache_krope, cache_mem_k, cache_mem_v, page_table, g_mix, w_in, w_conv, g_q, w_uq, g_kv, w_uk, w_uv, g_conv_out, g_attn_out, w_out, g_cross, g_mem, w_mq, w_mk, w_mv, w_mo, g_ffn, w_router, b_router, w_gate_up, b_gate_up, w_down, b_down, g_final):
    bp, s_len, d = x_prompt.shape
    bd, t_len, _ = x_sample.shape
    depth = g_mix.shape[0]
    assert depth == 1 and t_len == 1, "kernel is written for one layer and one decode token per sequence"
    n_pages = page_table.shape[1]
    past = n_pages * PAGE_SIZE
    n_p = bp * s_len
    l = 0
    w = _prepare(l, s_len, bd, past, g_mix, w_in, w_conv, g_q, w_uq, g_kv, w_uk, w_uv, g_conv_out, g_attn_out,
                 w_out, g_cross, g_mem, w_mq, w_mk, w_mv, w_mo, g_ffn, w_router, b_router, w_gate_up,
                 b_gate_up, w_down, b_down, g_final)

    xs = x_sample.reshape(bd, d)
    prev0 = cache_conv[l, :, 0, :]
    prev1 = cache_conv[l, :, 1, :]
    q_s, qlat_s, lat_s, kr_s, ycn_s, u_s = _inproj_sample(xs, prev0, prev1, w)
    olat = _decode(page_table, jnp.transpose(qlat_s, (1, 0, 2)), jnp.transpose(q_s, (1, 0, 2)),
                   lat_s.reshape(bd, 1, KV_LORA), kr_s.reshape(bd, 1, QK_ROPE), cache_ckv,
                   jnp.swapaxes(cache_krope, 2, 3))
    ya_s = _uv_project(jnp.transpose(olat, (1, 0, 2)), w["w_uv_heads"])
    h1_s, qm_s = _mixout(xs, ycn_s, ya_s, w, tm=bd)
    o_s = _cross_sample(qm_s.astype(F32).reshape(bd, MEM_HEADS, MEM_HEAD_DIM), cache_mem_k[l], cache_mem_v[l], tb=4)
    h2_s, xt_s, gate_s, _, _, _ = _router(h1_s, o_s.reshape(bd, d).astype(BF16), w, tm=bd)
    y_s = _moe(xt_s, gate_s, h2_s, w, tm=bd)

    q, k, v, lat_p, kr_p, ycn_p, conv_p = _inproj_prompt(x_prompt, w, tm=512)
    ya_p = _attention(q, k, v, tq=512)
    mk_p, mv_p = _memory_kv(mem_prompt, w["g_mem"], w["w_mk"], w["w_mv"])
    xp = x_prompt.reshape(n_p, d)
    h2_p, xt_p, route_p, gw_p, cnt_p = _post_prompt(xp, ycn_p.reshape(n_p, -1), ya_p, mk_p, mv_p, w, tm=512)
    y_p = _routed_moe(h2_p, xt_p, route_p, gw_p, cnt_p, w)

    mem_shape = (1, bp, MEM_TOKENS, MEM_HEADS, MEM_HEAD_DIM)
    return (y_p.reshape(bp, s_len, d), y_s.reshape(bd, 1, d),
            conv_p[None], lat_p[None], kr_p[None], mk_p.reshape(mem_shape), mv_p.reshape(mem_shape),
            jnp.stack([prev1, u_s], axis=1)[None], lat_s.reshape(1, bd, 1, KV_LORA),
            kr_s.reshape(1, bd, 1, QK_ROPE))
```

```python
import functools

import jax
import jax.numpy as jnp
from jax import lax
from jax.experimental import pallas as pl
from jax.experimental.pallas import tpu as pltpu
from jax.experimental.pallas import tpu_sc as plsc

D_MODEL = 1024
CONV_WIDTH = 512
CONV_K = 3
MLA_HEADS = 8
QK_NOPE = 64
QK_ROPE = 32
V_HEAD = 64
Q_LORA = 384
KV_LORA = 256
ROPE_THETA = 10000.0
PAGE_SIZE = 128
MEM_TOKENS = 256
MEM_HEADS = 4
MEM_HEAD_DIM = D_MODEL // MEM_HEADS
N_EXPERTS = 32
TOP_K = 4
D_FF = D_MODEL
SWIGLU_LIMIT = 7.0
SWIGLU_ALPHA = 1.702
NORM_EPS = 1e-6

LANES = 128
HALF_ROPE = QK_ROPE // 2
IN_WIDTH = 3 * CONV_WIDTH + Q_LORA + KV_LORA + QK_ROPE
IN_WIDTH_PAD = 3 * CONV_WIDTH + Q_LORA + KV_LORA + LANES
OFF_CQ = 3 * CONV_WIDTH
OFF_CKV = OFF_CQ + Q_LORA
OFF_KR = OFF_CKV + KV_LORA
MLA_SCALE = (QK_NOPE + QK_ROPE) ** -0.5
MEM_SCALE = MEM_HEAD_DIM ** -0.5
LOG2_E = 1.4426950408889634
VMEM_LIMIT = 48 * 1024 * 1024
PAGES_PER_STEP = 32

BF16 = jnp.bfloat16
F32 = jnp.float32
NT_DIMS = (((1,), (1,)), ((), ()))


def _params(*sem):
    return pltpu.CompilerParams(dimension_semantics=sem, vmem_limit_bytes=VMEM_LIMIT)


def _rms(x, g):
    return x * lax.rsqrt(jnp.mean(x * x, axis=-1, keepdims=True) + NORM_EPS) * g


def _dot(a, b):
    return jnp.dot(a, b, preferred_element_type=F32)


def _dot_nt(a, b):
    return lax.dot_general(a, b, NT_DIMS, preferred_element_type=F32)


def _rope_group(x, rc, ra, rb):
    return x * rc + pltpu.roll(x, LANES - HALF_ROPE, 1) * ra + pltpu.roll(x, HALF_ROPE, 1) * rb


def _full(shape):
    return pl.BlockSpec(shape, lambda *_: (0,) * len(shape))


def _memkv_kernel(mem_ref, g_ref, wk_ref, wv_ref, k_ref, v_ref):
    m = _rms(mem_ref[0], g_ref[...]).astype(BF16)
    k_ref[0] = _dot(m, wk_ref[...])
    v_ref[0] = _dot(m, wv_ref[...])


def _memory_kv(mem, g_mem, w_mk, w_mv):
    b, n, d = mem.shape
    blk = pl.BlockSpec((1, n, d), lambda i: (i, 0, 0))
    return pl.pallas_call(
        _memkv_kernel,
        grid=(b,),
        in_specs=[blk, _full((1, d)), _full((d, d)), _full((d, d))],
        out_specs=[blk, blk],
        out_shape=[jax.ShapeDtypeStruct((b, n, d), F32)] * 2,
        compiler_params=_params("arbitrary"),
        name="memory_kv",
    )(mem, g_mem, w_mk, w_mv)


def _inproj_common(x, gmix, win, gq, wuq, gkv, rc, ra, rb):
    a = _rms(x, gmix).astype(BF16)
    z = _dot(a, win)
    b_g = z[:, 0:CONV_WIDTH]
    u = z[:, CONV_WIDTH:2 * CONV_WIDTH] * z[:, 2 * CONV_WIDTH:3 * CONV_WIDTH]
    cq = _rms(z[:, OFF_CQ:OFF_CKV], gq).astype(BF16)
    q = _dot(cq, wuq)
    q_heads = [_rope_group(q[:, h * LANES:(h + 1) * LANES], rc, ra, rb) for h in range(MLA_HEADS)]
    lat = _rms(z[:, OFF_CKV:OFF_KR], gkv)
    kr = _rope_group(z[:, OFF_KR:OFF_KR + LANES], rc, ra, rb)
    return b_g, u, q_heads, lat, kr


def _inproj_prompt_kernel(x_ref, gmix_ref, win_ref, wconv_ref, gq_ref, wuq_ref, gkv_ref, wuk_ref, wuv_ref,
                          gco_ref, rc_ref, ra_ref, rb_ref,
                          q_ref, k_ref, v_ref, lat_ref, kr_ref, ycn_ref, conv_ref, ubuf, *, tm):
    j = pl.program_id(1)
    b_g, u, q_heads, lat, kr = _inproj_common(
        x_ref[0], gmix_ref[...], win_ref[...], gq_ref[...], wuq_ref[...], gkv_ref[...],
        rc_ref[...], ra_ref[...], rb_ref[...])
    for h in range(MLA_HEADS):
        q_ref[0, h] = (q_heads[h] * (MLA_SCALE * LOG2_E)).astype(BF16)
    lat_ref[0] = lat
    kr_ref[0] = kr[:, QK_NOPE:QK_NOPE + QK_ROPE]
    lat_b = lat.astype(BF16)
    kn = _dot(lat_b, wuk_ref[...])
    v_t = _dot_nt(wuv_ref[...], lat_b)
    for h in range(MLA_HEADS):
        k_ref[0, h] = (kn[:, h * LANES:(h + 1) * LANES] + kr).astype(BF16)
        v_ref[0, h, 0] = v_t[h * V_HEAD:(h + 1) * V_HEAD, :].astype(BF16)

    @pl.when(j == 0)
    def _():
        ubuf[0:8, :] = jnp.zeros((8, CONV_WIDTH), F32)

    ubuf[8:8 + tm, :] = u
    u1 = ubuf[7:7 + tm, :]
    u2 = ubuf[6:6 + tm, :]
    wc = wconv_ref[...]
    yc = b_g * (wc[0:1, :] * u2 + wc[1:2, :] * u1 + wc[2:3, :] * u)
    ycn_ref[0] = _rms(yc, gco_ref[...]).astype(BF16)
    ubuf[0:8, :] = ubuf[tm:tm + 8, :]
    conv_ref[0] = u[tm - (CONV_K - 1):tm, :]


def _inproj_prompt(x, w, tm):
    b, s, d = x.shape
    h = MLA_HEADS
    tok = lambda n: pl.BlockSpec((1, tm, n), lambda i, j: (i, j, 0))
    head = pl.BlockSpec((1, h, tm, LANES), lambda i, j: (i, 0, j, 0))
    rope = pl.BlockSpec((tm, LANES), lambda i, j: (j, 0))
    qkv_shape = jax.ShapeDtypeStruct((b, h, s, LANES), BF16)
    return pl.pallas_call(
        functools.partial(_inproj_prompt_kernel, tm=tm),
        grid=(b, s // tm),
        in_specs=[tok(d), _full((1, d)), _full((d, IN_WIDTH_PAD)), _full((CONV_K, CONV_WIDTH)),
                  _full((1, Q_LORA)), _full((Q_LORA, h * LANES)), _full((1, KV_LORA)),
                  _full((KV_LORA, h * LANES)), _full((h * V_HEAD, KV_LORA)), _full((1, CONV_WIDTH)),
                  rope, rope, rope],
        out_specs=[head, head, pl.BlockSpec((1, h, 1, V_HEAD, tm), lambda i, j: (i, 0, j, 0, 0)),
                   tok(KV_LORA), tok(QK_ROPE), tok(CONV_WIDTH),
                   pl.BlockSpec((1, CONV_K - 1, CONV_WIDTH), lambda i, j: (i, 0, 0))],
        out_shape=[qkv_shape, qkv_shape, jax.ShapeDtypeStruct((b, h, s // tm, V_HEAD, tm), BF16),
                   jax.ShapeDtypeStruct((b, s, KV_LORA), F32),
                   jax.ShapeDtypeStruct((b, s, QK_ROPE), F32),
                   jax.ShapeDtypeStruct((b, s, CONV_WIDTH), BF16),
                   jax.ShapeDtypeStruct((b, CONV_K - 1, CONV_WIDTH), F32)],
        scratch_shapes=[pltpu.VMEM((tm + 8, CONV_WIDTH), F32)],
        compiler_params=_params("arbitrary", "arbitrary"),
        name="inproj_prompt",
    )(x, w["g_mix"], w["w_in"], w["w_conv"], w["g_q"], w["w_uq"], w["g_kv"], w["w_uk"], w["w_uv"],
      w["g_conv_out"], *w["rope_prompt"])


def _inproj_sample_kernel(x_ref, gmix_ref, win_ref, wconv_ref, gq_ref, wuq_ref, gkv_ref, wabs_ref,
                          gco_ref, rc_ref, ra_ref, rb_ref, p0_ref, p1_ref,
                          q_ref, qlat_ref, lat_ref, kr_ref, ycn_ref, u_ref):
    b_g, u, q_heads, lat, kr = _inproj_common(
        x_ref[...], gmix_ref[...], win_ref[...], gq_ref[...], wuq_ref[...], gkv_ref[...],
        rc_ref[...], ra_ref[...], rb_ref[...])
    for h in range(MLA_HEADS):
        qh = q_heads[h].astype(BF16)
        q_ref[h] = qh
        qlat_ref[h] = _dot(qh, wabs_ref[h]).astype(BF16)
    lat_ref[...] = lat
    kr_ref[...] = kr[:, QK_NOPE:QK_NOPE + QK_ROPE]
    wc = wconv_ref[...]
    yc = b_g * (wc[0:1, :] * p0_ref[...] + wc[1:2, :] * p1_ref[...] + wc[2:3, :] * u)
    ycn_ref[...] = _rms(yc, gco_ref[...]).astype(BF16)
    u_ref[...] = u


def _inproj_sample(x, prev0, prev1, w):
    n, d = x.shape
    h = MLA_HEADS
    return pl.pallas_call(
        _inproj_sample_kernel,
        grid=(1,),
        in_specs=[_full((n, d)), _full((1, d)), _full((d, IN_WIDTH_PAD)), _full((CONV_K, CONV_WIDTH)),
                  _full((1, Q_LORA)), _full((Q_LORA, h * LANES)), _full((1, KV_LORA)),
                  _full((h, LANES, KV_LORA)), _full((1, CONV_WIDTH)),
                  _full((n, LANES)), _full((n, LANES)), _full((n, LANES)),
                  _full((n, CONV_WIDTH)), _full((n, CONV_WIDTH))],
        out_specs=[_full((h, n, LANES)), _full((h, n, KV_LORA)), _full((n, KV_LORA)), _full((n, QK_ROPE)),
                   _full((n, CONV_WIDTH)), _full((n, CONV_WIDTH))],
        out_shape=[jax.ShapeDtypeStruct((h, n, LANES), BF16),
                   jax.ShapeDtypeStruct((h, n, KV_LORA), BF16),
                   jax.ShapeDtypeStruct((n, KV_LORA), F32),
                   jax.ShapeDtypeStruct((n, QK_ROPE), F32),
                   jax.ShapeDtypeStruct((n, CONV_WIDTH), BF16),
                   jax.ShapeDtypeStruct((n, CONV_WIDTH), F32)],
        compiler_params=_params("arbitrary"),
        name="inproj_sample",
    )(x, w["g_mix"], w["w_in"], w["w_conv"], w["g_q"], w["w_uq"], w["g_kv"], w["w_abs"],
      w["g_conv_out"], *w["rope_sample"], prev0, prev1)


def _softmax_step(s, v, m_sc, l_sc, acc_sc):
    m_prev = m_sc[...]
    m_next = jnp.maximum(m_prev, jnp.max(s, axis=1, keepdims=True))
    p = jnp.exp(s - m_next[:, 0:1])
    alpha = jnp.exp(m_prev - m_next)
    l_sc[...] = alpha * l_sc[...] + jnp.sum(p, axis=1, keepdims=True)
    pv = _dot(p.astype(BF16), v)
    acc_sc[...] = acc_sc[...] * alpha[:, 0:1] + pv
    m_sc[...] = m_next


def _attn_kernel(q_ref, k_ref, v_ref, o_ref, m_sc, l_sc, acc_sc, s0_sc, s1_sc, *, tq):
    qi = pl.program_id(2)
    key = lax.broadcasted_iota(jnp.int32, (tq, tq), 0)
    qry = lax.broadcasted_iota(jnp.int32, (tq, tq), 1)
    m_sc[...] = jnp.full(m_sc.shape, -jnp.inf, F32)
    l_sc[...] = jnp.zeros(l_sc.shape, F32)
    acc_sc[...] = jnp.zeros(acc_sc.shape, F32)

    def scores(j, buf):
        for hh in range(2):
            k = k_ref[0, hh, pl.ds(pl.multiple_of(j * tq, tq), tq), :]
            buf[hh] = _dot_nt(k, q_ref[0, hh])

    def consume(j, buf, masked):
        for hh in range(2):
            s = buf[hh]
            if masked:
                s = jnp.where(key <= qry, s, -jnp.inf)
            m_prev = m_sc[hh]
            m_next = jnp.maximum(m_prev, jnp.max(s, axis=0, keepdims=True))
            p = jnp.exp2(s - m_next)
            alpha = jnp.exp2(m_prev - m_next)
            l_sc[hh] = alpha * l_sc[hh] + jnp.sum(p, axis=0, keepdims=True)
            acc_sc[hh] = acc_sc[hh] * alpha + _dot(v_ref[0, hh, j], p.astype(BF16))
            m_sc[hh] = m_next

    def pair(jj, carry):
        j = 2 * jj
        scores(j + 1, s1_sc)
        consume(j, s0_sc, False)
        scores(j + 2, s0_sc)
        consume(j + 1, s1_sc, False)
        return carry

    scores(0, s0_sc)
    lax.fori_loop(0, qi // 2, pair, 0)

    @pl.when(qi % 2 == 0)
    def _():
        consume(qi, s0_sc, True)

    @pl.when(qi % 2 == 1)
    def _():
        scores(qi, s1_sc)
        consume(qi - 1, s0_sc, False)
        consume(qi, s1_sc, True)

    o_ref[0] = jnp.concatenate([acc_sc[0] / l_sc[0], acc_sc[1] / l_sc[1]], axis=0)


def _attention(q, k, v_t, tq):
    b, h, s, _ = q.shape
    qspec = pl.BlockSpec((1, 2, tq, LANES), lambda i, p, j: (i, p, j, 0))
    kspec = pl.BlockSpec((1, 2, s, LANES), lambda i, p, j: (i, p, 0, 0))
    vspec = pl.BlockSpec((1, 2, s // tq, V_HEAD, tq), lambda i, p, j: (i, p, 0, 0, 0))
    return pl.pallas_call(
        functools.partial(_attn_kernel, tq=tq),
        grid=(b, h // 2, s // tq),
        in_specs=[qspec, kspec, vspec],
        out_specs=pl.BlockSpec((1, LANES, tq), lambda i, p, j: (i, p, j)),
        out_shape=jax.ShapeDtypeStruct((b, h * V_HEAD, s), F32),
        scratch_shapes=[pltpu.VMEM((2, 1, tq), F32), pltpu.VMEM((2, 1, tq), F32), pltpu.VMEM((2, V_HEAD, tq), F32),
                        pltpu.VMEM((2, tq, tq), F32), pltpu.VMEM((2, tq, tq), F32)],
        compiler_params=_params("arbitrary", "arbitrary", "arbitrary"),
        name="mla_prompt_attention",
    )(q, k, v_t)


def _decode_kernel(pt_ref, qlat_ref, q_ref, lat_ref, krn_ref, ckv_hbm, kr_hbm, o_ref,
                   ckv_buf, kr_buf, sems, m_sc, l_sc, acc_sc, *, n_steps, n_seqs):
    npg = PAGES_PER_STEP
    b = pl.program_id(0)
    step = pl.program_id(1)
    t = b * n_steps + step
    slot = t % 2

    def start_pages(seq, st, sl):
        for i in range(npg):
            page = pt_ref[seq, st * npg + i]
            pltpu.make_async_copy(ckv_hbm.at[0, page], ckv_buf.at[sl, i], sems.at[0, sl]).start(priority=i % 2)
            pltpu.make_async_copy(kr_hbm.at[0, page], kr_buf.at[sl, i], sems.at[1, sl]).start(priority=(i + 1) % 2)

    def wait_pages(sl):
        pltpu.make_async_copy(ckv_hbm.at[0, pl.ds(0, npg)], ckv_buf.at[sl], sems.at[0, sl]).wait()
        pltpu.make_async_copy(kr_hbm.at[0, pl.ds(0, npg)], kr_buf.at[sl], sems.at[1, sl]).wait()

    last = n_seqs * n_steps - 1

    @pl.when(t == 0)
    def _():
        start_pages(0, 0, 0)

    @pl.when(step == 0)
    def _():
        m_sc[...] = jnp.full(m_sc.shape, -jnp.inf, F32)
        l_sc[...] = jnp.zeros(l_sc.shape, F32)
        acc_sc[...] = jnp.zeros(acc_sc.shape, F32)

    ql = qlat_ref[0]
    qr = q_ref[0][:, QK_NOPE:QK_NOPE + QK_ROPE]
    wait_pages(slot)
    nxt = jnp.minimum(t + 1, last)
    start_pages(nxt // n_steps, nxt % n_steps, 1 - slot)
    ckv = ckv_buf[slot].reshape(npg * PAGE_SIZE, KV_LORA).astype(BF16)
    kr_t = jnp.concatenate([kr_buf[slot, i] for i in range(npg)], axis=1).astype(BF16)
    s = (_dot_nt(ql, ckv) + _dot(qr, kr_t)) * MLA_SCALE
    _softmax_step(s, ckv, m_sc, l_sc, acc_sc)

    @pl.when(step == n_steps - 1)
    def _():
        lat = lat_ref[0]
        s_new = (jnp.sum(ql.astype(F32) * lat, axis=1, keepdims=True)
                 + jnp.sum(qr.astype(F32) * krn_ref[0], axis=1, keepdims=True)) * MLA_SCALE
        m_prev = m_sc[...]
        m_next = jnp.maximum(m_prev, s_new)
        p_new = jnp.exp(s_new - m_next[:, 0:1])
        alpha = jnp.exp(m_prev - m_next)
        l_fin = alpha * l_sc[...] + p_new
        acc = acc_sc[...] * alpha[:, 0:1] + p_new * lat
        o_ref[0] = acc / l_fin[:, 0:1]

    @pl.when(t == last)
    def _():
        wait_pages(1 - slot)


def _decode(page_table, qlat, q, lat, krn, cache_ckv, cache_krope):
    bd, n_pages = page_table.shape
    npg = PAGES_PER_STEP
    n_steps = n_pages // npg
    h = MLA_HEADS
    per_b = lambda *shape: pl.BlockSpec((1,) + shape, lambda b, s, pt: (b,) + (0,) * len(shape))
    hbm = pl.BlockSpec(memory_space=pl.ANY)
    grid_spec = pltpu.PrefetchScalarGridSpec(
        num_scalar_prefetch=1,
        grid=(bd, n_steps),
        in_specs=[per_b(h, KV_LORA), per_b(h, LANES), per_b(1, KV_LORA), per_b(1, QK_ROPE), hbm, hbm],
        out_specs=per_b(h, KV_LORA),
        scratch_shapes=[pltpu.VMEM((2, npg, PAGE_SIZE, KV_LORA), F32), pltpu.VMEM((2, npg, QK_ROPE, PAGE_SIZE), F32),
                        pltpu.SemaphoreType.DMA((2, 2)),
                        pltpu.VMEM((h, LANES), F32), pltpu.VMEM((h, LANES), F32), pltpu.VMEM((h, KV_LORA), F32)],
    )
    return pl.pallas_call(
        functools.partial(_decode_kernel, n_steps=n_steps, n_seqs=bd),
        grid_spec=grid_spec,
        out_shape=jax.ShapeDtypeStruct((bd, h, KV_LORA), F32),
        compiler_params=_params("arbitrary", "arbitrary"),
        name="mla_decode",
    )(page_table, qlat, q, lat, krn, cache_ckv, cache_krope)


def _uv_kernel(olat_ref, wuv_ref, ya_ref):
    acc = None
    for h in range(MLA_HEADS):
        part = _dot(olat_ref[h].astype(BF16), wuv_ref[h])
        acc = part if acc is None else acc + part
    ya_ref[...] = acc


def _uv_project(olat, wuv_heads):
    h, n, c = olat.shape
    width = MLA_HEADS * V_HEAD
    return pl.pallas_call(
        _uv_kernel,
        grid=(1,),
        in_specs=[_full((h, n, c)), _full((h, c, width))],
        out_specs=_full((n, width)),
        out_shape=jax.ShapeDtypeStruct((n, width), F32),
        compiler_params=_params("arbitrary"),
        name="decode_uv",
    )(olat, wuv_heads)


def _mixout_math(x, ycn, ya, ga, wout, gcross, wmq):
    yan = _rms(ya, ga).astype(BF16)
    mixed = jnp.concatenate([ycn, yan], axis=1)
    h1 = x + _dot(mixed, wout)
    return h1, _dot(_rms(h1, gcross).astype(BF16), wmq).astype(BF16)


def _mixout_kernel(x_ref, ycn_ref, ya_ref, ga_ref, wout_ref, gcross_ref, wmq_ref, h1_ref, qm_ref):
    h1_ref[...], qm_ref[...] = _mixout_math(x_ref[...], ycn_ref[...], ya_ref[...], ga_ref[...], wout_ref[...],
                                            gcross_ref[...], wmq_ref[...])


def _mixout(x, ycn, ya, w, tm):
    n, d = x.shape
    width = MLA_HEADS * V_HEAD
    mix = CONV_WIDTH + width
    tok = lambda width: pl.BlockSpec((tm, width), lambda i: (i, 0))
    return pl.pallas_call(
        _mixout_kernel,
        grid=(n // tm,),
        in_specs=[tok(d), tok(CONV_WIDTH), tok(width), _full((1, width)),
                  _full((mix, d)), _full((1, d)), _full((d, d))],
        out_specs=[tok(d), tok(d)],
        out_shape=[jax.ShapeDtypeStruct((n, d), F32), jax.ShapeDtypeStruct((n, d), BF16)],
        compiler_params=_params("arbitrary"),
        name="mix_out",
    )(x, ycn, ya, w["g_attn_out"], w["w_out"], w["g_cross"], w["w_mq"])


def _cross_math(q, k_ref, v_ref):
    outs = []
    for h in range(MEM_HEADS):
        sl = slice(h * MEM_HEAD_DIM, (h + 1) * MEM_HEAD_DIM)
        s = _dot_nt(q[:, sl], k_ref[0, :, sl].astype(BF16)) * MEM_SCALE
        e = jnp.exp(s - jnp.max(s, axis=1, keepdims=True))
        p = e / jnp.sum(e, axis=1, keepdims=True)
        outs.append(_dot(p.astype(BF16), v_ref[0, :, sl].astype(BF16)))
    return jnp.concatenate(outs, axis=1).astype(BF16)


def _cross_sample_kernel(q_ref, k_ref, v_ref, o_ref, *, tb):
    for t in range(tb):
        s = jnp.sum(k_ref[t] * q_ref[t][None], axis=2, keepdims=True) * MEM_SCALE
        e = jnp.exp(s - jnp.max(s, axis=0, keepdims=True))
        p = e / jnp.sum(e, axis=0, keepdims=True)
        o_ref[t] = jnp.sum(p * v_ref[t], axis=0)


def _cross_sample(qm, mk, mv, tb):
    n = qm.shape[0]
    tok = pl.BlockSpec((tb, MEM_HEADS, MEM_HEAD_DIM), lambda i: (i, 0, 0))
    mem = pl.BlockSpec((tb, MEM_TOKENS, MEM_HEADS, MEM_HEAD_DIM), lambda i: (i, 0, 0, 0))
    return pl.pallas_call(
        functools.partial(_cross_sample_kernel, tb=tb),
        grid=(n // tb,),
        in_specs=[tok, mem, mem],
        out_specs=tok,
        out_shape=jax.ShapeDtypeStruct((n, MEM_HEADS, MEM_HEAD_DIM), F32),
        compiler_params=_params("arbitrary"),
        name="cross_sample",
    )(qm, mk, mv)


def _router_math(h1, o, wmo, gffn, wr, br, cnt_prev):
    h2 = h1 + _dot(o, wmo)
    xt = _rms(h2, gffn).astype(BF16)
    logits = _dot(xt, wr) + br
    tm = logits.shape[0]
    lane = lax.broadcasted_iota(jnp.int32, logits.shape, 1)
    logits = jnp.where(lane < N_EXPERTS, logits, -jnp.inf)
    work = logits
    sel = lane < 0
    picks = []
    for k in range(TOP_K):
        m = jnp.max(work, axis=1, keepdims=True)
        idx = jnp.min(jnp.where(work == m, lane, LANES), axis=1, keepdims=True)
        hit = lane == idx
        sel = jnp.logical_or(sel, hit)
        work = jnp.where(hit, -jnp.inf, work)
        picks.append((m, idx, hit))
    top = picks[0][0]
    e = jnp.where(sel, jnp.exp(logits - top), 0.0)
    denom = jnp.sum(e, axis=1, keepdims=True)

    sel_f = sel.astype(F32)
    earlier = (lax.broadcasted_iota(jnp.int32, (tm, tm), 0) > lax.broadcasted_iota(jnp.int32, (tm, tm), 1))
    rank_all = _dot(earlier.astype(BF16), sel_f.astype(BF16)) + cnt_prev
    route = jnp.zeros(logits.shape, jnp.int32)
    gw = jnp.zeros(logits.shape, F32)
    for k, (m, idx, hit) in enumerate(picks):
        rank = jnp.sum(jnp.where(hit, rank_all, 0.0), axis=1, keepdims=True).astype(jnp.int32)
        route = jnp.where(lane == k, idx, route)
        route = jnp.where(lane == TOP_K + k, rank, route)
        gw = jnp.where(lane == k, jnp.exp(m - top) / denom, gw)
    cnt = cnt_prev + jnp.sum(sel_f, axis=0, keepdims=True)
    return h2, _pack_rows(xt.astype(F32)), e / denom, route, gw, cnt


def _router_kernel(h1_ref, o_ref, wmo_ref, gffn_ref, wr_ref, br_ref,
                   h2_ref, xt_ref, gate_ref, route_ref, gw_ref, cnt_ref, cnt_sc):
    @pl.when(pl.program_id(0) == 0)
    def _():
        cnt_sc[...] = jnp.zeros(cnt_sc.shape, F32)

    h2_ref[...], xt_ref[...], gate_ref[...], route_ref[...], gw_ref[...], cnt = _router_math(
        h1_ref[...], o_ref[...], wmo_ref[...], gffn_ref[...], wr_ref[...], br_ref[...], cnt_sc[...])
    cnt_sc[...] = cnt
    cnt_ref[...] = cnt


def _post_prompt_kernel(x_ref, ycn_ref, yat_ref, k_ref, v_ref, ga_ref, wout_ref, gcross_ref, wmq_ref,
                        wmo_ref, gffn_ref, wr_ref, br_ref, h2_ref, xt_ref, route_ref, gw_ref, cnt_ref, cnt_sc):
    @pl.when(pl.program_id(0) == 0)
    def _():
        cnt_sc[...] = jnp.zeros(cnt_sc.shape, F32)

    h1, qm = _mixout_math(x_ref[...], ycn_ref[...], yat_ref[0].T, ga_ref[...], wout_ref[...], gcross_ref[...],
                          wmq_ref[...])
    o = _cross_math(qm, k_ref, v_ref)
    h2_ref[...], xt_ref[...], _, route_ref[...], gw_ref[...], cnt = _router_math(
        h1, o, wmo_ref[...], gffn_ref[...], wr_ref[...], br_ref[...], cnt_sc[...])
    cnt_sc[...] = cnt
    cnt_ref[...] = cnt


def _post_prompt(x, ycn, ya_t, mk, mv, w, tm):
    n, d = x.shape
    width = MLA_HEADS * V_HEAD
    tiles = ya_t.shape[2] // tm
    tok = lambda width: pl.BlockSpec((tm, width), lambda i: (i, 0))
    mem = pl.BlockSpec((1, MEM_TOKENS, d), lambda i: (i // tiles, 0, 0))
    return pl.pallas_call(
        _post_prompt_kernel,
        grid=(n // tm,),
        in_specs=[tok(d), tok(CONV_WIDTH), pl.BlockSpec((1, width, tm), lambda i: (i // tiles, 0, i % tiles)),
                  mem, mem, _full((1, width)), _full((CONV_WIDTH + width, d)), _full((1, d)), _full((d, d)),
                  _full((d, d)), _full((1, d)), _full((d, LANES)), _full((1, LANES))],
        out_specs=[tok(d), tok(d // 2), tok(LANES), tok(LANES), _full((1, LANES))],
        out_shape=[jax.ShapeDtypeStruct((n, d), F32), jax.ShapeDtypeStruct((n, d // 2), jnp.uint32),
                   jax.ShapeDtypeStruct((n, LANES), jnp.int32), jax.ShapeDtypeStruct((n, LANES), F32),
                   jax.ShapeDtypeStruct((1, LANES), F32)],
        scratch_shapes=[pltpu.VMEM((1, LANES), F32)],
        compiler_params=_params("arbitrary"),
        name="post_prompt",
    )(x, ycn, ya_t, mk, mv, w["g_attn_out"], w["w_out"], w["g_cross"], w["w_mq"], w["w_mo"], w["g_ffn"],
      w["w_router"], w["b_router"])


def _router(h1, o, w, tm):
    n, d = h1.shape
    tok = lambda width: pl.BlockSpec((tm, width), lambda i: (i, 0))
    return pl.pallas_call(
        _router_kernel,
        grid=(n // tm,),
        in_specs=[tok(d), tok(d), _full((d, d)), _full((1, d)), _full((d, LANES)), _full((1, LANES))],
        out_specs=[tok(d), tok(d // 2), tok(LANES), tok(LANES), tok(LANES), _full((1, LANES))],
        out_shape=[jax.ShapeDtypeStruct((n, d), F32), jax.ShapeDtypeStruct((n, d // 2), jnp.uint32),
                   jax.ShapeDtypeStruct((n, LANES), F32), jax.ShapeDtypeStruct((n, LANES), jnp.int32),
                   jax.ShapeDtypeStruct((n, LANES), F32), jax.ShapeDtypeStruct((1, LANES), F32)],
        scratch_shapes=[pltpu.VMEM((1, LANES), F32)],
        compiler_params=_params("arbitrary"),
        name="router",
    )(h1, o, w["w_mo"], w["g_ffn"], w["w_router"], w["b_router"])


SC_CORES = 2
SC_SUBCORES = 16
SC_WORKERS = SC_CORES * SC_SUBCORES
SC_CHUNK = 32
EXPERT_ROWS = 512
EXPERTS_VMEM_LIMIT = 56 * 1024 * 1024


def _sc_gather(table, idx):
    b = idx.shape[0]
    d = table.shape[1]
    per_worker = b // SC_WORKERS
    n_chunks = per_worker // SC_CHUNK
    assert per_worker * SC_WORKERS == b and n_chunks * SC_CHUNK == per_worker
    mesh = plsc.VectorSubcoreMesh(core_axis_name="c", subcore_axis_name="s")

    assert n_chunks % 2 == 0
    row_buf = pltpu.VMEM((SC_CHUNK, d), table.dtype)

    @functools.partial(
        pl.kernel, mesh=mesh,
        out_type=jax.ShapeDtypeStruct((b, d), table.dtype),
        scratch_types=[pltpu.VMEM((per_worker,), jnp.int32), row_buf, row_buf] + [pltpu.SemaphoreType.DMA] * 4,
    )
    def gather(table_hbm, idx_hbm, out_hbm, idx_v, rows0, rows1, g0, g1, w0, w1):
        wid = lax.axis_index("s") * SC_CORES + lax.axis_index("c")
        base = pl.multiple_of(wid * per_worker, 8)
        rows, gsem, wsem = (rows0, rows1), (g0, g1), (w0, w1)
        pltpu.sync_copy(idx_hbm.at[pl.ds(base, per_worker)], idx_v)

        def fetch(c, buf):
            ids = idx_v.at[pl.ds(pl.multiple_of(c * SC_CHUNK, 8), SC_CHUNK)]
            return pltpu.make_async_copy(table_hbm.at[ids], rows[buf], gsem[buf])

        def flush(c, buf):
            dst = out_hbm.at[pl.ds(pl.multiple_of(base + c * SC_CHUNK, 8), SC_CHUNK)]
            return pltpu.make_async_copy(rows[buf], dst, wsem[buf])

        fetch(0, 0).start()

        @pl.loop(0, n_chunks, step=2)
        def _(c0):
            for buf in (0, 1):
                c = c0 + buf
                fetch(c, buf).wait()

                @pl.when(c + 1 < n_chunks)
                def _():
                    @pl.when(c >= 1)
                    def _():
                        flush(c - 1, 1 - buf).wait()

                    fetch(c + 1, 1 - buf).start()

                flush(c, buf).start()

        flush(n_chunks - 2, 0).wait()
        flush(n_chunks - 1, 1).wait()

    return gather(table, idx)


def _sc_scatter_rows(x, slot_rows, n_out):
    n, d = x.shape
    per_worker = n // SC_WORKERS
    n_chunks = per_worker // SC_CHUNK
    assert per_worker * SC_WORKERS == n and n_chunks * SC_CHUNK == per_worker and n_chunks % 2 == 0
    assert slot_rows.shape == (n // SC_CHUNK * TOP_K, SC_CHUNK)
    idx_rows = n_chunks * TOP_K
    mesh = plsc.VectorSubcoreMesh(core_axis_name="c", subcore_axis_name="s")
    row_buf = pltpu.VMEM((SC_CHUNK, d), x.dtype)

    @functools.partial(
        pl.kernel, mesh=mesh,
        out_type=jax.ShapeDtypeStruct((n_out, d), x.dtype),
        scratch_types=[pltpu.VMEM((idx_rows, SC_CHUNK), jnp.int32), row_buf, row_buf]
        + [pltpu.SemaphoreType.DMA] * 4,
    )
    def scatter(x_hbm, idx_hbm, out_hbm, idx_v, rows0, rows1, r0, r1, s0, s1):
        wid = lax.axis_index("s") * SC_CORES + lax.axis_index("c")
        base = pl.multiple_of(wid * per_worker, 8)
        rows, rsem, ssem = (rows0, rows1), (r0, r1), (s0, s1)
        pltpu.sync_copy(idx_hbm.at[pl.ds(pl.multiple_of(wid * idx_rows, 8), idx_rows)], idx_v)

        def fetch(c, buf):
            src = x_hbm.at[pl.ds(pl.multiple_of(base + c * SC_CHUNK, 8), SC_CHUNK)]
            return pltpu.make_async_copy(src, rows[buf], rsem[buf])

        def spread(c, k, buf):
            return pltpu.make_async_copy(rows[buf], out_hbm.at[idx_v.at[c * TOP_K + k]], ssem[buf])

        fetch(0, 0).start()

        @pl.loop(0, n_chunks, step=2)
        def _(c0):
            for buf in (0, 1):
                c = c0 + buf
                fetch(c, buf).wait()

                @pl.when(c + 1 < n_chunks)
                def _():
                    @pl.when(c >= 1)
                    def _():
                        for k in range(TOP_K):
                            spread(c - 1, k, 1 - buf).wait()

                    fetch(c + 1, 1 - buf).start()

                for k in range(TOP_K):
                    spread(c, k, buf).start()

        for k in range(TOP_K):
            spread(n_chunks - 2, k, 0).wait()
        for k in range(TOP_K):
            spread(n_chunks - 1, k, 1).wait()

    return scatter(x, slot_rows)


def _swiglu_expert(x, wgu, bgu, wd, bd):
    gu = _dot(x, wgu) + bgu
    g = jnp.minimum(gu[:, :D_FF], SWIGLU_LIMIT)
    u = jnp.clip(gu[:, D_FF:], -SWIGLU_LIMIT, SWIGLU_LIMIT)
    hdn = (u + 1.0) * (g * (1.0 / (1.0 + jnp.exp(-SWIGLU_ALPHA * g))))
    return _dot(hdn.astype(BF16), wd) + bd


def _pack_rows(x):
    bits = lax.bitcast_convert_type(x, jnp.uint32)
    n = x.shape[1] // 2
    return bits[:, :n] | (bits[:, n:] >> 16)


def _unpack_rows(p):
    hi = lax.bitcast_convert_type(p & jnp.uint32(0xFFFF0000), F32)
    lo = lax.bitcast_convert_type(p << 16, F32)
    return jnp.concatenate([hi, lo], axis=1).astype(BF16)


def _experts_kernel(te_ref, tb_ref, first_ref, used_ref, xs_ref, wgu_ref, bgu_ref, wd_ref, bd_ref, ys_ref,
                    wgu_sc, wd_sc):
    del te_ref, tb_ref
    t = pl.program_id(0)

    @pl.when(t < used_ref[0])
    def _():
        @pl.when(first_ref[t] == 1)
        def _():
            wgu_sc[...] = wgu_ref[0].astype(BF16)
            wd_sc[...] = wd_ref[0].astype(BF16)

        ys_ref[...] = _swiglu_expert(_unpack_rows(xs_ref[...]), wgu_sc[...], bgu_ref[0], wd_sc[...], bd_ref[0])


def _experts(tile_expert, tile_block, tile_first, used, xs, w):
    n_slots = xs.shape[0]
    d = D_MODEL
    r = EXPERT_ROWS
    rows = lambda width: pl.BlockSpec((r, width), lambda t, te, tb, tf, u: (tb[t], 0))
    per_e = lambda *shape: pl.BlockSpec((1,) + shape, lambda t, te, tb, tf, u: (te[t],) + (0,) * len(shape))
    grid_spec = pltpu.PrefetchScalarGridSpec(
        num_scalar_prefetch=4,
        grid=(n_slots // r,),
        in_specs=[rows(d // 2), per_e(d, 2 * D_FF), per_e(1, 2 * D_FF), per_e(D_FF, d), per_e(1, d)],
        out_specs=rows(d),
        scratch_shapes=[pltpu.VMEM((d, 2 * D_FF), BF16), pltpu.VMEM((D_FF, d), BF16)],
    )
    return pl.pallas_call(
        _experts_kernel,
        grid_spec=grid_spec,
        out_shape=jax.ShapeDtypeStruct((n_slots, d), F32),
        compiler_params=pltpu.CompilerParams(dimension_semantics=("arbitrary",),
                                             vmem_limit_bytes=EXPERTS_VMEM_LIMIT),
        name="experts",
    )(tile_expert, tile_block, tile_first, used, xs, w["w_gate_up"], w["b_gate_up"], w["w_down"], w["b_down"])


def _combine_kernel(h2_ref, gw_ref, y0_ref, y1_ref, y2_ref, y3_ref, gfin_ref, y_ref):
    gw = gw_ref[...]
    moe = None
    for k, yk in enumerate((y0_ref, y1_ref, y2_ref, y3_ref)):
        part = gw[:, k:k + 1] * yk[0]
        moe = part if moe is None else moe + part
    y_ref[...] = _rms(h2_ref[...] + moe, gfin_ref[...])


def _combine(h2, gw, ys4, w, tm):
    n, d = h2.shape
    tok = lambda width: pl.BlockSpec((tm, width), lambda i: (i, 0))
    part = lambda k: pl.BlockSpec((1, tm, d), lambda i, k=k: (k, i, 0))
    return pl.pallas_call(
        _combine_kernel,
        grid=(n // tm,),
        in_specs=[tok(d), tok(LANES)] + [part(k) for k in range(TOP_K)] + [_full((1, d))],
        out_specs=tok(d),
        out_shape=jax.ShapeDtypeStruct((n, d), F32),
        compiler_params=_params("arbitrary"),
        name="moe_combine",
    )(h2, gw, *([ys4] * TOP_K), w["g_final"])


def _routed_moe(h2, xt, route, gw, counts, w):
    n, d = h2.shape
    r = EXPERT_ROWS
    n_tiles = (n * TOP_K) // r + N_EXPERTS
    n_slots = n_tiles * r
    cnt = counts[0, :N_EXPERTS].astype(jnp.int32)
    padded = ((cnt + r - 1) // r) * r
    ends = jnp.cumsum(padded)
    starts = ends - padded
    eid = route[:, 0:TOP_K]
    rank = route[:, TOP_K:2 * TOP_K]
    onehot = eid[:, :, None] == jnp.arange(N_EXPERTS, dtype=jnp.int32)[None, None, :]
    slot = jnp.sum(jnp.where(onehot, starts[None, None, :], 0), axis=-1) + rank
    used = (ends[-1] // r).astype(jnp.int32)
    tile_block = jnp.minimum(jnp.arange(n_tiles, dtype=jnp.int32), used - 1)
    tile_expert = jnp.minimum(
        jnp.sum((tile_block[:, None] >= (ends // r)[None, :]).astype(jnp.int32), axis=1), N_EXPERTS - 1)
    tile_first = jnp.concatenate([jnp.ones((1,), jnp.int32),
                                  (tile_expert[1:] != tile_expert[:-1]).astype(jnp.int32)])
    slot_rows = slot.reshape(n // SC_CHUNK, SC_CHUNK, TOP_K).transpose(0, 2, 1).reshape(-1, SC_CHUNK)
    xs = _sc_scatter_rows(xt, slot_rows, n_slots)
    ys = _experts(tile_expert, tile_block, tile_first, used.reshape(1), xs, w)
    ys4 = _sc_gather(ys, slot.T.reshape(-1)).reshape(TOP_K, n, d)
    return _combine(h2, gw, ys4, w, tm=512)


def _moe_kernel(xt_ref, gate_ref, h2_ref, wgu_ref, bgu_ref, wd_ref, bd_ref, gfin_ref, y_ref, acc_sc):
    e = pl.program_id(1)

    @pl.when(e == 0)
    def _():
        acc_sc[...] = jnp.zeros(acc_sc.shape, F32)

    down = _swiglu_expert(_unpack_rows(xt_ref[...]), wgu_ref[0].astype(BF16), bgu_ref[0],
                          wd_ref[0].astype(BF16), bd_ref[0])
    gate = gate_ref[...]
    lane = lax.broadcasted_iota(jnp.int32, gate.shape, 1)
    ge = jnp.sum(jnp.where(lane == e, gate, 0.0), axis=1, keepdims=True)
    acc_sc[...] += ge * down

    @pl.when(e == N_EXPERTS - 1)
    def _():
        y_ref[...] = _rms(h2_ref[...] + acc_sc[...], gfin_ref[...])


def _moe(xt, gate, h2, w, tm):
    n, d = h2.shape
    tok = lambda width: pl.BlockSpec((tm, width), lambda i, e: (i, 0))
    return pl.pallas_call(
        _moe_kernel,
        grid=(n // tm, N_EXPERTS),
        in_specs=[tok(d // 2), tok(LANES), tok(d),
                  pl.BlockSpec((1, d, 2 * D_FF), lambda i, e: (e, 0, 0)),
                  pl.BlockSpec((1, 1, 2 * D_FF), lambda i, e: (e, 0, 0)),
                  pl.BlockSpec((1, D_FF, d), lambda i, e: (e, 0, 0)),
                  pl.BlockSpec((1, 1, d), lambda i, e: (e, 0, 0)),
                  _full((1, d))],
        out_specs=tok(d),
        out_shape=jax.ShapeDtypeStruct((n, d), F32),
        scratch_shapes=[pltpu.VMEM((tm, d), F32)],
        compiler_params=_params("arbitrary", "arbitrary"),
        name="moe",
    )(xt, gate, h2, w["w_gate_up"], w["b_gate_up"], w["w_down"], w["b_down"], w["g_final"])


def _rope_tables(pos):
    inv = ROPE_THETA ** (-jnp.arange(HALF_ROPE, dtype=F32) / HALF_ROPE)
    ang = pos.astype(F32)[:, None] * inv[None, :]
    cos, sin = jnp.cos(ang), jnp.sin(ang)
    n = pos.shape[0]
    pad = LANES - QK_NOPE - QK_ROPE
    rc = jnp.concatenate([jnp.ones((n, QK_NOPE), F32), cos, cos, jnp.zeros((n, pad), F32)], axis=1)
    ra = jnp.concatenate([jnp.zeros((n, QK_NOPE), F32), -sin, jnp.zeros((n, HALF_ROPE + pad), F32)], axis=1)
    rb = jnp.concatenate([jnp.zeros((n, QK_NOPE + HALF_ROPE), F32), sin, jnp.zeros((n, pad), F32)], axis=1)
    return rc, ra, rb


def _prepare(l, s_len, n_sample, past, g_mix, w_in, w_conv, g_q, w_uq, g_kv, w_uk, w_uv, g_conv_out, g_attn_out,
             w_out, g_cross, g_mem, w_mq, w_mk, w_mv, w_mo, g_ffn, w_router, b_router, w_gate_up, b_gate_up,
             w_down, b_down, g_final):
    h = MLA_HEADS
    row = lambda g: g.reshape(1, -1).astype(F32)
    head_pad = LANES - QK_NOPE - QK_ROPE
    win = w_in[l]
    win = jnp.concatenate([win[:, :OFF_KR], jnp.zeros((D_MODEL, QK_NOPE), F32), win[:, OFF_KR:],
                           jnp.zeros((D_MODEL, head_pad), F32)], axis=1)
    wuq = jnp.pad(w_uq[l].reshape(Q_LORA, h, QK_NOPE + QK_ROPE), ((0, 0), (0, 0), (0, head_pad)))
    wuk = jnp.pad(w_uk[l], ((0, 0), (0, 0), (0, LANES - QK_NOPE)))
    wabs = jnp.pad(jnp.transpose(w_uk[l], (1, 2, 0)), ((0, 0), (0, LANES - QK_NOPE), (0, 0)))
    eye = jnp.eye(h, dtype=F32)
    wuv_heads = jnp.einsum("chd,hg->hcgd", w_uv[l], eye).reshape(h, KV_LORA, h * V_HEAD)
    wr = jnp.pad(w_router[l], ((0, 0), (0, LANES - N_EXPERTS)))
    br = jnp.pad(b_router[l], (0, LANES - N_EXPERTS)).reshape(1, LANES)
    return {
        "g_mix": row(g_mix[l]), "w_in": win.astype(BF16), "w_conv": w_conv[l].astype(F32),
        "g_q": row(g_q[l]), "w_uq": wuq.reshape(Q_LORA, h * LANES).astype(BF16),
        "g_kv": row(g_kv[l]), "w_uk": wuk.reshape(KV_LORA, h * LANES).astype(BF16),
        "w_uv": w_uv[l].reshape(KV_LORA, h * V_HEAD).T.astype(BF16), "w_abs": wabs.astype(BF16),
        "w_uv_heads": wuv_heads.astype(BF16),
        "g_conv_out": row(g_conv_out[l]), "g_attn_out": row(g_attn_out[l]),
        "w_out": w_out[l].astype(BF16), "g_cross": row(g_cross[l]), "g_mem": row(g_mem[l]),
        "w_mq": w_mq[l].astype(BF16), "w_mk": w_mk[l].astype(BF16), "w_mv": w_mv[l].astype(BF16),
        "w_mo": w_mo[l].astype(BF16), "g_ffn": row(g_ffn[l]),
        "w_router": wr.astype(BF16), "b_router": br.astype(F32),
        "w_gate_up": w_gate_up[l], "b_gate_up": b_gate_up[l].reshape(N_EXPERTS, 1, 2 * D_FF),
        "w_down": w_down[l], "b_down": b_down[l].reshape(N_EXPERTS, 1, D_MODEL),
        "g_final": row(g_final),
        "rope_prompt": _rope_tables(jnp.arange(s_len)),
        "rope_sample": _rope_tables(jnp.full((n_sample,), past)),
    }


def kernel(x_prompt, x_sample, mem_prompt, cache_conv, cache_ckv, cache_krope, cache_mem_k, cache_mem_v, page_table, g_mix, w_in, w_conv, g_q, w_uq, g_kv, w_uk, w_uv, g_conv_out, g_attn_out, w_out, g_cross, g_mem, w_mq, w_mk, w_mv, w_mo, g_ffn, w_router, b_router, w_gate_up, b_gate_up, w_down, b_down, g_final):
    bp, s_len, d = x_prompt.shape
    bd, t_len, _ = x_sample.shape
    depth = g_mix.shape[0]
    assert depth == 1 and t_len == 1, "kernel is written for one layer and one decode token per sequence"
    n_pages = page_table.shape[1]
    past = n_pages * PAGE_SIZE
    n_p = bp * s_len
    l = 0
    w = _prepare(l, s_len, bd, past, g_mix, w_in, w_conv, g_q, w_uq, g_kv, w_uk, w_uv, g_conv_out, g_attn_out,
                 w_out, g_cross, g_mem, w_mq, w_mk, w_mv, w_mo, g_ffn, w_router, b_router, w_gate_up,
                 b_gate_up, w_down, b_down, g_final)

    xs = x_sample.reshape(bd, d)
    prev0 = cache_conv[l, :, 0, :]
    prev1 = cache_conv[l, :, 1, :]
    q_s, qlat_s, lat_s, kr_s, ycn_s, u_s = _inproj_sample(xs, prev0, prev1, w)
    olat = _decode(page_table, jnp.transpose(qlat_s, (1, 0, 2)), jnp.transpose(q_s, (1, 0, 2)),
                   lat_s.reshape(bd, 1, KV_LORA), kr_s.reshape(bd, 1, QK_ROPE), cache_ckv,
                   jnp.swapaxes(cache_krope, 2, 3))
    ya_s = _uv_project(jnp.transpose(olat, (1, 0, 2)), w["w_uv_heads"])
    h1_s, qm_s = _mixout(xs, ycn_s, ya_s, w, tm=bd)
    o_s = _cross_sample(qm_s.astype(F32).reshape(bd, MEM_HEADS, MEM_HEAD_DIM), cache_mem_k[l], cache_mem_v[l], tb=4)
    h2_s, xt_s, gate_s, _, _, _ = _router(h1_s, o_s.reshape(bd, d).astype(BF16), w, tm=bd)
    y_s = _moe(xt_s, gate_s, h2_s, w, tm=bd)

    q, k, v, lat_p, kr_p, ycn_p, conv_p = _inproj_prompt(x_prompt, w, tm=512)
    ya_p = _attention(q, k, v, tq=512)
    mk_p, mv_p = _memory_kv(mem_prompt, w["g_mem"], w["w_mk"], w["w_mv"])
    xp = x_prompt.reshape(n_p, d)
    h2_p, xt_p, route_p, gw_p, cnt_p = _post_prompt(xp, ycn_p.reshape(n_p, -1), ya_p, mk_p, mv_p, w, tm=512)
    y_p = _routed_moe(h2_p, xt_p, route_p, gw_p, cnt_p, w)

    mem_shape = (1, bp, MEM_TOKENS, MEM_HEADS, MEM_HEAD_DIM)
    return (y_p.reshape(bp, s_len, d), y_s.reshape(bd, 1, d),
            conv_p[None], lat_p[None], kr_p[None], mk_p.reshape(mem_shape), mv_p.reshape(mem_shape),
            jnp.stack([prev1, u_s], axis=1)[None], lat_s.reshape(1, bd, 1, KV_LORA),
            kr_s.reshape(1, bd, 1, QK_ROPE))
```

```python
import functools

import jax
import jax.numpy as jnp
from jax import lax
from jax.experimental import pallas as pl
from jax.experimental.pallas import tpu as pltpu
from jax.experimental.pallas import tpu_sc as plsc

D_MODEL = 1024
CONV_WIDTH = 512
CONV_K = 3
MLA_HEADS = 8
QK_NOPE = 64
QK_ROPE = 32
V_HEAD = 64
Q_LORA = 384
KV_LORA = 256
ROPE_THETA = 10000.0
PAGE_SIZE = 128
MEM_TOKENS = 256
MEM_HEADS = 4
MEM_HEAD_DIM = D_MODEL // MEM_HEADS
N_EXPERTS = 32
TOP_K = 4
D_FF = D_MODEL
SWIGLU_LIMIT = 7.0
SWIGLU_ALPHA = 1.702
NORM_EPS = 1e-6

LANES = 128
HALF_ROPE = QK_ROPE // 2
IN_WIDTH = 3 * CONV_WIDTH + Q_LORA + KV_LORA + QK_ROPE
IN_WIDTH_PAD = 3 * CONV_WIDTH + Q_LORA + KV_LORA + LANES
OFF_CQ = 3 * CONV_WIDTH
OFF_CKV = OFF_CQ + Q_LORA
OFF_KR = OFF_CKV + KV_LORA
MLA_SCALE = (QK_NOPE + QK_ROPE) ** -0.5
MEM_SCALE = MEM_HEAD_DIM ** -0.5
LOG2_E = 1.4426950408889634
VMEM_LIMIT = 48 * 1024 * 1024
PAGES_PER_STEP = 64
DECODE_SLOTS = 3

BF16 = jnp.bfloat16
F32 = jnp.float32
NT_DIMS = (((1,), (1,)), ((), ()))


def _params(*sem):
    return pltpu.CompilerParams(dimension_semantics=sem, vmem_limit_bytes=VMEM_LIMIT)


def _rms(x, g):
    return x * lax.rsqrt(jnp.mean(x * x, axis=-1, keepdims=True) + NORM_EPS) * g


def _dot(a, b):
    return jnp.dot(a, b, preferred_element_type=F32)


def _dot_nt(a, b):
    return lax.dot_general(a, b, NT_DIMS, preferred_element_type=F32)


def _rope_group(x, rc, ra, rb):
    return x * rc + pltpu.roll(x, LANES - HALF_ROPE, 1) * ra + pltpu.roll(x, HALF_ROPE, 1) * rb


def _full(shape):
    return pl.BlockSpec(shape, lambda *_: (0,) * len(shape))


def _memkv_kernel(mem_ref, g_ref, wk_ref, wv_ref, k_ref, v_ref):
    m = _rms(mem_ref[0], g_ref[...]).astype(BF16)
    k_ref[0] = _dot(m, wk_ref[...])
    v_ref[0] = _dot(m, wv_ref[...])


def _memory_kv(mem, g_mem, w_mk, w_mv):
    b, n, d = mem.shape
    blk = pl.BlockSpec((1, n, d), lambda i: (i, 0, 0))
    return pl.pallas_call(
        _memkv_kernel,
        grid=(b,),
        in_specs=[blk, _full((1, d)), _full((d, d)), _full((d, d))],
        out_specs=[blk, blk],
        out_shape=[jax.ShapeDtypeStruct((b, n, d), F32)] * 2,
        compiler_params=_params("arbitrary"),
        name="memory_kv",
    )(mem, g_mem, w_mk, w_mv)


def _inproj_common(x, gmix, win, gq, wuq, gkv, rc, ra, rb):
    a = _rms(x, gmix).astype(BF16)
    z = _dot(a, win)
    b_g = z[:, 0:CONV_WIDTH]
    u = z[:, CONV_WIDTH:2 * CONV_WIDTH] * z[:, 2 * CONV_WIDTH:3 * CONV_WIDTH]
    cq = _rms(z[:, OFF_CQ:OFF_CKV], gq).astype(BF16)
    q = _dot(cq, wuq)
    q_heads = [_rope_group(q[:, h * LANES:(h + 1) * LANES], rc, ra, rb) for h in range(MLA_HEADS)]
    lat = _rms(z[:, OFF_CKV:OFF_KR], gkv)
    kr = _rope_group(z[:, OFF_KR:OFF_KR + LANES], rc, ra, rb)
    return b_g, u, q_heads, lat, kr


def _inproj_prompt_kernel(x_ref, gmix_ref, win_ref, wconv_ref, gq_ref, wuq_ref, gkv_ref, wuk_ref, wuv_ref,
                          gco_ref, rc_ref, ra_ref, rb_ref,
                          q_ref, k_ref, v_ref, lat_ref, kr_ref, ycn_ref, conv_ref, ubuf, *, tm):
    j = pl.program_id(1)
    b_g, u, q_heads, lat, kr = _inproj_common(
        x_ref[0], gmix_ref[...], win_ref[...], gq_ref[...], wuq_ref[...], gkv_ref[...],
        rc_ref[...], ra_ref[...], rb_ref[...])
    for h in range(MLA_HEADS):
        q_ref[0, h] = (q_heads[h] * (MLA_SCALE * LOG2_E)).astype(BF16)
    lat_ref[0] = lat
    kr_ref[0] = kr[:, QK_NOPE:QK_NOPE + QK_ROPE]
    lat_b = lat.astype(BF16)
    kn = _dot(lat_b, wuk_ref[...])
    v_t = _dot_nt(wuv_ref[...], lat_b)
    for h in range(MLA_HEADS):
        k_ref[0, h] = (kn[:, h * LANES:(h + 1) * LANES] + kr).astype(BF16)
        v_ref[0, h, 0] = v_t[h * V_HEAD:(h + 1) * V_HEAD, :].astype(BF16)

    @pl.when(j == 0)
    def _():
        ubuf[0:8, :] = jnp.zeros((8, CONV_WIDTH), F32)

    ubuf[8:8 + tm, :] = u
    u1 = ubuf[7:7 + tm, :]
    u2 = ubuf[6:6 + tm, :]
    wc = wconv_ref[...]
    yc = b_g * (wc[0:1, :] * u2 + wc[1:2, :] * u1 + wc[2:3, :] * u)
    ycn_ref[0] = _rms(yc, gco_ref[...]).astype(BF16)
    ubuf[0:8, :] = ubuf[tm:tm + 8, :]
    conv_ref[0] = u[tm - (CONV_K - 1):tm, :]


def _inproj_prompt(x, w, tm):
    b, s, d = x.shape
    h = MLA_HEADS
    tok = lambda n: pl.BlockSpec((1, tm, n), lambda i, j: (i, j, 0))
    head = pl.BlockSpec((1, h, tm, LANES), lambda i, j: (i, 0, j, 0))
    rope = pl.BlockSpec((tm, LANES), lambda i, j: (j, 0))
    qkv_shape = jax.ShapeDtypeStruct((b, h, s, LANES), BF16)
    return pl.pallas_call(
        functools.partial(_inproj_prompt_kernel, tm=tm),
        grid=(b, s // tm),
        in_specs=[tok(d), _full((1, d)), _full((d, IN_WIDTH_PAD)), _full((CONV_K, CONV_WIDTH)),
                  _full((1, Q_LORA)), _full((Q_LORA, h * LANES)), _full((1, KV_LORA)),
                  _full((KV_LORA, h * LANES)), _full((h * V_HEAD, KV_LORA)), _full((1, CONV_WIDTH)),
                  rope, rope, rope],
        out_specs=[head, head, pl.BlockSpec((1, h, 1, V_HEAD, tm), lambda i, j: (i, 0, j, 0, 0)),
                   tok(KV_LORA), tok(QK_ROPE), tok(CONV_WIDTH),
                   pl.BlockSpec((1, CONV_K - 1, CONV_WIDTH), lambda i, j: (i, 0, 0))],
        out_shape=[qkv_shape, qkv_shape, jax.ShapeDtypeStruct((b, h, s // tm, V_HEAD, tm), BF16),
                   jax.ShapeDtypeStruct((b, s, KV_LORA), F32),
                   jax.ShapeDtypeStruct((b, s, QK_ROPE), F32),
                   jax.ShapeDtypeStruct((b, s, CONV_WIDTH), BF16),
                   jax.ShapeDtypeStruct((b, CONV_K - 1, CONV_WIDTH), F32)],
        scratch_shapes=[pltpu.VMEM((tm + 8, CONV_WIDTH), F32)],
        compiler_params=_params("arbitrary", "arbitrary"),
        name="inproj_prompt",
    )(x, w["g_mix"], w["w_in"], w["w_conv"], w["g_q"], w["w_uq"], w["g_kv"], w["w_uk"], w["w_uv"],
      w["g_conv_out"], *w["rope_prompt"])


def _inproj_sample_kernel(x_ref, gmix_ref, win_ref, wconv_ref, gq_ref, wuq_ref, gkv_ref, wabs_ref,
                          gco_ref, rc_ref, ra_ref, rb_ref, p0_ref, p1_ref,
                          q_ref, qlat_ref, lat_ref, kr_ref, ycn_ref, u_ref):
    b_g, u, q_heads, lat, kr = _inproj_common(
        x_ref[...], gmix_ref[...], win_ref[...], gq_ref[...], wuq_ref[...], gkv_ref[...],
        rc_ref[...], ra_ref[...], rb_ref[...])
    for h in range(MLA_HEADS):
        qh = q_heads[h].astype(BF16)
        q_ref[h] = qh
        qlat_ref[h] = _dot(qh, wabs_ref[h]).astype(BF16)
    lat_ref[...] = lat
    kr_ref[...] = kr[:, QK_NOPE:QK_NOPE + QK_ROPE]
    wc = wconv_ref[...]
    yc = b_g * (wc[0:1, :] * p0_ref[...] + wc[1:2, :] * p1_ref[...] + wc[2:3, :] * u)
    ycn_ref[...] = _rms(yc, gco_ref[...]).astype(BF16)
    u_ref[...] = u


def _inproj_sample(x, prev0, prev1, w):
    n, d = x.shape
    h = MLA_HEADS
    return pl.pallas_call(
        _inproj_sample_kernel,
        grid=(1,),
        in_specs=[_full((n, d)), _full((1, d)), _full((d, IN_WIDTH_PAD)), _full((CONV_K, CONV_WIDTH)),
                  _full((1, Q_LORA)), _full((Q_LORA, h * LANES)), _full((1, KV_LORA)),
                  _full((h, LANES, KV_LORA)), _full((1, CONV_WIDTH)),
                  _full((n, LANES)), _full((n, LANES)), _full((n, LANES)),
                  _full((n, CONV_WIDTH)), _full((n, CONV_WIDTH))],
        out_specs=[_full((h, n, LANES)), _full((h, n, KV_LORA)), _full((n, KV_LORA)), _full((n, QK_ROPE)),
                   _full((n, CONV_WIDTH)), _full((n, CONV_WIDTH))],
        out_shape=[jax.ShapeDtypeStruct((h, n, LANES), BF16),
                   jax.ShapeDtypeStruct((h, n, KV_LORA), BF16),
                   jax.ShapeDtypeStruct((n, KV_LORA), F32),
                   jax.ShapeDtypeStruct((n, QK_ROPE), F32),
                   jax.ShapeDtypeStruct((n, CONV_WIDTH), BF16),
                   jax.ShapeDtypeStruct((n, CONV_WIDTH), F32)],
        compiler_params=_params("arbitrary"),
        name="inproj_sample",
    )(x, w["g_mix"], w["w_in"], w["w_conv"], w["g_q"], w["w_uq"], w["g_kv"], w["w_abs"],
      w["g_conv_out"], *w["rope_sample"], prev0, prev1)


def _softmax_step(s, v, m_sc, l_sc, acc_sc):
    m_prev = m_sc[...]
    m_next = jnp.maximum(m_prev, jnp.max(s, axis=1, keepdims=True))
    p = jnp.exp(s - m_next[:, 0:1])
    alpha = jnp.exp(m_prev - m_next)
    l_sc[...] = alpha * l_sc[...] + jnp.sum(p, axis=1, keepdims=True)
    pv = _dot(p.astype(BF16), v)
    acc_sc[...] = acc_sc[...] * alpha[:, 0:1] + pv
    m_sc[...] = m_next


def _attn_kernel(q_ref, k_ref, v_ref, o_ref, m_sc, l_sc, acc_sc, s0_sc, s1_sc, *, tq):
    qi = pl.program_id(2)
    key = lax.broadcasted_iota(jnp.int32, (tq, tq), 0)
    qry = lax.broadcasted_iota(jnp.int32, (tq, tq), 1)
    m_sc[...] = jnp.full(m_sc.shape, -jnp.inf, F32)
    l_sc[...] = jnp.zeros(l_sc.shape, F32)
    acc_sc[...] = jnp.zeros(acc_sc.shape, F32)

    def scores(j, buf):
        for hh in range(2):
            k = k_ref[0, hh, pl.ds(pl.multiple_of(j * tq, tq), tq), :]
            buf[hh] = _dot_nt(k, q_ref[0, hh])

    def consume(j, buf, masked):
        for hh in range(2):
            s = buf[hh]
            if masked:
                s = jnp.where(key <= qry, s, -jnp.inf)
            m_prev = m_sc[hh]
            m_next = jnp.maximum(m_prev, jnp.max(s, axis=0, keepdims=True))
            p = jnp.exp2(s - m_next)
            alpha = jnp.exp2(m_prev - m_next)
            l_sc[hh] = alpha * l_sc[hh] + jnp.sum(p, axis=0, keepdims=True)
            acc_sc[hh] = acc_sc[hh] * alpha + _dot(v_ref[0, hh, j], p.astype(BF16))
            m_sc[hh] = m_next

    def pair(jj, carry):
        j = 2 * jj
        scores(j + 1, s1_sc)
        consume(j, s0_sc, False)
        scores(j + 2, s0_sc)
        consume(j + 1, s1_sc, False)
        return carry

    scores(0, s0_sc)
    lax.fori_loop(0, qi // 2, pair, 0)

    @pl.when(qi % 2 == 0)
    def _():
        consume(qi, s0_sc, True)

    @pl.when(qi % 2 == 1)
    def _():
        scores(qi, s1_sc)
        consume(qi - 1, s0_sc, False)
        consume(qi, s1_sc, True)

    o_ref[0] = jnp.concatenate([acc_sc[0] / l_sc[0], acc_sc[1] / l_sc[1]], axis=0)


def _attention(q, k, v_t, tq):
    b, h, s, _ = q.shape
    qspec = pl.BlockSpec((1, 2, tq, LANES), lambda i, p, j: (i, p, j, 0))
    kspec = pl.BlockSpec((1, 2, s, LANES), lambda i, p, j: (i, p, 0, 0))
    vspec = pl.BlockSpec((1, 2, s // tq, V_HEAD, tq), lambda i, p, j: (i, p, 0, 0, 0))
    return pl.pallas_call(
        functools.partial(_attn_kernel, tq=tq),
        grid=(b, h // 2, s // tq),
        in_specs=[qspec, kspec, vspec],
        out_specs=pl.BlockSpec((1, LANES, tq), lambda i, p, j: (i, p, j)),
        out_shape=jax.ShapeDtypeStruct((b, h * V_HEAD, s), F32),
        scratch_shapes=[pltpu.VMEM((2, 1, tq), F32), pltpu.VMEM((2, 1, tq), F32), pltpu.VMEM((2, V_HEAD, tq), F32),
                        pltpu.VMEM((2, tq, tq), F32), pltpu.VMEM((2, tq, tq), F32)],
        compiler_params=_params("arbitrary", "arbitrary", "arbitrary"),
        name="mla_prompt_attention",
    )(q, k, v_t)


def _decode_kernel(pt_ref, qlat_ref, q_ref, lat_ref, krn_ref, ckv_hbm, kr_hbm, o_ref,
                   ckv_buf, kr_buf, sems, m_sc, l_sc, acc_sc, *, n_steps, n_seqs):
    npg = PAGES_PER_STEP
    ahead = DECODE_SLOTS - 1
    b = pl.program_id(0)
    step = pl.program_id(1)
    t = b * n_steps + step
    slot = t % DECODE_SLOTS

    def start_pages(seq, st, sl):
        for i in range(npg):
            page = pt_ref[seq, st * npg + i]
            pltpu.make_async_copy(ckv_hbm.at[0, page], ckv_buf.at[sl, i], sems.at[0, sl]).start(priority=i % 2)
            pltpu.make_async_copy(kr_hbm.at[0, page], kr_buf.at[sl, i], sems.at[1, sl]).start(priority=(i + 1) % 2)

    def wait_pages(sl):
        pltpu.make_async_copy(ckv_hbm.at[0, pl.ds(0, npg)], ckv_buf.at[sl], sems.at[0, sl]).wait()
        pltpu.make_async_copy(kr_hbm.at[0, pl.ds(0, npg)], kr_buf.at[sl], sems.at[1, sl]).wait()

    last = n_seqs * n_steps - 1

    @pl.when(t == 0)
    def _():
        for k in range(ahead):
            start_pages(k // n_steps, k % n_steps, k)

    @pl.when(step == 0)
    def _():
        m_sc[...] = jnp.full(m_sc.shape, -jnp.inf, F32)
        l_sc[...] = jnp.zeros(l_sc.shape, F32)
        acc_sc[...] = jnp.zeros(acc_sc.shape, F32)

    ql = qlat_ref[0]
    qr = q_ref[0][:, QK_NOPE:QK_NOPE + QK_ROPE]
    wait_pages(slot)
    nxt = jnp.minimum(t + ahead, last)
    start_pages(nxt // n_steps, nxt % n_steps, (t + ahead) % DECODE_SLOTS)
    ckv = ckv_buf[slot].reshape(npg * PAGE_SIZE, KV_LORA).astype(BF16)
    kr_t = jnp.concatenate([kr_buf[slot, i] for i in range(npg)], axis=1).astype(BF16)
    s = (_dot_nt(ql, ckv) + _dot(qr, kr_t)) * MLA_SCALE
    _softmax_step(s, ckv, m_sc, l_sc, acc_sc)

    @pl.when(step == n_steps - 1)
    def _():
        lat = lat_ref[0]
        s_new = (jnp.sum(ql.astype(F32) * lat, axis=1, keepdims=True)
                 + jnp.sum(qr.astype(F32) * krn_ref[0], axis=1, keepdims=True)) * MLA_SCALE
        m_prev = m_sc[...]
        m_next = jnp.maximum(m_prev, s_new)
        p_new = jnp.exp(s_new - m_next[:, 0:1])
        alpha = jnp.exp(m_prev - m_next)
        l_fin = alpha * l_sc[...] + p_new
        acc = acc_sc[...] * alpha[:, 0:1] + p_new * lat
        o_ref[0] = acc / l_fin[:, 0:1]

    @pl.when(t == last)
    def _():
        for k in range(1, DECODE_SLOTS):
            wait_pages((t + k) % DECODE_SLOTS)


def _decode(page_table, qlat, q, lat, krn, cache_ckv, cache_krope):
    bd, n_pages = page_table.shape
    npg = PAGES_PER_STEP
    n_steps = n_pages // npg
    assert n_steps * npg == n_pages and bd * n_steps >= DECODE_SLOTS
    h = MLA_HEADS
    per_b =lambda *shape: pl.BlockSpec((1,) + shape, lambda b, s, pt: (b,) + (0,) * len(shape))
    hbm = pl.BlockSpec(memory_space=pl.ANY)
    grid_spec = pltpu.PrefetchScalarGridSpec(
        num_scalar_prefetch=1,
        grid=(bd, n_steps),
        in_specs=[per_b(h, KV_LORA), per_b(h, LANES), per_b(1, KV_LORA), per_b(1, QK_ROPE), hbm, hbm],
        out_specs=per_b(h, KV_LORA),
        scratch_shapes=[pltpu.VMEM((DECODE_SLOTS, npg, PAGE_SIZE, KV_LORA), F32),
                        pltpu.VMEM((DECODE_SLOTS, npg, QK_ROPE, PAGE_SIZE), F32),
                        pltpu.SemaphoreType.DMA((2, DECODE_SLOTS)),
                        pltpu.VMEM((h, LANES), F32), pltpu.VMEM((h, LANES), F32), pltpu.VMEM((h, KV_LORA), F32)],
    )
    return pl.pallas_call(
        functools.partial(_decode_kernel, n_steps=n_steps, n_seqs=bd),
        grid_spec=grid_spec,
        out_shape=jax.ShapeDtypeStruct((bd, h, KV_LORA), F32),
        compiler_params=_params("arbitrary", "arbitrary"),
        name="mla_decode",
    )(page_table, qlat, q, lat, krn, cache_ckv, cache_krope)


def _uv_kernel(olat_ref, wuv_ref, ya_ref):
    acc = None
    for h in range(MLA_HEADS):
        part = _dot(olat_ref[h].astype(BF16), wuv_ref[h])
        acc = part if acc is None else acc + part
    ya_ref[...] = acc


def _uv_project(olat, wuv_heads):
    h, n, c = olat.shape
    width = MLA_HEADS * V_HEAD
    return pl.pallas_call(
        _uv_kernel,
        grid=(1,),
        in_specs=[_full((h, n, c)), _full((h, c, width))],
        out_specs=_full((n, width)),
        out_shape=jax.ShapeDtypeStruct((n, width), F32),
        compiler_params=_params("arbitrary"),
        name="decode_uv",
    )(olat, wuv_heads)


def _mixout_math(x, ycn, ya, ga, wout, gcross, wmq):
    yan = _rms(ya, ga).astype(BF16)
    mixed = jnp.concatenate([ycn, yan], axis=1)
    h1 = x + _dot(mixed, wout)
    return h1, _dot(_rms(h1, gcross).astype(BF16), wmq).astype(BF16)


def _mixout_kernel(x_ref, ycn_ref, ya_ref, ga_ref, wout_ref, gcross_ref, wmq_ref, h1_ref, qm_ref):
    h1_ref[...], qm_ref[...] = _mixout_math(x_ref[...], ycn_ref[...], ya_ref[...], ga_ref[...], wout_ref[...],
                                            gcross_ref[...], wmq_ref[...])


def _mixout(x, ycn, ya, w, tm):
    n, d = x.shape
    width = MLA_HEADS * V_HEAD
    mix = CONV_WIDTH + width
    tok = lambda width: pl.BlockSpec((tm, width), lambda i: (i, 0))
    return pl.pallas_call(
        _mixout_kernel,
        grid=(n // tm,),
        in_specs=[tok(d), tok(CONV_WIDTH), tok(width), _full((1, width)),
                  _full((mix, d)), _full((1, d)), _full((d, d))],
        out_specs=[tok(d), tok(d)],
        out_shape=[jax.ShapeDtypeStruct((n, d), F32), jax.ShapeDtypeStruct((n, d), BF16)],
        compiler_params=_params("arbitrary"),
        name="mix_out",
    )(x, ycn, ya, w["g_attn_out"], w["w_out"], w["g_cross"], w["w_mq"])


def _cross_math(q, k_ref, v_ref):
    outs = []
    for h in range(MEM_HEADS):
        sl = slice(h * MEM_HEAD_DIM, (h + 1) * MEM_HEAD_DIM)
        s = _dot_nt(q[:, sl], k_ref[0, :, sl].astype(BF16)) * MEM_SCALE
        e = jnp.exp(s - jnp.max(s, axis=1, keepdims=True))
        p = e / jnp.sum(e, axis=1, keepdims=True)
        outs.append(_dot(p.astype(BF16), v_ref[0, :, sl].astype(BF16)))
    return jnp.concatenate(outs, axis=1).astype(BF16)


def _cross_sample_kernel(q_ref, k_ref, v_ref, o_ref, *, tb):
    for t in range(tb):
        s = jnp.sum(k_ref[t] * q_ref[t][None], axis=2, keepdims=True) * MEM_SCALE
        e = jnp.exp(s - jnp.max(s, axis=0, keepdims=True))
        p = e / jnp.sum(e, axis=0, keepdims=True)
        o_ref[t] = jnp.sum(p * v_ref[t], axis=0)


def _cross_sample(qm, mk, mv, tb):
    n = qm.shape[0]
    tok = pl.BlockSpec((tb, MEM_HEADS, MEM_HEAD_DIM), lambda i: (i, 0, 0))
    mem = pl.BlockSpec((tb, MEM_TOKENS, MEM_HEADS, MEM_HEAD_DIM), lambda i: (i, 0, 0, 0))
    return pl.pallas_call(
        functools.partial(_cross_sample_kernel, tb=tb),
        grid=(n // tb,),
        in_specs=[tok, mem, mem],
        out_specs=tok,
        out_shape=jax.ShapeDtypeStruct((n, MEM_HEADS, MEM_HEAD_DIM), F32),
        compiler_params=_params("arbitrary"),
        name="cross_sample",
    )(qm, mk, mv)


def _router_math(h1, o, wmo, gffn, wr, br, cnt_prev):
    h2 = h1 + _dot(o, wmo)
    xt = _rms(h2, gffn).astype(BF16)
    logits = _dot(xt, wr) + br
    tm = logits.shape[0]
    lane = lax.broadcasted_iota(jnp.int32, logits.shape, 1)
    logits = jnp.where(lane < N_EXPERTS, logits, -jnp.inf)
    work = logits
    sel = lane < 0
    picks = []
    for k in range(TOP_K):
        m = jnp.max(work, axis=1, keepdims=True)
        idx = jnp.min(jnp.where(work == m, lane, LANES), axis=1, keepdims=True)
        hit = lane == idx
        sel = jnp.logical_or(sel, hit)
        work = jnp.where(hit, -jnp.inf, work)
        picks.append((m, idx, hit))
    top = picks[0][0]
    e = jnp.where(sel, jnp.exp(logits - top), 0.0)
    denom = jnp.sum(e, axis=1, keepdims=True)

    sel_f = sel.astype(F32)
    earlier = (lax.broadcasted_iota(jnp.int32, (tm, tm), 0) > lax.broadcasted_iota(jnp.int32, (tm, tm), 1))
    rank_all = _dot(earlier.astype(BF16), sel_f.astype(BF16)) + cnt_prev
    route = jnp.zeros(logits.shape, jnp.int32)
    gw = jnp.zeros(logits.shape, F32)
    for k, (m, idx, hit) in enumerate(picks):
        rank = jnp.sum(jnp.where(hit, rank_all, 0.0), axis=1, keepdims=True).astype(jnp.int32)
        route = jnp.where(lane == k, idx, route)
        route = jnp.where(lane == TOP_K + k, rank, route)
        gw = jnp.where(lane == k, jnp.exp(m - top) / denom, gw)
    cnt = cnt_prev + jnp.sum(sel_f, axis=0, keepdims=True)
    return h2, _pack_rows(xt.astype(F32)), e / denom, route, gw, cnt


def _router_kernel(h1_ref, o_ref, wmo_ref, gffn_ref, wr_ref, br_ref,
                   h2_ref, xt_ref, gate_ref, route_ref, gw_ref, cnt_ref, cnt_sc):
    @pl.when(pl.program_id(0) == 0)
    def _():
        cnt_sc[...] = jnp.zeros(cnt_sc.shape, F32)

    h2_ref[...], xt_ref[...], gate_ref[...], route_ref[...], gw_ref[...], cnt = _router_math(
        h1_ref[...], o_ref[...], wmo_ref[...], gffn_ref[...], wr_ref[...], br_ref[...], cnt_sc[...])
    cnt_sc[...] = cnt
    cnt_ref[...] = cnt


def _post_prompt_kernel(x_ref, ycn_ref, yat_ref, k_ref, v_ref, ga_ref, wout_ref, gcross_ref, wmq_ref,
                        wmo_ref, gffn_ref, wr_ref, br_ref, h2_ref, xt_ref, route_ref, gw_ref, cnt_ref, cnt_sc):
    @pl.when(pl.program_id(0) == 0)
    def _():
        cnt_sc[...] = jnp.zeros(cnt_sc.shape, F32)

    h1, qm = _mixout_math(x_ref[...], ycn_ref[...], yat_ref[0].T, ga_ref[...], wout_ref[...], gcross_ref[...],
                          wmq_ref[...])
    o = _cross_math(qm, k_ref, v_ref)
    h2_ref[...], xt_ref[...], _, route_ref[...], gw_ref[...], cnt = _router_math(
        h1, o, wmo_ref[...], gffn_ref[...], wr_ref[...], br_ref[...], cnt_sc[...])
    cnt_sc[...] = cnt
    cnt_ref[...] = cnt


def _post_prompt(x, ycn, ya_t, mk, mv, w, tm):
    n, d = x.shape
    width = MLA_HEADS * V_HEAD
    tiles = ya_t.shape[2] // tm
    tok = lambda width: pl.BlockSpec((tm, width), lambda i: (i, 0))
    mem = pl.BlockSpec((1, MEM_TOKENS, d), lambda i: (i // tiles, 0, 0))
    return pl.pallas_call(
        _post_prompt_kernel,
        grid=(n // tm,),
        in_specs=[tok(d), tok(CONV_WIDTH), pl.BlockSpec((1, width, tm), lambda i: (i // tiles, 0, i % tiles)),
                  mem, mem, _full((1, width)), _full((CONV_WIDTH + width, d)), _full((1, d)), _full((d, d)),
                  _full((d, d)), _full((1, d)), _full((d, LANES)), _full((1, LANES))],
        out_specs=[tok(d), tok(d // 2), tok(LANES), tok(LANES), _full((1, LANES))],
        out_shape=[jax.ShapeDtypeStruct((n, d), F32), jax.ShapeDtypeStruct((n, d // 2), jnp.uint32),
                   jax.ShapeDtypeStruct((n, LANES), jnp.int32), jax.ShapeDtypeStruct((n, LANES), F32),
                   jax.ShapeDtypeStruct((1, LANES), F32)],
        scratch_shapes=[pltpu.VMEM((1, LANES), F32)],
        compiler_params=_params("arbitrary"),
        name="post_prompt",
    )(x, ycn, ya_t, mk, mv, w["g_attn_out"], w["w_out"], w["g_cross"], w["w_mq"], w["w_mo"], w["g_ffn"],
      w["w_router"], w["b_router"])


def _router(h1, o, w, tm):
    n, d = h1.shape
    tok = lambda width: pl.BlockSpec((tm, width), lambda i: (i, 0))
    return pl.pallas_call(
        _router_kernel,
        grid=(n // tm,),
        in_specs=[tok(d), tok(d), _full((d, d)), _full((1, d)), _full((d, LANES)), _full((1, LANES))],
        out_specs=[tok(d), tok(d // 2), tok(LANES), tok(LANES), tok(LANES), _full((1, LANES))],
        out_shape=[jax.ShapeDtypeStruct((n, d), F32), jax.ShapeDtypeStruct((n, d // 2), jnp.uint32),
                   jax.ShapeDtypeStruct((n, LANES), F32), jax.ShapeDtypeStruct((n, LANES), jnp.int32),
                   jax.ShapeDtypeStruct((n, LANES), F32), jax.ShapeDtypeStruct((1, LANES), F32)],
        scratch_shapes=[pltpu.VMEM((1, LANES), F32)],
        compiler_params=_params("arbitrary"),
        name="router",
    )(h1, o, w["w_mo"], w["g_ffn"], w["w_router"], w["b_router"])


SC_CORES = 2
SC_SUBCORES = 16
SC_WORKERS = SC_CORES * SC_SUBCORES
SC_CHUNK = 32
EXPERT_ROWS = 512
EXPERTS_VMEM_LIMIT = 56 * 1024 * 1024


def _sc_gather(table, idx):
    b = idx.shape[0]
    d = table.shape[1]
    per_worker = b // SC_WORKERS
    n_chunks = per_worker // SC_CHUNK
    assert per_worker * SC_WORKERS == b and n_chunks * SC_CHUNK == per_worker
    mesh = plsc.VectorSubcoreMesh(core_axis_name="c", subcore_axis_name="s")

    assert n_chunks % 2 == 0
    row_buf = pltpu.VMEM((SC_CHUNK, d), table.dtype)

    @functools.partial(
        pl.kernel, mesh=mesh,
        out_type=jax.ShapeDtypeStruct((b, d), table.dtype),
        scratch_types=[pltpu.VMEM((per_worker,), jnp.int32), row_buf, row_buf] + [pltpu.SemaphoreType.DMA] * 4,
    )
    def gather(table_hbm, idx_hbm, out_hbm, idx_v, rows0, rows1, g0, g1, w0, w1):
        wid = lax.axis_index("s") * SC_CORES + lax.axis_index("c")
        base = pl.multiple_of(wid * per_worker, 8)
        rows, gsem, wsem = (rows0, rows1), (g0, g1), (w0, w1)
        pltpu.sync_copy(idx_hbm.at[pl.ds(base, per_worker)], idx_v)

        def fetch(c, buf):
            ids = idx_v.at[pl.ds(pl.multiple_of(c * SC_CHUNK, 8), SC_CHUNK)]
            return pltpu.make_async_copy(table_hbm.at[ids], rows[buf], gsem[buf])

        def flush(c, buf):
            dst = out_hbm.at[pl.ds(pl.multiple_of(base + c * SC_CHUNK, 8), SC_CHUNK)]
            return pltpu.make_async_copy(rows[buf], dst, wsem[buf])

        fetch(0, 0).start()

        @pl.loop(0, n_chunks, step=2)
        def _(c0):
            for buf in (0, 1):
                c = c0 + buf
                fetch(c, buf).wait()

                @pl.when(c + 1 < n_chunks)
                def _():
                    @pl.when(c >= 1)
                    def _():
                        flush(c - 1, 1 - buf).wait()

                    fetch(c + 1, 1 - buf).start()

                flush(c, buf).start()

        flush(n_chunks - 2, 0).wait()
        flush(n_chunks - 1, 1).wait()

    return gather(table, idx)


def _sc_scatter_rows(x, slot_rows, n_out):
    n, d = x.shape
    per_worker = n // SC_WORKERS
    n_chunks = per_worker // SC_CHUNK
    assert per_worker * SC_WORKERS == n and n_chunks * SC_CHUNK == per_worker and n_chunks % 2 == 0
    assert slot_rows.shape == (n // SC_CHUNK * TOP_K, SC_CHUNK)
    idx_rows = n_chunks * TOP_K
    mesh = plsc.VectorSubcoreMesh(core_axis_name="c", subcore_axis_name="s")
    row_buf = pltpu.VMEM((SC_CHUNK, d), x.dtype)

    @functools.partial(
        pl.kernel, mesh=mesh,
        out_type=jax.ShapeDtypeStruct((n_out, d), x.dtype),
        scratch_types=[pltpu.VMEM((idx_rows, SC_CHUNK), jnp.int32), row_buf, row_buf]
        + [pltpu.SemaphoreType.DMA] * 4,
    )
    def scatter(x_hbm, idx_hbm, out_hbm, idx_v, rows0, rows1, r0, r1, s0, s1):
        wid = lax.axis_index("s") * SC_CORES + lax.axis_index("c")
        base = pl.multiple_of(wid * per_worker, 8)
        rows, rsem, ssem = (rows0, rows1), (r0, r1), (s0, s1)
        pltpu.sync_copy(idx_hbm.at[pl.ds(pl.multiple_of(wid * idx_rows, 8), idx_rows)], idx_v)

        def fetch(c, buf):
            src = x_hbm.at[pl.ds(pl.multiple_of(base + c * SC_CHUNK, 8), SC_CHUNK)]
            return pltpu.make_async_copy(src, rows[buf], rsem[buf])

        def spread(c, k, buf):
            return pltpu.make_async_copy(rows[buf], out_hbm.at[idx_v.at[c * TOP_K + k]], ssem[buf])

        fetch(0, 0).start()

        @pl.loop(0, n_chunks, step=2)
        def _(c0):
            for buf in (0, 1):
                c = c0 + buf
                fetch(c, buf).wait()

                @pl.when(c + 1 < n_chunks)
                def _():
                    @pl.when(c >= 1)
                    def _():
                        for k in range(TOP_K):
                            spread(c - 1, k, 1 - buf).wait()

                    fetch(c + 1, 1 - buf).start()

                for k in range(TOP_K):
                    spread(c, k, buf).start()

        for k in range(TOP_K):
            spread(n_chunks - 2, k, 0).wait()
        for k in range(TOP_K):
            spread(n_chunks - 1, k, 1).wait()

    return scatter(x, slot_rows)


def _swiglu_expert(x, wgu, bgu, wd, bd):
    gu = _dot(x, wgu) + bgu
    g = jnp.minimum(gu[:, :D_FF], SWIGLU_LIMIT)
    u = jnp.clip(gu[:, D_FF:], -SWIGLU_LIMIT, SWIGLU_LIMIT)
    hdn = (u + 1.0) * (g * (1.0 / (1.0 + jnp.exp(-SWIGLU_ALPHA * g))))
    return _dot(hdn.astype(BF16), wd) + bd


def _pack_rows(x):
    bits = lax.bitcast_convert_type(x, jnp.uint32)
    n = x.shape[1] // 2
    return bits[:, :n] | (bits[:, n:] >> 16)


def _unpack_rows(p):
    hi = lax.bitcast_convert_type(p & jnp.uint32(0xFFFF0000), F32)
    lo = lax.bitcast_convert_type(p << 16, F32)
    return jnp.concatenate([hi, lo], axis=1).astype(BF16)


def _experts_kernel(te_ref, tb_ref, first_ref, used_ref, xs_ref, wgu_ref, bgu_ref, wd_ref, bd_ref, ys_ref,
                    wgu_sc, wd_sc):
    del te_ref, tb_ref
    t = pl.program_id(0)

    @pl.when(t < used_ref[0])
    def _():
        @pl.when(first_ref[t] == 1)
        def _():
            wgu_sc[...] = wgu_ref[0].astype(BF16)
            wd_sc[...] = wd_ref[0].astype(BF16)

        ys_ref[...] = _swiglu_expert(_unpack_rows(xs_ref[...]), wgu_sc[...], bgu_ref[0], wd_sc[...], bd_ref[0])


def _experts(tile_expert, tile_block, tile_first, used, xs, w):
    n_slots = xs.shape[0]
    d = D_MODEL
    r = EXPERT_ROWS
    rows = lambda width: pl.BlockSpec((r, width), lambda t, te, tb, tf, u: (tb[t], 0))
    per_e = lambda *shape: pl.BlockSpec((1,) + shape, lambda t, te, tb, tf, u: (te[t],) + (0,) * len(shape))
    grid_spec = pltpu.PrefetchScalarGridSpec(
        num_scalar_prefetch=4,
        grid=(n_slots // r,),
        in_specs=[rows(d // 2), per_e(d, 2 * D_FF), per_e(1, 2 * D_FF), per_e(D_FF, d), per_e(1, d)],
        out_specs=rows(d),
        scratch_shapes=[pltpu.VMEM((d, 2 * D_FF), BF16), pltpu.VMEM((D_FF, d), BF16)],
    )
    return pl.pallas_call(
        _experts_kernel,
        grid_spec=grid_spec,
        out_shape=jax.ShapeDtypeStruct((n_slots, d), F32),
        compiler_params=pltpu.CompilerParams(dimension_semantics=("arbitrary",),
                                             vmem_limit_bytes=EXPERTS_VMEM_LIMIT),
        name="experts",
    )(tile_expert, tile_block, tile_first, used, xs, w["w_gate_up"], w["b_gate_up"], w["w_down"], w["b_down"])


def _combine_kernel(h2_ref, gw_ref, y0_ref, y1_ref, y2_ref, y3_ref, gfin_ref, y_ref):
    gw = gw_ref[...]
    moe = None
    for k, yk in enumerate((y0_ref, y1_ref, y2_ref, y3_ref)):
        part = gw[:, k:k + 1] * yk[0]
        moe = part if moe is None else moe + part
    y_ref[...] = _rms(h2_ref[...] + moe, gfin_ref[...])


def _combine(h2, gw, ys4, w, tm):
    n, d = h2.shape
    tok = lambda width: pl.BlockSpec((tm, width), lambda i: (i, 0))
    part = lambda k: pl.BlockSpec((1, tm, d), lambda i, k=k: (k, i, 0))
    return pl.pallas_call(
        _combine_kernel,
        grid=(n // tm,),
        in_specs=[tok(d), tok(LANES)] + [part(k) for k in range(TOP_K)] + [_full((1, d))],
        out_specs=tok(d),
        out_shape=jax.ShapeDtypeStruct((n, d), F32),
        compiler_params=_params("arbitrary"),
        name="moe_combine",
    )(h2, gw, *([ys4] * TOP_K), w["g_final"])


def _routed_moe(h2, xt, route, gw, counts, w):
    n, d = h2.shape
    r = EXPERT_ROWS
    n_tiles = (n * TOP_K) // r + N_EXPERTS
    n_slots = n_tiles * r
    cnt = counts[0, :N_EXPERTS].astype(jnp.int32)
    padded = ((cnt + r - 1) // r) * r
    ends = jnp.cumsum(padded)
    starts = ends - padded
    eid = route[:, 0:TOP_K]
    rank = route[:, TOP_K:2 * TOP_K]
    onehot = eid[:, :, None] == jnp.arange(N_EXPERTS, dtype=jnp.int32)[None, None, :]
    slot = jnp.sum(jnp.where(onehot, starts[None, None, :], 0), axis=-1) + rank
    used = (ends[-1] // r).astype(jnp.int32)
    tile_block = jnp.minimum(jnp.arange(n_tiles, dtype=jnp.int32), used - 1)
    tile_expert = jnp.minimum(
        jnp.sum((tile_block[:, None] >= (ends // r)[None, :]).astype(jnp.int32), axis=1), N_EXPERTS - 1)
    tile_first = jnp.concatenate([jnp.ones((1,), jnp.int32),
                                  (tile_expert[1:] != tile_expert[:-1]).astype(jnp.int32)])
    slot_rows = slot.reshape(n // SC_CHUNK, SC_CHUNK, TOP_K).transpose(0, 2, 1).reshape(-1, SC_CHUNK)
    xs = _sc_scatter_rows(xt, slot_rows, n_slots)
    ys = _experts(tile_expert, tile_block, tile_first, used.reshape(1), xs, w)
    ys4 = _sc_gather(ys, slot.T.reshape(-1)).reshape(TOP_K, n, d)
    return _combine(h2, gw, ys4, w, tm=512)


def _moe_kernel(xt_ref, gate_ref, h2_ref, wgu_ref, bgu_ref, wd_ref, bd_ref, gfin_ref, y_ref, acc_sc):
    e = pl.program_id(1)

    @pl.when(e == 0)
    def _():
        acc_sc[...] = jnp.zeros(acc_sc.shape, F32)

    down = _swiglu_expert(_unpack_rows(xt_ref[...]), wgu_ref[0].astype(BF16), bgu_ref[0],
                          wd_ref[0].astype(BF16), bd_ref[0])
    gate = gate_ref[...]
    lane = lax.broadcasted_iota(jnp.int32, gate.shape, 1)
    ge = jnp.sum(jnp.where(lane == e, gate, 0.0), axis=1, keepdims=True)
    acc_sc[...] += ge * down

    @pl.when(e == N_EXPERTS - 1)
    def _():
        y_ref[...] = _rms(h2_ref[...] + acc_sc[...], gfin_ref[...])


def _moe(xt, gate, h2, w, tm):
    n, d = h2.shape
    tok = lambda width: pl.BlockSpec((tm, width), lambda i, e: (i, 0))
    return pl.pallas_call(
        _moe_kernel,
        grid=(n // tm, N_EXPERTS),
        in_specs=[tok(d // 2), tok(LANES), tok(d),
                  pl.BlockSpec((1, d, 2 * D_FF), lambda i, e: (e, 0, 0)),
                  pl.BlockSpec((1, 1, 2 * D_FF), lambda i, e: (e, 0, 0)),
                  pl.BlockSpec((1, D_FF, d), lambda i, e: (e, 0, 0)),
                  pl.BlockSpec((1, 1, d), lambda i, e: (e, 0, 0)),
                  _full((1, d))],
        out_specs=tok(d),
        out_shape=jax.ShapeDtypeStruct((n, d), F32),
        scratch_shapes=[pltpu.VMEM((tm, d), F32)],
        compiler_params=_params("arbitrary", "arbitrary"),
        name="moe",
    )(xt, gate, h2, w["w_gate_up"], w["b_gate_up"], w["w_down"], w["b_down"], w["g_final"])


def _rope_tables(pos):
    inv = ROPE_THETA ** (-jnp.arange(HALF_ROPE, dtype=F32) / HALF_ROPE)
    ang = pos.astype(F32)[:, None] * inv[None, :]
    cos, sin = jnp.cos(ang), jnp.sin(ang)
    n = pos.shape[0]
    pad = LANES - QK_NOPE - QK_ROPE
    rc = jnp.concatenate([jnp.ones((n, QK_NOPE), F32), cos, cos, jnp.zeros((n, pad), F32)], axis=1)
    ra = jnp.concatenate([jnp.zeros((n, QK_NOPE), F32), -sin, jnp.zeros((n, HALF_ROPE + pad), F32)], axis=1)
    rb = jnp.concatenate([jnp.zeros((n, QK_NOPE + HALF_ROPE), F32), sin, jnp.zeros((n, pad), F32)], axis=1)
    return rc, ra, rb


def _prepare(l, s_len, n_sample, past, g_mix, w_in, w_conv, g_q, w_uq, g_kv, w_uk, w_uv, g_conv_out, g_attn_out,
             w_out, g_cross, g_mem, w_mq, w_mk, w_mv, w_mo, g_ffn, w_router, b_router, w_gate_up, b_gate_up,
             w_down, b_down, g_final):
    h = MLA_HEADS
    row = lambda g: g.reshape(1, -1).astype(F32)
    head_pad = LANES - QK_NOPE - QK_ROPE
    win = w_in[l]
    win = jnp.concatenate([win[:, :OFF_KR], jnp.zeros((D_MODEL, QK_NOPE), F32), win[:, OFF_KR:],
                           jnp.zeros((D_MODEL, head_pad), F32)], axis=1)
    wuq = jnp.pad(w_uq[l].reshape(Q_LORA, h, QK_NOPE + QK_ROPE), ((0, 0), (0, 0), (0, head_pad)))
    wuk = jnp.pad(w_uk[l], ((0, 0), (0, 0), (0, LANES - QK_NOPE)))
    wabs = jnp.pad(jnp.transpose(w_uk[l], (1, 2, 0)), ((0, 0), (0, LANES - QK_NOPE), (0, 0)))
    eye = jnp.eye(h, dtype=F32)
    wuv_heads = jnp.einsum("chd,hg->hcgd", w_uv[l], eye).reshape(h, KV_LORA, h * V_HEAD)
    wr = jnp.pad(w_router[l], ((0, 0), (0, LANES - N_EXPERTS)))
    br = jnp.pad(b_router[l], (0, LANES - N_EXPERTS)).reshape(1, LANES)
    return {
        "g_mix": row(g_mix[l]), "w_in": win.astype(BF16), "w_conv": w_conv[l].astype(F32),
        "g_q": row(g_q[l]), "w_uq": wuq.reshape(Q_LORA, h * LANES).astype(BF16),
        "g_kv": row(g_kv[l]), "w_uk": wuk.reshape(KV_LORA, h * LANES).astype(BF16),
        "w_uv": w_uv[l].reshape(KV_LORA, h * V_HEAD).T.astype(BF16), "w_abs": wabs.astype(BF16),
        "w_uv_heads": wuv_heads.astype(BF16),
        "g_conv_out": row(g_conv_out[l]), "g_attn_out": row(g_attn_out[l]),
        "w_out": w_out[l].astype(BF16), "g_cross": row(g_cross[l]), "g_mem": row(g_mem[l]),
        "w_mq": w_mq[l].astype(BF16), "w_mk": w_mk[l].astype(BF16), "w_mv": w_mv[l].astype(BF16),
        "w_mo": w_mo[l].astype(BF16), "g_ffn": row(g_ffn[l]),
        "w_router": wr.astype(BF16), "b_router": br.astype(F32),
        "w_gate_up": w_gate_up[l], "b_gate_up": b_gate_up[l].reshape(N_EXPERTS, 1, 2 * D_FF),
        "w_down": w_down[l], "b_down": b_down[l].reshape(N_EXPERTS, 1, D_MODEL),
        "g_final": row(g_final),
        "rope_prompt": _rope_tables(jnp.arange(s_len)),
        "rope_sample": _rope_tables(jnp.full((n_sample,), past)),
    }


def kernel(x_prompt, x_sample, mem_prompt, cache_conv, cache_ckv, cache_krope, cache_mem_k, cache_mem_v, page_table, g_mix, w_in, w_conv, g_q, w_uq, g_kv, w_uk, w_uv, g_conv_out, g_attn_out, w_out, g_cross, g_mem, w_mq, w_mk, w_mv, w_mo, g_ffn, w_router, b_router, w_gate_up, b_gate_up, w_down, b_down, g_final):
    bp, s_len, d = x_prompt.shape
    bd, t_len, _ = x_sample.shape
    depth = g_mix.shape[0]
    assert depth == 1 and t_len == 1, "kernel is written for one layer and one decode token per sequence"
    n_pages = page_table.shape[1]
    past = n_pages * PAGE_SIZE
    n_p = bp * s_len
    l = 0
    w = _prepare(l, s_len, bd, past, g_mix, w_in, w_conv, g_q, w_uq, g_kv, w_uk, w_uv, g_conv_out, g_attn_out,
                 w_out, g_cross, g_mem, w_mq, w_mk, w_mv, w_mo, g_ffn, w_router, b_router, w_gate_up,
                 b_gate_up, w_down, b_down, g_final)

    xs = x_sample.reshape(bd, d)
    prev0 = cache_conv[l, :, 0, :]
    prev1 = cache_conv[l, :, 1, :]
    q_s, qlat_s, lat_s, kr_s, ycn_s, u_s = _inproj_sample(xs, prev0, prev1, w)
    olat = _decode(page_table, jnp.transpose(qlat_s, (1, 0, 2)), jnp.transpose(q_s, (1, 0, 2)),
                   lat_s.reshape(bd, 1, KV_LORA), kr_s.reshape(bd, 1, QK_ROPE), cache_ckv,
                   jnp.swapaxes(cache_krope, 2, 3))
    ya_s = _uv_project(jnp.transpose(olat, (1, 0, 2)), w["w_uv_heads"])
    h1_s, qm_s = _mixout(xs, ycn_s, ya_s, w, tm=bd)
    o_s = _cross_sample(qm_s.astype(F32).reshape(bd, MEM_HEADS, MEM_HEAD_DIM), cache_mem_k[l], cache_mem_v[l], tb=4)
    h2_s, xt_s, gate_s, _, _, _ = _router(h1_s, o_s.reshape(bd, d).astype(BF16), w, tm=bd)
    y_s = _moe(xt_s, gate_s, h2_s, w, tm=bd)

    q, k, v, lat_p, kr_p, ycn_p, conv_p = _inproj_prompt(x_prompt, w, tm=512)
    ya_p = _attention(q, k, v, tq=512)
    mk_p, mv_p = _memory_kv(mem_prompt, w["g_mem"], w["w_mk"], w["w_mv"])
    xp = x_prompt.reshape(n_p, d)
    h2_p, xt_p, route_p, gw_p, cnt_p = _post_prompt(xp, ycn_p.reshape(n_p, -1), ya_p, mk_p, mv_p, w, tm=512)
    y_p = _routed_moe(h2_p, xt_p, route_p, gw_p, cnt_p, w)

    mem_shape = (1, bp, MEM_TOKENS, MEM_HEADS, MEM_HEAD_DIM)
    return (y_p.reshape(bp, s_len, d), y_s.reshape(bd, 1, d),
            conv_p[None], lat_p[None], kr_p[None], mk_p.reshape(mem_shape), mv_p.reshape(mem_shape),
            jnp.stack([prev1, u_s], axis=1)[None], lat_s.reshape(1, bd, 1, KV_LORA),
            kr_s.reshape(1, bd, 1, QK_ROPE))
```

```python
import functools

import jax
import jax.numpy as jnp
from jax import lax
from jax.experimental import pallas as pl
from jax.experimental.pallas import tpu as pltpu
from jax.experimental.pallas import tpu_sc as plsc

D_MODEL = 1024
CONV_WIDTH = 512
CONV_K = 3
MLA_HEADS = 8
QK_NOPE = 64
QK_ROPE = 32
V_HEAD = 64
Q_LORA = 384
KV_LORA = 256
ROPE_THETA = 10000.0
PAGE_SIZE = 128
MEM_TOKENS = 256
MEM_HEADS = 4
MEM_HEAD_DIM = D_MODEL // MEM_HEADS
N_EXPERTS = 32
TOP_K = 4
D_FF = D_MODEL
SWIGLU_LIMIT = 7.0
SWIGLU_ALPHA = 1.702
NORM_EPS = 1e-6

LANES = 128
HALF_ROPE = QK_ROPE // 2
IN_WIDTH = 3 * CONV_WIDTH + Q_LORA + KV_LORA + QK_ROPE
IN_WIDTH_PAD = 3 * CONV_WIDTH + Q_LORA + KV_LORA + LANES
OFF_CQ = 3 * CONV_WIDTH
OFF_CKV = OFF_CQ + Q_LORA
OFF_KR = OFF_CKV + KV_LORA
MLA_SCALE = (QK_NOPE + QK_ROPE) ** -0.5
MEM_SCALE = MEM_HEAD_DIM ** -0.5
LOG2_E = 1.4426950408889634
VMEM_LIMIT = 48 * 1024 * 1024
PAGES_PER_STEP = 64
DECODE_SLOTS = 3

BF16 = jnp.bfloat16
F32 = jnp.float32
NT_DIMS = (((1,), (1,)), ((), ()))


def _params(*sem):
    return pltpu.CompilerParams(dimension_semantics=sem, vmem_limit_bytes=VMEM_LIMIT)


def _rms(x, g):
    return x * lax.rsqrt(jnp.mean(x * x, axis=-1, keepdims=True) + NORM_EPS) * g


def _dot(a, b):
    return jnp.dot(a, b, preferred_element_type=F32)


def _dot_nt(a, b):
    return lax.dot_general(a, b, NT_DIMS, preferred_element_type=F32)


def _rope_group(x, rc, ra, rb):
    return x * rc + pltpu.roll(x, LANES - HALF_ROPE, 1) * ra + pltpu.roll(x, HALF_ROPE, 1) * rb


def _full(shape):
    return pl.BlockSpec(shape, lambda *_: (0,) * len(shape))


def _memkv_kernel(mem_ref, g_ref, wk_ref, wv_ref, k_ref, v_ref):
    m = _rms(mem_ref[0], g_ref[...]).astype(BF16)
    k_ref[0] = _dot(m, wk_ref[...])
    v_ref[0] = _dot(m, wv_ref[...])


def _memory_kv(mem, g_mem, w_mk, w_mv):
    b, n, d = mem.shape
    blk = pl.BlockSpec((1, n, d), lambda i: (i, 0, 0))
    return pl.pallas_call(
        _memkv_kernel,
        grid=(b,),
        in_specs=[blk, _full((1, d)), _full((d, d)), _full((d, d))],
        out_specs=[blk, blk],
        out_shape=[jax.ShapeDtypeStruct((b, n, d), F32)] * 2,
        compiler_params=_params("arbitrary"),
        name="memory_kv",
    )(mem, g_mem, w_mk, w_mv)


def _inproj_common(x, gmix, win, gq, wuq, gkv, rc, ra, rb):
    a = _rms(x, gmix).astype(BF16)
    z = _dot(a, win)
    b_g = z[:, 0:CONV_WIDTH]
    u = z[:, CONV_WIDTH:2 * CONV_WIDTH] * z[:, 2 * CONV_WIDTH:3 * CONV_WIDTH]
    cq = _rms(z[:, OFF_CQ:OFF_CKV], gq).astype(BF16)
    q = _dot(cq, wuq)
    q_heads = [_rope_group(q[:, h * LANES:(h + 1) * LANES], rc, ra, rb) for h in range(MLA_HEADS)]
    lat = _rms(z[:, OFF_CKV:OFF_KR], gkv)
    kr = _rope_group(z[:, OFF_KR:OFF_KR + LANES], rc, ra, rb)
    return b_g, u, q_heads, lat, kr


def _inproj_prompt_kernel(x_ref, gmix_ref, win_ref, wconv_ref, gq_ref, wuq_ref, gkv_ref, wuk_ref, wuv_ref,
                          gco_ref, rc_ref, ra_ref, rb_ref,
                          q_ref, k_ref, v_ref, lat_ref, kr_ref, ycn_ref, conv_ref, ubuf, *, tm):
    j = pl.program_id(1)
    b_g, u, q_heads, lat, kr = _inproj_common(
        x_ref[0], gmix_ref[...], win_ref[...], gq_ref[...], wuq_ref[...], gkv_ref[...],
        rc_ref[...], ra_ref[...], rb_ref[...])
    for h in range(MLA_HEADS):
        q_ref[0, h] = (q_heads[h] * (MLA_SCALE * LOG2_E)).astype(BF16)
    lat_ref[0] = lat
    kr_ref[0] = kr[:, QK_NOPE:QK_NOPE + QK_ROPE]
    lat_b = lat.astype(BF16)
    kn = _dot(lat_b, wuk_ref[...])
    v_t = _dot_nt(wuv_ref[...], lat_b)
    for h in range(MLA_HEADS):
        k_ref[0, h] = (kn[:, h * LANES:(h + 1) * LANES] + kr).astype(BF16)
        v_ref[0, h, 0] = v_t[h * V_HEAD:(h + 1) * V_HEAD, :].astype(BF16)

    @pl.when(j == 0)
    def _():
        ubuf[0:8, :] = jnp.zeros((8, CONV_WIDTH), F32)

    ubuf[8:8 + tm, :] = u
    u1 = ubuf[7:7 + tm, :]
    u2 = ubuf[6:6 + tm, :]
    wc = wconv_ref[...]
    yc = b_g * (wc[0:1, :] * u2 + wc[1:2, :] * u1 + wc[2:3, :] * u)
    ycn_ref[0] = _rms(yc, gco_ref[...]).astype(BF16)
    ubuf[0:8, :] = ubuf[tm:tm + 8, :]
    conv_ref[0] = u[tm - (CONV_K - 1):tm, :]


def _inproj_prompt(x, w, tm):
    b, s, d = x.shape
    h = MLA_HEADS
    tok = lambda n: pl.BlockSpec((1, tm, n), lambda i, j: (i, j, 0))
    head = pl.BlockSpec((1, h, tm, LANES), lambda i, j: (i, 0, j, 0))
    rope = pl.BlockSpec((tm, LANES), lambda i, j: (j, 0))
    qkv_shape = jax.ShapeDtypeStruct((b, h, s, LANES), BF16)
    return pl.pallas_call(
        functools.partial(_inproj_prompt_kernel, tm=tm),
        grid=(b, s // tm),
        in_specs=[tok(d), _full((1, d)), _full((d, IN_WIDTH_PAD)), _full((CONV_K, CONV_WIDTH)),
                  _full((1, Q_LORA)), _full((Q_LORA, h * LANES)), _full((1, KV_LORA)),
                  _full((KV_LORA, h * LANES)), _full((h * V_HEAD, KV_LORA)), _full((1, CONV_WIDTH)),
                  rope, rope, rope],
        out_specs=[head, head, pl.BlockSpec((1, h, 1, V_HEAD, tm), lambda i, j: (i, 0, j, 0, 0)),
                   tok(KV_LORA), tok(QK_ROPE), tok(CONV_WIDTH),
                   pl.BlockSpec((1, CONV_K - 1, CONV_WIDTH), lambda i, j: (i, 0, 0))],
        out_shape=[qkv_shape, qkv_shape, jax.ShapeDtypeStruct((b, h, s // tm, V_HEAD, tm), BF16),
                   jax.ShapeDtypeStruct((b, s, KV_LORA), F32),
                   jax.ShapeDtypeStruct((b, s, QK_ROPE), F32),
                   jax.ShapeDtypeStruct((b, s, CONV_WIDTH), BF16),
                   jax.ShapeDtypeStruct((b, CONV_K - 1, CONV_WIDTH), F32)],
        scratch_shapes=[pltpu.VMEM((tm + 8, CONV_WIDTH), F32)],
        compiler_params=_params("arbitrary", "arbitrary"),
        name="inproj_prompt",
    )(x, w["g_mix"], w["w_in"], w["w_conv"], w["g_q"], w["w_uq"], w["g_kv"], w["w_uk"], w["w_uv"],
      w["g_conv_out"], *w["rope_prompt"])


def _inproj_sample_kernel(x_ref, gmix_ref, win_ref, wconv_ref, gq_ref, wuq_ref, gkv_ref, wabs_ref,
                          gco_ref, rc_ref, ra_ref, rb_ref, p0_ref, p1_ref,
                          q_ref, qlat_ref, lat_ref, kr_ref, ycn_ref, u_ref):
    b_g, u, q_heads, lat, kr = _inproj_common(
        x_ref[...], gmix_ref[...], win_ref[...], gq_ref[...], wuq_ref[...], gkv_ref[...],
        rc_ref[...], ra_ref[...], rb_ref[...])
    for h in range(MLA_HEADS):
        qh = q_heads[h].astype(BF16)
        q_ref[h] = qh
        qlat_ref[h] = _dot(qh, wabs_ref[h]).astype(BF16)
    lat_ref[...] = lat
    kr_ref[...] = kr[:, QK_NOPE:QK_NOPE + QK_ROPE]
    wc = wconv_ref[...]
    yc = b_g * (wc[0:1, :] * p0_ref[...] + wc[1:2, :] * p1_ref[...] + wc[2:3, :] * u)
    ycn_ref[...] = _rms(yc, gco_ref[...]).astype(BF16)
    u_ref[...] = u


def _inproj_sample(x, prev0, prev1, w):
    n, d = x.shape
    h = MLA_HEADS
    return pl.pallas_call(
        _inproj_sample_kernel,
        grid=(1,),
        in_specs=[_full((n, d)), _full((1, d)), _full((d, IN_WIDTH_PAD)), _full((CONV_K, CONV_WIDTH)),
                  _full((1, Q_LORA)), _full((Q_LORA, h * LANES)), _full((1, KV_LORA)),
                  _full((h, LANES, KV_LORA)), _full((1, CONV_WIDTH)),
                  _full((n, LANES)), _full((n, LANES)), _full((n, LANES)),
                  _full((n, CONV_WIDTH)), _full((n, CONV_WIDTH))],
        out_specs=[_full((h, n, LANES)), _full((h, n, KV_LORA)), _full((n, KV_LORA)), _full((n, QK_ROPE)),
                   _full((n, CONV_WIDTH)), _full((n, CONV_WIDTH))],
        out_shape=[jax.ShapeDtypeStruct((h, n, LANES), BF16),
                   jax.ShapeDtypeStruct((h, n, KV_LORA), BF16),
                   jax.ShapeDtypeStruct((n, KV_LORA), F32),
                   jax.ShapeDtypeStruct((n, QK_ROPE), F32),
                   jax.ShapeDtypeStruct((n, CONV_WIDTH), BF16),
                   jax.ShapeDtypeStruct((n, CONV_WIDTH), F32)],
        compiler_params=_params("arbitrary"),
        name="inproj_sample",
    )(x, w["g_mix"], w["w_in"], w["w_conv"], w["g_q"], w["w_uq"], w["g_kv"], w["w_abs"],
      w["g_conv_out"], *w["rope_sample"], prev0, prev1)


def _softmax_step(s, v, m_sc, l_sc, acc_sc):
    m_prev = m_sc[...]
    m_next = jnp.maximum(m_prev, jnp.max(s, axis=1, keepdims=True))
    p = jnp.exp(s - m_next[:, 0:1])
    alpha = jnp.exp(m_prev - m_next)
    l_sc[...] = alpha * l_sc[...] + jnp.sum(p, axis=1, keepdims=True)
    pv = _dot(p.astype(BF16), v)
    acc_sc[...] = acc_sc[...] * alpha[:, 0:1] + pv
    m_sc[...] = m_next


def _attn_kernel(q_ref, k_ref, v_ref, o_ref, m_sc, l_sc, acc_sc, s0_sc, s1_sc, *, tq):
    qi = pl.program_id(2)
    key = lax.broadcasted_iota(jnp.int32, (tq, tq), 0)
    qry = lax.broadcasted_iota(jnp.int32, (tq, tq), 1)
    m_sc[...] = jnp.full(m_sc.shape, -jnp.inf, F32)
    l_sc[...] = jnp.zeros(l_sc.shape, F32)
    acc_sc[...] = jnp.zeros(acc_sc.shape, F32)

    def scores(j, buf):
        for hh in range(2):
            k = k_ref[0, hh, pl.ds(pl.multiple_of(j * tq, tq), tq), :]
            buf[hh] = _dot_nt(k, q_ref[0, hh])

    def consume(j, buf, masked):
        for hh in range(2):
            s = buf[hh]
            if masked:
                s = jnp.where(key <= qry, s, -jnp.inf)
            m_prev = m_sc[hh]
            m_next = jnp.maximum(m_prev, jnp.max(s, axis=0, keepdims=True))
            p = jnp.exp2(s - m_next)
            alpha = jnp.exp2(m_prev - m_next)
            l_sc[hh] = alpha * l_sc[hh] + jnp.sum(p, axis=0, keepdims=True)
            acc_sc[hh] = acc_sc[hh] * alpha + _dot(v_ref[0, hh, j], p.astype(BF16))
            m_sc[hh] = m_next

    def pair(jj, carry):
        j = 2 * jj
        scores(j + 1, s1_sc)
        consume(j, s0_sc, False)
        scores(j + 2, s0_sc)
        consume(j + 1, s1_sc, False)
        return carry

    scores(0, s0_sc)
    lax.fori_loop(0, qi // 2, pair, 0)

    @pl.when(qi % 2 == 0)
    def _():
        consume(qi, s0_sc, True)

    @pl.when(qi % 2 == 1)
    def _():
        scores(qi, s1_sc)
        consume(qi - 1, s0_sc, False)
        consume(qi, s1_sc, True)

    o_ref[0] = jnp.concatenate([acc_sc[0] / l_sc[0], acc_sc[1] / l_sc[1]], axis=0)


def _attention(q, k, v_t, tq):
    b, h, s, _ = q.shape
    qspec = pl.BlockSpec((1, 2, tq, LANES), lambda i, p, j: (i, p, j, 0))
    kspec = pl.BlockSpec((1, 2, s, LANES), lambda i, p, j: (i, p, 0, 0))
    vspec = pl.BlockSpec((1, 2, s // tq, V_HEAD, tq), lambda i, p, j: (i, p, 0, 0, 0))
    return pl.pallas_call(
        functools.partial(_attn_kernel, tq=tq),
        grid=(b, h // 2, s // tq),
        in_specs=[qspec, kspec, vspec],
        out_specs=pl.BlockSpec((1, LANES, tq), lambda i, p, j: (i, p, j)),
        out_shape=jax.ShapeDtypeStruct((b, h * V_HEAD, s), F32),
        scratch_shapes=[pltpu.VMEM((2, 1, tq), F32), pltpu.VMEM((2, 1, tq), F32), pltpu.VMEM((2, V_HEAD, tq), F32),
                        pltpu.VMEM((2, tq, tq), F32), pltpu.VMEM((2, tq, tq), F32)],
        compiler_params=_params("arbitrary", "arbitrary", "arbitrary"),
        name="mla_prompt_attention",
    )(q, k, v_t)


def _decode_kernel(pt_ref, qlat_ref, q_ref, lat_ref, krn_ref, ckv_hbm, kr_hbm, o_ref,
                   ckv_buf, kr_buf, sems, m_sc, l_sc, acc_sc, *, n_steps, n_seqs):
    npg = PAGES_PER_STEP
    ahead = DECODE_SLOTS - 1
    b = pl.program_id(0)
    step = pl.program_id(1)
    t = b * n_steps + step
    slot = t % DECODE_SLOTS

    def start_pages(seq, st, sl):
        for i in range(npg):
            page = pt_ref[seq, st * npg + i]
            pltpu.make_async_copy(ckv_hbm.at[0, page], ckv_buf.at[sl, i], sems.at[0, sl]).start(priority=i % 2)
            pltpu.make_async_copy(kr_hbm.at[0, page], kr_buf.at[sl, i], sems.at[1, sl]).start(priority=(i + 1) % 2)

    def wait_pages(sl):
        pltpu.make_async_copy(ckv_hbm.at[0, pl.ds(0, npg)], ckv_buf.at[sl], sems.at[0, sl]).wait()
        pltpu.make_async_copy(kr_hbm.at[0, pl.ds(0, npg)], kr_buf.at[sl], sems.at[1, sl]).wait()

    last = n_seqs * n_steps - 1

    @pl.when(t == 0)
    def _():
        for k in range(ahead):
            start_pages(k // n_steps, k % n_steps, k)

    @pl.when(step == 0)
    def _():
        m_sc[...] = jnp.full(m_sc.shape, -jnp.inf, F32)
        l_sc[...] = jnp.zeros(l_sc.shape, F32)
        acc_sc[...] = jnp.zeros(acc_sc.shape, F32)

    ql = qlat_ref[0]
    qr = q_ref[0][:, QK_NOPE:QK_NOPE + QK_ROPE]
    wait_pages(slot)
    nxt = jnp.minimum(t + ahead, last)
    start_pages(nxt // n_steps, nxt % n_steps, (t + ahead) % DECODE_SLOTS)
    ckv = ckv_buf[slot].reshape(npg * PAGE_SIZE, KV_LORA).astype(BF16)
    kr_t = jnp.concatenate([kr_buf[slot, i] for i in range(npg)], axis=1).astype(BF16)
    s = (_dot_nt(ql, ckv) + _dot(qr, kr_t)) * MLA_SCALE
    _softmax_step(s, ckv, m_sc, l_sc, acc_sc)

    @pl.when(step == n_steps - 1)
    def _():
        lat = lat_ref[0]
        s_new = (jnp.sum(ql.astype(F32) * lat, axis=1, keepdims=True)
                 + jnp.sum(qr.astype(F32) * krn_ref[0], axis=1, keepdims=True)) * MLA_SCALE
        m_prev = m_sc[...]
        m_next = jnp.maximum(m_prev, s_new)
        p_new = jnp.exp(s_new - m_next[:, 0:1])
        alpha = jnp.exp(m_prev - m_next)
        l_fin = alpha * l_sc[...] + p_new
        acc = acc_sc[...] * alpha[:, 0:1] + p_new * lat
        o_ref[0] = acc / l_fin[:, 0:1]

    @pl.when(t == last)
    def _():
        for k in range(1, DECODE_SLOTS):
            wait_pages((t + k) % DECODE_SLOTS)


def _decode(page_table, qlat, q, lat, krn, cache_ckv, cache_krope):
    bd, n_pages = page_table.shape
    npg = PAGES_PER_STEP
    n_steps = n_pages // npg
    assert n_steps * npg == n_pages and bd * n_steps >= DECODE_SLOTS
    h = MLA_HEADS
    per_b =lambda *shape: pl.BlockSpec((1,) + shape, lambda b, s, pt: (b,) + (0,) * len(shape))
    hbm = pl.BlockSpec(memory_space=pl.ANY)
    grid_spec = pltpu.PrefetchScalarGridSpec(
        num_scalar_prefetch=1,
        grid=(bd, n_steps),
        in_specs=[per_b(h, KV_LORA), per_b(h, LANES), per_b(1, KV_LORA), per_b(1, QK_ROPE), hbm, hbm],
        out_specs=per_b(h, KV_LORA),
        scratch_shapes=[pltpu.VMEM((DECODE_SLOTS, npg, PAGE_SIZE, KV_LORA), F32),
                        pltpu.VMEM((DECODE_SLOTS, npg, QK_ROPE, PAGE_SIZE), F32),
                        pltpu.SemaphoreType.DMA((2, DECODE_SLOTS)),
                        pltpu.VMEM((h, LANES), F32), pltpu.VMEM((h, LANES), F32), pltpu.VMEM((h, KV_LORA), F32)],
    )
    return pl.pallas_call(
        functools.partial(_decode_kernel, n_steps=n_steps, n_seqs=bd),
        grid_spec=grid_spec,
        out_shape=jax.ShapeDtypeStruct((bd, h, KV_LORA), F32),
        compiler_params=_params("arbitrary", "arbitrary"),
        name="mla_decode",
    )(page_table, qlat, q, lat, krn, cache_ckv, cache_krope)


def _uv_kernel(olat_ref, wuv_ref, ya_ref):
    acc = None
    for h in range(MLA_HEADS):
        part = _dot(olat_ref[h].astype(BF16), wuv_ref[h])
        acc = part if acc is None else acc + part
    ya_ref[...] = acc


def _uv_project(olat, wuv_heads):
    h, n, c = olat.shape
    width = MLA_HEADS * V_HEAD
    return pl.pallas_call(
        _uv_kernel,
        grid=(1,),
        in_specs=[_full((h, n, c)), _full((h, c, width))],
        out_specs=_full((n, width)),
        out_shape=jax.ShapeDtypeStruct((n, width), F32),
        compiler_params=_params("arbitrary"),
        name="decode_uv",
    )(olat, wuv_heads)


def _mixout_math(x, ycn, ya, ga, wout, gcross, wmq):
    yan = _rms(ya, ga).astype(BF16)
    mixed = jnp.concatenate([ycn, yan], axis=1)
    h1 = x + _dot(mixed, wout)
    return h1, _dot(_rms(h1, gcross).astype(BF16), wmq).astype(BF16)


def _mixout_kernel(x_ref, ycn_ref, ya_ref, ga_ref, wout_ref, gcross_ref, wmq_ref, h1_ref, qm_ref):
    h1_ref[...], qm_ref[...] = _mixout_math(x_ref[...], ycn_ref[...], ya_ref[...], ga_ref[...], wout_ref[...],
                                            gcross_ref[...], wmq_ref[...])


def _mixout(x, ycn, ya, w, tm):
    n, d = x.shape
    width = MLA_HEADS * V_HEAD
    mix = CONV_WIDTH + width
    tok = lambda width: pl.BlockSpec((tm, width), lambda i: (i, 0))
    return pl.pallas_call(
        _mixout_kernel,
        grid=(n // tm,),
        in_specs=[tok(d), tok(CONV_WIDTH), tok(width), _full((1, width)),
                  _full((mix, d)), _full((1, d)), _full((d, d))],
        out_specs=[tok(d), tok(d)],
        out_shape=[jax.ShapeDtypeStruct((n, d), F32), jax.ShapeDtypeStruct((n, d), BF16)],
        compiler_params=_params("arbitrary"),
        name="mix_out",
    )(x, ycn, ya, w["g_attn_out"], w["w_out"], w["g_cross"], w["w_mq"])


def _cross_math(q, k_ref, v_ref):
    outs = []
    for h in range(MEM_HEADS):
        sl = slice(h * MEM_HEAD_DIM, (h + 1) * MEM_HEAD_DIM)
        s = _dot_nt(q[:, sl], k_ref[0, :, sl].astype(BF16)) * MEM_SCALE
        e = jnp.exp(s - jnp.max(s, axis=1, keepdims=True))
        p = e / jnp.sum(e, axis=1, keepdims=True)
        outs.append(_dot(p.astype(BF16), v_ref[0, :, sl].astype(BF16)))
    return jnp.concatenate(outs, axis=1).astype(BF16)


def _cross_sample_kernel(q_ref, k_ref, v_ref, o_ref, *, tb):
    for t in range(tb):
        s = jnp.sum(k_ref[t] * q_ref[t][None], axis=2, keepdims=True) * MEM_SCALE
        e = jnp.exp(s - jnp.max(s, axis=0, keepdims=True))
        p = e / jnp.sum(e, axis=0, keepdims=True)
        o_ref[t] = jnp.sum(p * v_ref[t], axis=0)


def _cross_sample(qm, mk, mv, tb):
    n = qm.shape[0]
    tok = pl.BlockSpec((tb, MEM_HEADS, MEM_HEAD_DIM), lambda i: (i, 0, 0))
    mem = pl.BlockSpec((tb, MEM_TOKENS, MEM_HEADS, MEM_HEAD_DIM), lambda i: (i, 0, 0, 0))
    return pl.pallas_call(
        functools.partial(_cross_sample_kernel, tb=tb),
        grid=(n // tb,),
        in_specs=[tok, mem, mem],
        out_specs=tok,
        out_shape=jax.ShapeDtypeStruct((n, MEM_HEADS, MEM_HEAD_DIM), F32),
        compiler_params=_params("arbitrary"),
        name="cross_sample",
    )(qm, mk, mv)


def _router_math(h1, o, wmo, gffn, wr, br, cnt_prev):
    h2 = h1 + _dot(o, wmo)
    xt = _rms(h2, gffn).astype(BF16)
    logits = _dot(xt, wr) + br
    tm = logits.shape[0]
    lane = lax.broadcasted_iota(jnp.int32, logits.shape, 1)
    logits = jnp.where(lane < N_EXPERTS, logits, -jnp.inf)
    work = logits
    sel = lane < 0
    picks = []
    for k in range(TOP_K):
        m = jnp.max(work, axis=1, keepdims=True)
        idx = jnp.min(jnp.where(work == m, lane, LANES), axis=1, keepdims=True)
        hit = lane == idx
        sel = jnp.logical_or(sel, hit)
        work = jnp.where(hit, -jnp.inf, work)
        picks.append((m, idx, hit))
    top = picks[0][0]
    e = jnp.where(sel, jnp.exp(logits - top), 0.0)
    denom = jnp.sum(e, axis=1, keepdims=True)

    sel_f = sel.astype(F32)
    earlier = (lax.broadcasted_iota(jnp.int32, (tm, tm), 0) > lax.broadcasted_iota(jnp.int32, (tm, tm), 1))
    rank_all = _dot(earlier.astype(BF16), sel_f.astype(BF16)) + cnt_prev
    route = jnp.zeros(logits.shape, jnp.int32)
    gw = jnp.zeros(logits.shape, F32)
    for k, (m, idx, hit) in enumerate(picks):
        rank = jnp.sum(jnp.where(hit, rank_all, 0.0), axis=1, keepdims=True).astype(jnp.int32)
        route = jnp.where(lane == k, idx, route)
        route = jnp.where(lane == TOP_K + k, rank, route)
        gw = jnp.where(lane == k, jnp.exp(m - top) / denom, gw)
    cnt = cnt_prev + jnp.sum(sel_f, axis=0, keepdims=True)
    return h2, _pack_rows(xt.astype(F32)), e / denom, route, gw, cnt


def _router_kernel(h1_ref, o_ref, wmo_ref, gffn_ref, wr_ref, br_ref,
                   h2_ref, xt_ref, gate_ref, route_ref, gw_ref, cnt_ref, cnt_sc):
    @pl.when(pl.program_id(0) == 0)
    def _():
        cnt_sc[...] = jnp.zeros(cnt_sc.shape, F32)

    h2_ref[...], xt_ref[...], gate_ref[...], route_ref[...], gw_ref[...], cnt = _router_math(
        h1_ref[...], o_ref[...], wmo_ref[...], gffn_ref[...], wr_ref[...], br_ref[...], cnt_sc[...])
    cnt_sc[...] = cnt
    cnt_ref[...] = cnt


def _post_prompt_kernel(x_ref, ycn_ref, yat_ref, k_ref, v_ref, ga_ref, wout_ref, gcross_ref, wmq_ref,
                        wmo_ref, gffn_ref, wr_ref, br_ref, h2_ref, xt_ref, route_ref, gw_ref, cnt_ref, cnt_sc):
    @pl.when(pl.program_id(0) == 0)
    def _():
        cnt_sc[...] = jnp.zeros(cnt_sc.shape, F32)

    h1, qm = _mixout_math(x_ref[...], ycn_ref[...], yat_ref[0].T, ga_ref[...], wout_ref[...], gcross_ref[...],
                          wmq_ref[...])
    o = _cross_math(qm, k_ref, v_ref)
    h2_ref[...], xt_ref[...], _, route_ref[...], gw_ref[...], cnt = _router_math(
        h1, o, wmo_ref[...], gffn_ref[...], wr_ref[...], br_ref[...], cnt_sc[...])
    cnt_sc[...] = cnt
    cnt_ref[...] = cnt


def _post_prompt(x, ycn, ya_t, mk, mv, w, tm):
    n, d = x.shape
    width = MLA_HEADS * V_HEAD
    tiles = ya_t.shape[2] // tm
    tok = lambda width: pl.BlockSpec((tm, width), lambda i: (i, 0))
    mem = pl.BlockSpec((1, MEM_TOKENS, d), lambda i: (i // tiles, 0, 0))
    return pl.pallas_call(
        _post_prompt_kernel,
        grid=(n // tm,),
        in_specs=[tok(d), tok(CONV_WIDTH), pl.BlockSpec((1, width, tm), lambda i: (i // tiles, 0, i % tiles)),
                  mem, mem, _full((1, width)), _full((CONV_WIDTH + width, d)), _full((1, d)), _full((d, d)),
                  _full((d, d)), _full((1, d)), _full((d, LANES)), _full((1, LANES))],
        out_specs=[tok(d), tok(d // 2), tok(LANES), tok(LANES), _full((1, LANES))],
        out_shape=[jax.ShapeDtypeStruct((n, d), F32), jax.ShapeDtypeStruct((n, d // 2), jnp.uint32),
                   jax.ShapeDtypeStruct((n, LANES), jnp.int32), jax.ShapeDtypeStruct((n, LANES), F32),
                   jax.ShapeDtypeStruct((1, LANES), F32)],
        scratch_shapes=[pltpu.VMEM((1, LANES), F32)],
        compiler_params=_params("arbitrary"),
        name="post_prompt",
    )(x, ycn, ya_t, mk, mv, w["g_attn_out"], w["w_out"], w["g_cross"], w["w_mq"], w["w_mo"], w["g_ffn"],
      w["w_router"], w["b_router"])


def _router(h1, o, w, tm):
    n, d = h1.shape
    tok = lambda width: pl.BlockSpec((tm, width), lambda i: (i, 0))
    return pl.pallas_call(
        _router_kernel,
        grid=(n // tm,),
        in_specs=[tok(d), tok(d), _full((d, d)), _full((1, d)), _full((d, LANES)), _full((1, LANES))],
        out_specs=[tok(d), tok(d // 2), tok(LANES), tok(LANES), tok(LANES), _full((1, LANES))],
        out_shape=[jax.ShapeDtypeStruct((n, d), F32), jax.ShapeDtypeStruct((n, d // 2), jnp.uint32),
                   jax.ShapeDtypeStruct((n, LANES), F32), jax.ShapeDtypeStruct((n, LANES), jnp.int32),
                   jax.ShapeDtypeStruct((n, LANES), F32), jax.ShapeDtypeStruct((1, LANES), F32)],
        scratch_shapes=[pltpu.VMEM((1, LANES), F32)],
        compiler_params=_params("arbitrary"),
        name="router",
    )(h1, o, w["w_mo"], w["g_ffn"], w["w_router"], w["b_router"])


SC_CORES = 2
SC_SUBCORES = 16
SC_WORKERS = SC_CORES * SC_SUBCORES
SC_CHUNK = 32
EXPERT_ROWS = 512
EXPERTS_VMEM_LIMIT = 56 * 1024 * 1024


def _sc_gather(table, idx):
    b = idx.shape[0]
    d = table.shape[1]
    per_worker = b // SC_WORKERS
    n_chunks = per_worker // SC_CHUNK
    assert per_worker * SC_WORKERS == b and n_chunks * SC_CHUNK == per_worker
    mesh = plsc.VectorSubcoreMesh(core_axis_name="c", subcore_axis_name="s")

    assert n_chunks % 2 == 0
    row_buf = pltpu.VMEM((SC_CHUNK, d), table.dtype)

    @functools.partial(
        pl.kernel, mesh=mesh,
        out_type=jax.ShapeDtypeStruct((b, d), table.dtype),
        scratch_types=[pltpu.VMEM((per_worker,), jnp.int32), row_buf, row_buf] + [pltpu.SemaphoreType.DMA] * 4,
    )
    def gather(table_hbm, idx_hbm, out_hbm, idx_v, rows0, rows1, g0, g1, w0, w1):
        wid = lax.axis_index("s") * SC_CORES + lax.axis_index("c")
        base = pl.multiple_of(wid * per_worker, 8)
        rows, gsem, wsem = (rows0, rows1), (g0, g1), (w0, w1)
        pltpu.sync_copy(idx_hbm.at[pl.ds(base, per_worker)], idx_v)

        def fetch(c, buf):
            ids = idx_v.at[pl.ds(pl.multiple_of(c * SC_CHUNK, 8), SC_CHUNK)]
            return pltpu.make_async_copy(table_hbm.at[ids], rows[buf], gsem[buf])

        def flush(c, buf):
            dst = out_hbm.at[pl.ds(pl.multiple_of(base + c * SC_CHUNK, 8), SC_CHUNK)]
            return pltpu.make_async_copy(rows[buf], dst, wsem[buf])

        fetch(0, 0).start()

        @pl.loop(0, n_chunks, step=2)
        def _(c0):
            for buf in (0, 1):
                c = c0 + buf
                fetch(c, buf).wait()

                @pl.when(c + 1 < n_chunks)
                def _():
                    @pl.when(c >= 1)
                    def _():
                        flush(c - 1, 1 - buf).wait()

                    fetch(c + 1, 1 - buf).start()

                flush(c, buf).start()

        flush(n_chunks - 2, 0).wait()
        flush(n_chunks - 1, 1).wait()

    return gather(table, idx)


def _sc_scatter_rows(x, slot_rows, n_out):
    n, d = x.shape
    per_worker = n // SC_WORKERS
    n_chunks = per_worker // SC_CHUNK
    assert per_worker * SC_WORKERS == n and n_chunks * SC_CHUNK == per_worker and n_chunks % 2 == 0
    assert slot_rows.shape == (n // SC_CHUNK * TOP_K, SC_CHUNK)
    idx_rows = n_chunks * TOP_K
    mesh = plsc.VectorSubcoreMesh(core_axis_name="c", subcore_axis_name="s")
    row_buf = pltpu.VMEM((SC_CHUNK, d), x.dtype)

    @functools.partial(
        pl.kernel, mesh=mesh,
        out_type=jax.ShapeDtypeStruct((n_out, d), x.dtype),
        scratch_types=[pltpu.VMEM((idx_rows, SC_CHUNK), jnp.int32), row_buf, row_buf]
        + [pltpu.SemaphoreType.DMA] * 4,
    )
    def scatter(x_hbm, idx_hbm, out_hbm, idx_v, rows0, rows1, r0, r1, s0, s1):
        wid = lax.axis_index("s") * SC_CORES + lax.axis_index("c")
        base = pl.multiple_of(wid * per_worker, 8)
        rows, rsem, ssem = (rows0, rows1), (r0, r1), (s0, s1)
        pltpu.sync_copy(idx_hbm.at[pl.ds(pl.multiple_of(wid * idx_rows, 8), idx_rows)], idx_v)

        def fetch(c, buf):
            src = x_hbm.at[pl.ds(pl.multiple_of(base + c * SC_CHUNK, 8), SC_CHUNK)]
            return pltpu.make_async_copy(src, rows[buf], rsem[buf])

        def spread(c, k, buf):
            return pltpu.make_async_copy(rows[buf], out_hbm.at[idx_v.at[c * TOP_K + k]], ssem[buf])

        fetch(0, 0).start()

        @pl.loop(0, n_chunks, step=2)
        def _(c0):
            for buf in (0, 1):
                c = c0 + buf
                fetch(c, buf).wait()

                @pl.when(c + 1 < n_chunks)
                def _():
                    @pl.when(c >= 1)
                    def _():
                        for k in range(TOP_K):
                            spread(c - 1, k, 1 - buf).wait()

                    fetch(c + 1, 1 - buf).start()

                for k in range(TOP_K):
                    spread(c, k, buf).start()

        for k in range(TOP_K):
            spread(n_chunks - 2, k, 0).wait()
        for k in range(TOP_K):
            spread(n_chunks - 1, k, 1).wait()

    return scatter(x, slot_rows)


def _swiglu_expert(x, wgu, bgu, wd, bd):
    gu = _dot(x, wgu) + bgu
    g = jnp.minimum(gu[:, :D_FF], SWIGLU_LIMIT)
    u = jnp.clip(gu[:, D_FF:], -SWIGLU_LIMIT, SWIGLU_LIMIT)
    hdn = (u + 1.0) * (g * (1.0 / (1.0 + jnp.exp(-SWIGLU_ALPHA * g))))
    return _dot(hdn.astype(BF16), wd) + bd


def _pack_rows(x):
    bits = lax.bitcast_convert_type(x, jnp.uint32)
    n = x.shape[1] // 2
    return bits[:, :n] | (bits[:, n:] >> 16)


def _unpack_rows_f32(p):
    hi = lax.bitcast_convert_type(p & jnp.uint32(0xFFFF0000), F32)
    lo = lax.bitcast_convert_type(p << 16, F32)
    return jnp.concatenate([hi, lo], axis=1)


def _unpack_rows(p):
    return _unpack_rows_f32(p).astype(BF16)


def _experts_kernel(te_ref, tb_ref, first_ref, used_ref, xs_ref, wgu_ref, bgu_ref, wd_ref, bd_ref, ys_ref,
                    wgu_sc, wd_sc):
    del te_ref, tb_ref
    t = pl.program_id(0)

    @pl.when(t < used_ref[0])
    def _():
        @pl.when(first_ref[t] == 1)
        def _():
            wgu_sc[...] = wgu_ref[0].astype(BF16)
            wd_sc[...] = wd_ref[0].astype(BF16)

        ys = _swiglu_expert(_unpack_rows(xs_ref[...]), wgu_sc[...], bgu_ref[0], wd_sc[...], bd_ref[0])
        ys_ref[...] = _pack_rows(ys.astype(BF16).astype(F32))


def _experts(tile_expert, tile_block, tile_first, used, xs, w):
    n_slots = xs.shape[0]
    d = D_MODEL
    r = EXPERT_ROWS
    rows = lambda width: pl.BlockSpec((r, width), lambda t, te, tb, tf, u: (tb[t], 0))
    per_e = lambda *shape: pl.BlockSpec((1,) + shape, lambda t, te, tb, tf, u: (te[t],) + (0,) * len(shape))
    grid_spec = pltpu.PrefetchScalarGridSpec(
        num_scalar_prefetch=4,
        grid=(n_slots // r,),
        in_specs=[rows(d // 2), per_e(d, 2 * D_FF), per_e(1, 2 * D_FF), per_e(D_FF, d), per_e(1, d)],
        out_specs=rows(d // 2),
        scratch_shapes=[pltpu.VMEM((d, 2 * D_FF), BF16), pltpu.VMEM((D_FF, d), BF16)],
    )
    return pl.pallas_call(
        _experts_kernel,
        grid_spec=grid_spec,
        out_shape=jax.ShapeDtypeStruct((n_slots, d // 2), jnp.uint32),
        compiler_params=pltpu.CompilerParams(dimension_semantics=("arbitrary",),
                                             vmem_limit_bytes=EXPERTS_VMEM_LIMIT),
        name="experts",
    )(tile_expert, tile_block, tile_first, used, xs, w["w_gate_up"], w["b_gate_up"], w["w_down"], w["b_down"])


def _combine_kernel(h2_ref, gw_ref, y0_ref, y1_ref, y2_ref, y3_ref, gfin_ref, y_ref):
    gw = gw_ref[...]
    moe = None
    for k, yk in enumerate((y0_ref, y1_ref, y2_ref, y3_ref)):
        part = gw[:, k:k + 1] * _unpack_rows_f32(yk[0])
        moe = part if moe is None else moe + part
    y_ref[...] = _rms(h2_ref[...] + moe, gfin_ref[...])


def _combine(h2, gw, ys4, w, tm):
    n, d = h2.shape
    tok = lambda width: pl.BlockSpec((tm, width), lambda i: (i, 0))
    part = lambda k: pl.BlockSpec((1, tm, d // 2), lambda i, k=k: (k, i, 0))
    return pl.pallas_call(
        _combine_kernel,
        grid=(n // tm,),
        in_specs=[tok(d), tok(LANES)] + [part(k) for k in range(TOP_K)] + [_full((1, d))],
        out_specs=tok(d),
        out_shape=jax.ShapeDtypeStruct((n, d), F32),
        compiler_params=_params("arbitrary"),
        name="moe_combine",
    )(h2, gw, *([ys4] * TOP_K), w["g_final"])


def _routed_moe(h2, xt, route, gw, counts, w):
    n, d = h2.shape
    r = EXPERT_ROWS
    n_tiles = (n * TOP_K) // r + N_EXPERTS
    n_slots = n_tiles * r
    cnt = counts[0, :N_EXPERTS].astype(jnp.int32)
    padded = ((cnt + r - 1) // r) * r
    ends = jnp.cumsum(padded)
    starts = ends - padded
    eid = route[:, 0:TOP_K]
    rank = route[:, TOP_K:2 * TOP_K]
    onehot = eid[:, :, None] == jnp.arange(N_EXPERTS, dtype=jnp.int32)[None, None, :]
    slot = jnp.sum(jnp.where(onehot, starts[None, None, :], 0), axis=-1) + rank
    used = (ends[-1] // r).astype(jnp.int32)
    tile_block = jnp.minimum(jnp.arange(n_tiles, dtype=jnp.int32), used - 1)
    tile_expert = jnp.minimum(
        jnp.sum((tile_block[:, None] >= (ends // r)[None, :]).astype(jnp.int32), axis=1), N_EXPERTS - 1)
    tile_first = jnp.concatenate([jnp.ones((1,), jnp.int32),
                                  (tile_expert[1:] != tile_expert[:-1]).astype(jnp.int32)])
    slot_rows = slot.reshape(n // SC_CHUNK, SC_CHUNK, TOP_K).transpose(0, 2, 1).reshape(-1, SC_CHUNK)
    xs = _sc_scatter_rows(xt, slot_rows, n_slots)
    ys = _experts(tile_expert, tile_block, tile_first, used.reshape(1), xs, w)
    ys4 = _sc_gather(ys, slot.T.reshape(-1)).reshape(TOP_K, n, d // 2)
    return _combine(h2, gw, ys4, w, tm=512)


def _moe_kernel(xt_ref, gate_ref, h2_ref, wgu_ref, bgu_ref, wd_ref, bd_ref, gfin_ref, y_ref, acc_sc):
    e = pl.program_id(1)

    @pl.when(e == 0)
    def _():
        acc_sc[...] = jnp.zeros(acc_sc.shape, F32)

    down = _swiglu_expert(_unpack_rows(xt_ref[...]), wgu_ref[0].astype(BF16), bgu_ref[0],
                          wd_ref[0].astype(BF16), bd_ref[0])
    gate = gate_ref[...]
    lane = lax.broadcasted_iota(jnp.int32, gate.shape, 1)
    ge = jnp.sum(jnp.where(lane == e, gate, 0.0), axis=1, keepdims=True)
    acc_sc[...] += ge * down

    @pl.when(e == N_EXPERTS - 1)
    def _():
        y_ref[...] = _rms(h2_ref[...] + acc_sc[...], gfin_ref[...])


def _moe(xt, gate, h2, w, tm):
    n, d = h2.shape
    tok = lambda width: pl.BlockSpec((tm, width), lambda i, e: (i, 0))
    return pl.pallas_call(
        _moe_kernel,
        grid=(n // tm, N_EXPERTS),
        in_specs=[tok(d // 2), tok(LANES), tok(d),
                  pl.BlockSpec((1, d, 2 * D_FF), lambda i, e: (e, 0, 0)),
                  pl.BlockSpec((1, 1, 2 * D_FF), lambda i, e: (e, 0, 0)),
                  pl.BlockSpec((1, D_FF, d), lambda i, e: (e, 0, 0)),
                  pl.BlockSpec((1, 1, d), lambda i, e: (e, 0, 0)),
                  _full((1, d))],
        out_specs=tok(d),
        out_shape=jax.ShapeDtypeStruct((n, d), F32),
        scratch_shapes=[pltpu.VMEM((tm, d), F32)],
        compiler_params=_params("arbitrary", "arbitrary"),
        name="moe",
    )(xt, gate, h2, w["w_gate_up"], w["b_gate_up"], w["w_down"], w["b_down"], w["g_final"])


def _rope_tables(pos):
    inv = ROPE_THETA ** (-jnp.arange(HALF_ROPE, dtype=F32) / HALF_ROPE)
    ang = pos.astype(F32)[:, None] * inv[None, :]
    cos, sin = jnp.cos(ang), jnp.sin(ang)
    n = pos.shape[0]
    pad = LANES - QK_NOPE - QK_ROPE
    rc = jnp.concatenate([jnp.ones((n, QK_NOPE), F32), cos, cos, jnp.zeros((n, pad), F32)], axis=1)
    ra = jnp.concatenate([jnp.zeros((n, QK_NOPE), F32), -sin, jnp.zeros((n, HALF_ROPE + pad), F32)], axis=1)
    rb = jnp.concatenate([jnp.zeros((n, QK_NOPE + HALF_ROPE), F32), sin, jnp.zeros((n, pad), F32)], axis=1)
    return rc, ra, rb


def _prepare(l, s_len, n_sample, past, g_mix, w_in, w_conv, g_q, w_uq, g_kv, w_uk, w_uv, g_conv_out, g_attn_out,
             w_out, g_cross, g_mem, w_mq, w_mk, w_mv, w_mo, g_ffn, w_router, b_router, w_gate_up, b_gate_up,
             w_down, b_down, g_final):
    h = MLA_HEADS
    row = lambda g: g.reshape(1, -1).astype(F32)
    head_pad = LANES - QK_NOPE - QK_ROPE
    win = w_in[l]
    win = jnp.concatenate([win[:, :OFF_KR], jnp.zeros((D_MODEL, QK_NOPE), F32), win[:, OFF_KR:],
                           jnp.zeros((D_MODEL, head_pad), F32)], axis=1)
    wuq = jnp.pad(w_uq[l].reshape(Q_LORA, h, QK_NOPE + QK_ROPE), ((0, 0), (0, 0), (0, head_pad)))
    wuk = jnp.pad(w_uk[l], ((0, 0), (0, 0), (0, LANES - QK_NOPE)))
    wabs = jnp.pad(jnp.transpose(w_uk[l], (1, 2, 0)), ((0, 0), (0, LANES - QK_NOPE), (0, 0)))
    eye = jnp.eye(h, dtype=F32)
    wuv_heads = jnp.einsum("chd,hg->hcgd", w_uv[l], eye).reshape(h, KV_LORA, h * V_HEAD)
    wr = jnp.pad(w_router[l], ((0, 0), (0, LANES - N_EXPERTS)))
    br = jnp.pad(b_router[l], (0, LANES - N_EXPERTS)).reshape(1, LANES)
    return {
        "g_mix": row(g_mix[l]), "w_in": win.astype(BF16), "w_conv": w_conv[l].astype(F32),
        "g_q": row(g_q[l]), "w_uq": wuq.reshape(Q_LORA, h * LANES).astype(BF16),
        "g_kv": row(g_kv[l]), "w_uk": wuk.reshape(KV_LORA, h * LANES).astype(BF16),
        "w_uv": w_uv[l].reshape(KV_LORA, h * V_HEAD).T.astype(BF16), "w_abs": wabs.astype(BF16),
        "w_uv_heads": wuv_heads.astype(BF16),
        "g_conv_out": row(g_conv_out[l]), "g_attn_out": row(g_attn_out[l]),
        "w_out": w_out[l].astype(BF16), "g_cross": row(g_cross[l]), "g_mem": row(g_mem[l]),
        "w_mq": w_mq[l].astype(BF16), "w_mk": w_mk[l].astype(BF16), "w_mv": w_mv[l].astype(BF16),
        "w_mo": w_mo[l].astype(BF16), "g_ffn": row(g_ffn[l]),
        "w_router": wr.astype(BF16), "b_router": br.astype(F32),
        "w_gate_up": w_gate_up[l], "b_gate_up": b_gate_up[l].reshape(N_EXPERTS, 1, 2 * D_FF),
        "w_down": w_down[l], "b_down": b_down[l].reshape(N_EXPERTS, 1, D_MODEL),
        "g_final": row(g_final),
        "rope_prompt": _rope_tables(jnp.arange(s_len)),
        "rope_sample": _rope_tables(jnp.full((n_sample,), past)),
    }


def kernel(x_prompt, x_sample, mem_prompt, cache_conv, cache_ckv, cache_krope, cache_mem_k, cache_mem_v, page_table, g_mix, w_in, w_conv, g_q, w_uq, g_kv, w_uk, w_uv, g_conv_out, g_attn_out, w_out, g_cross, g_mem, w_mq, w_mk, w_mv, w_mo, g_ffn, w_router, b_router, w_gate_up, b_gate_up, w_down, b_down, g_final):
    bp, s_len, d = x_prompt.shape
    bd, t_len, _ = x_sample.shape
    depth = g_mix.shape[0]
    assert depth == 1 and t_len == 1, "kernel is written for one layer and one decode token per sequence"
    n_pages = page_table.shape[1]
    past = n_pages * PAGE_SIZE
    n_p = bp * s_len
    l = 0
    w = _prepare(l, s_len, bd, past, g_mix, w_in, w_conv, g_q, w_uq, g_kv, w_uk, w_uv, g_conv_out, g_attn_out,
                 w_out, g_cross, g_mem, w_mq, w_mk, w_mv, w_mo, g_ffn, w_router, b_router, w_gate_up,
                 b_gate_up, w_down, b_down, g_final)

    xs = x_sample.reshape(bd, d)
    prev0 = cache_conv[l, :, 0, :]
    prev1 = cache_conv[l, :, 1, :]
    q_s, qlat_s, lat_s, kr_s, ycn_s, u_s = _inproj_sample(xs, prev0, prev1, w)
    olat = _decode(page_table, jnp.transpose(qlat_s, (1, 0, 2)), jnp.transpose(q_s, (1, 0, 2)),
                   lat_s.reshape(bd, 1, KV_LORA), kr_s.reshape(bd, 1, QK_ROPE), cache_ckv,
                   jnp.swapaxes(cache_krope, 2, 3))
    ya_s = _uv_project(jnp.transpose(olat, (1, 0, 2)), w["w_uv_heads"])
    h1_s, qm_s = _mixout(xs, ycn_s, ya_s, w, tm=bd)
    o_s = _cross_sample(qm_s.astype(F32).reshape(bd, MEM_HEADS, MEM_HEAD_DIM), cache_mem_k[l], cache_mem_v[l], tb=4)
    h2_s, xt_s, gate_s, _, _, _ = _router(h1_s, o_s.reshape(bd, d).astype(BF16), w, tm=bd)
    y_s = _moe(xt_s, gate_s, h2_s, w, tm=bd)

    q, k, v, lat_p, kr_p, ycn_p, conv_p = _inproj_prompt(x_prompt, w, tm=512)
    ya_p = _attention(q, k, v, tq=512)
    mk_p, mv_p = _memory_kv(mem_prompt, w["g_mem"], w["w_mk"], w["w_mv"])
    xp = x_prompt.reshape(n_p, d)
    h2_p, xt_p, route_p, gw_p, cnt_p = _post_prompt(xp, ycn_p.reshape(n_p, -1), ya_p, mk_p, mv_p, w, tm=512)
    y_p = _routed_moe(h2_p, xt_p, route_p, gw_p, cnt_p, w)

    mem_shape = (1, bp, MEM_TOKENS, MEM_HEADS, MEM_HEAD_DIM)
    return (y_p.reshape(bp, s_len, d), y_s.reshape(bd, 1, d),
            conv_p[None], lat_p[None], kr_p[None], mk_p.reshape(mem_shape), mv_p.reshape(mem_shape),
            jnp.stack([prev1, u_s], axis=1)[None], lat_s.reshape(1, bd, 1, KV_LORA),
            kr_s.reshape(1, bd, 1, QK_ROPE))
```

```python
import functools

import jax
import jax.numpy as jnp
from jax import lax
from jax.experimental import pallas as pl
from jax.experimental.pallas import tpu as pltpu
from jax.experimental.pallas import tpu_sc as plsc

D_MODEL = 1024
CONV_WIDTH = 512
CONV_K = 3
MLA_HEADS = 8
QK_NOPE = 64
QK_ROPE = 32
V_HEAD = 64
Q_LORA = 384
KV_LORA = 256
ROPE_THETA = 10000.0
PAGE_SIZE = 128
MEM_TOKENS = 256
MEM_HEADS = 4
MEM_HEAD_DIM = D_MODEL // MEM_HEADS
N_EXPERTS = 32
TOP_K = 4
D_FF = D_MODEL
SWIGLU_LIMIT = 7.0
SWIGLU_ALPHA = 1.702
NORM_EPS = 1e-6

LANES = 128
HALF_ROPE = QK_ROPE // 2
IN_WIDTH = 3 * CONV_WIDTH + Q_LORA + KV_LORA + QK_ROPE
IN_WIDTH_PAD = 3 * CONV_WIDTH + Q_LORA + KV_LORA + LANES
OFF_CQ = 3 * CONV_WIDTH
OFF_CKV = OFF_CQ + Q_LORA
OFF_KR = OFF_CKV + KV_LORA
MLA_SCALE = (QK_NOPE + QK_ROPE) ** -0.5
MEM_SCALE = MEM_HEAD_DIM ** -0.5
LOG2_E = 1.4426950408889634
VMEM_LIMIT = 48 * 1024 * 1024
PAGES_PER_STEP = 128
DECODE_SLOTS = 2
DECODE_VMEM_LIMIT = 58 * 1024 * 1024

BF16 = jnp.bfloat16
F32 = jnp.float32
NT_DIMS = (((1,), (1,)), ((), ()))


def _params(*sem):
    return pltpu.CompilerParams(dimension_semantics=sem, vmem_limit_bytes=VMEM_LIMIT)


def _rms(x, g):
    return x * lax.rsqrt(jnp.mean(x * x, axis=-1, keepdims=True) + NORM_EPS) * g


def _dot(a, b):
    return jnp.dot(a, b, preferred_element_type=F32)


def _dot_nt(a, b):
    return lax.dot_general(a, b, NT_DIMS, preferred_element_type=F32)


def _rope_group(x, rc, ra, rb):
    return x * rc + pltpu.roll(x, LANES - HALF_ROPE, 1) * ra + pltpu.roll(x, HALF_ROPE, 1) * rb


def _full(shape):
    return pl.BlockSpec(shape, lambda *_: (0,) * len(shape))


def _memkv_kernel(mem_ref, g_ref, wk_ref, wv_ref, k_ref, v_ref):
    m = _rms(mem_ref[0], g_ref[...]).astype(BF16)
    k_ref[0] = _dot(m, wk_ref[...])
    v_ref[0] = _dot(m, wv_ref[...])


def _memory_kv(mem, g_mem, w_mk, w_mv):
    b, n, d = mem.shape
    blk = pl.BlockSpec((1, n, d), lambda i: (i, 0, 0))
    return pl.pallas_call(
        _memkv_kernel,
        grid=(b,),
        in_specs=[blk, _full((1, d)), _full((d, d)), _full((d, d))],
        out_specs=[blk, blk],
        out_shape=[jax.ShapeDtypeStruct((b, n, d), F32)] * 2,
        compiler_params=_params("arbitrary"),
        name="memory_kv",
    )(mem, g_mem, w_mk, w_mv)


def _inproj_common(x, gmix, win, gq, wuq, gkv, rc, ra, rb):
    a = _rms(x, gmix).astype(BF16)
    z = _dot(a, win)
    b_g = z[:, 0:CONV_WIDTH]
    u = z[:, CONV_WIDTH:2 * CONV_WIDTH] * z[:, 2 * CONV_WIDTH:3 * CONV_WIDTH]
    cq = _rms(z[:, OFF_CQ:OFF_CKV], gq).astype(BF16)
    q = _dot(cq, wuq)
    q_heads = [_rope_group(q[:, h * LANES:(h + 1) * LANES], rc, ra, rb) for h in range(MLA_HEADS)]
    lat = _rms(z[:, OFF_CKV:OFF_KR], gkv)
    kr = _rope_group(z[:, OFF_KR:OFF_KR + LANES], rc, ra, rb)
    return b_g, u, q_heads, lat, kr


def _inproj_prompt_kernel(x_ref, gmix_ref, win_ref, wconv_ref, gq_ref, wuq_ref, gkv_ref, wuk_ref, wuv_ref,
                          gco_ref, rc_ref, ra_ref, rb_ref,
                          q_ref, k_ref, v_ref, lat_ref, kr_ref, ycn_ref, conv_ref, ubuf, *, tm):
    j = pl.program_id(1)
    b_g, u, q_heads, lat, kr = _inproj_common(
        x_ref[0], gmix_ref[...], win_ref[...], gq_ref[...], wuq_ref[...], gkv_ref[...],
        rc_ref[...], ra_ref[...], rb_ref[...])
    for h in range(MLA_HEADS):
        q_ref[0, h] = (q_heads[h] * (MLA_SCALE * LOG2_E)).astype(BF16)
    lat_ref[0] = lat
    kr_ref[0] = kr[:, QK_NOPE:QK_NOPE + QK_ROPE]
    lat_b = lat.astype(BF16)
    kn = _dot(lat_b, wuk_ref[...])
    v_t = _dot_nt(wuv_ref[...], lat_b)
    for h in range(MLA_HEADS):
        k_ref[0, h] = (kn[:, h * LANES:(h + 1) * LANES] + kr).astype(BF16)
        v_ref[0, h, 0] = v_t[h * V_HEAD:(h + 1) * V_HEAD, :].astype(BF16)

    @pl.when(j == 0)
    def _():
        ubuf[0:8, :] = jnp.zeros((8, CONV_WIDTH), F32)

    ubuf[8:8 + tm, :] = u
    u1 = ubuf[7:7 + tm, :]
    u2 = ubuf[6:6 + tm, :]
    wc = wconv_ref[...]
    yc = b_g * (wc[0:1, :] * u2 + wc[1:2, :] * u1 + wc[2:3, :] * u)
    ycn_ref[0] = _rms(yc, gco_ref[...]).astype(BF16)
    ubuf[0:8, :] = ubuf[tm:tm + 8, :]
    conv_ref[0] = u[tm - (CONV_K - 1):tm, :]


def _inproj_prompt(x, w, tm):
    b, s, d = x.shape
    h = MLA_HEADS
    tok = lambda n: pl.BlockSpec((1, tm, n), lambda i, j: (i, j, 0))
    head = pl.BlockSpec((1, h, tm, LANES), lambda i, j: (i, 0, j, 0))
    rope = pl.BlockSpec((tm, LANES), lambda i, j: (j, 0))
    qkv_shape = jax.ShapeDtypeStruct((b, h, s, LANES), BF16)
    return pl.pallas_call(
        functools.partial(_inproj_prompt_kernel, tm=tm),
        grid=(b, s // tm),
        in_specs=[tok(d), _full((1, d)), _full((d, IN_WIDTH_PAD)), _full((CONV_K, CONV_WIDTH)),
                  _full((1, Q_LORA)), _full((Q_LORA, h * LANES)), _full((1, KV_LORA)),
                  _full((KV_LORA, h * LANES)), _full((h * V_HEAD, KV_LORA)), _full((1, CONV_WIDTH)),
                  rope, rope, rope],
        out_specs=[head, head, pl.BlockSpec((1, h, 1, V_HEAD, tm), lambda i, j: (i, 0, j, 0, 0)),
                   tok(KV_LORA), tok(QK_ROPE), tok(CONV_WIDTH),
                   pl.BlockSpec((1, CONV_K - 1, CONV_WIDTH), lambda i, j: (i, 0, 0))],
        out_shape=[qkv_shape, qkv_shape, jax.ShapeDtypeStruct((b, h, s // tm, V_HEAD, tm), BF16),
                   jax.ShapeDtypeStruct((b, s, KV_LORA), F32),
                   jax.ShapeDtypeStruct((b, s, QK_ROPE), F32),
                   jax.ShapeDtypeStruct((b, s, CONV_WIDTH), BF16),
                   jax.ShapeDtypeStruct((b, CONV_K - 1, CONV_WIDTH), F32)],
        scratch_shapes=[pltpu.VMEM((tm + 8, CONV_WIDTH), F32)],
        compiler_params=_params("arbitrary", "arbitrary"),
        name="inproj_prompt",
    )(x, w["g_mix"], w["w_in"], w["w_conv"], w["g_q"], w["w_uq"], w["g_kv"], w["w_uk"], w["w_uv"],
      w["g_conv_out"], *w["rope_prompt"])


def _inproj_sample_kernel(x_ref, gmix_ref, win_ref, wconv_ref, gq_ref, wuq_ref, gkv_ref, wabs_ref,
                          gco_ref, rc_ref, ra_ref, rb_ref, p0_ref, p1_ref,
                          q_ref, qlat_ref, lat_ref, kr_ref, ycn_ref, u_ref):
    b_g, u, q_heads, lat, kr = _inproj_common(
        x_ref[...], gmix_ref[...], win_ref[...], gq_ref[...], wuq_ref[...], gkv_ref[...],
        rc_ref[...], ra_ref[...], rb_ref[...])
    for h in range(MLA_HEADS):
        qh = q_heads[h].astype(BF16)
        q_ref[h] = qh
        qlat_ref[h] = _dot(qh, wabs_ref[h]).astype(BF16)
    lat_ref[...] = lat
    kr_ref[...] = kr[:, QK_NOPE:QK_NOPE + QK_ROPE]
    wc = wconv_ref[...]
    yc = b_g * (wc[0:1, :] * p0_ref[...] + wc[1:2, :] * p1_ref[...] + wc[2:3, :] * u)
    ycn_ref[...] = _rms(yc, gco_ref[...]).astype(BF16)
    u_ref[...] = u


def _inproj_sample(x, prev0, prev1, w):
    n, d = x.shape
    h = MLA_HEADS
    return pl.pallas_call(
        _inproj_sample_kernel,
        grid=(1,),
        in_specs=[_full((n, d)), _full((1, d)), _full((d, IN_WIDTH_PAD)), _full((CONV_K, CONV_WIDTH)),
                  _full((1, Q_LORA)), _full((Q_LORA, h * LANES)), _full((1, KV_LORA)),
                  _full((h, LANES, KV_LORA)), _full((1, CONV_WIDTH)),
                  _full((n, LANES)), _full((n, LANES)), _full((n, LANES)),
                  _full((n, CONV_WIDTH)), _full((n, CONV_WIDTH))],
        out_specs=[_full((h, n, LANES)), _full((h, n, KV_LORA)), _full((n, KV_LORA)), _full((n, QK_ROPE)),
                   _full((n, CONV_WIDTH)), _full((n, CONV_WIDTH))],
        out_shape=[jax.ShapeDtypeStruct((h, n, LANES), BF16),
                   jax.ShapeDtypeStruct((h, n, KV_LORA), BF16),
                   jax.ShapeDtypeStruct((n, KV_LORA), F32),
                   jax.ShapeDtypeStruct((n, QK_ROPE), F32),
                   jax.ShapeDtypeStruct((n, CONV_WIDTH), BF16),
                   jax.ShapeDtypeStruct((n, CONV_WIDTH), F32)],
        compiler_params=_params("arbitrary"),
        name="inproj_sample",
    )(x, w["g_mix"], w["w_in"], w["w_conv"], w["g_q"], w["w_uq"], w["g_kv"], w["w_abs"],
      w["g_conv_out"], *w["rope_sample"], prev0, prev1)


def _softmax_step(s, v, m_sc, l_sc, acc_sc):
    m_prev = m_sc[...]
    m_next = jnp.maximum(m_prev, jnp.max(s, axis=1, keepdims=True))
    p = jnp.exp(s - m_next[:, 0:1])
    alpha = jnp.exp(m_prev - m_next)
    l_sc[...] = alpha * l_sc[...] + jnp.sum(p, axis=1, keepdims=True)
    pv = _dot(p.astype(BF16), v)
    acc_sc[...] = acc_sc[...] * alpha[:, 0:1] + pv
    m_sc[...] = m_next


def _attn_kernel(q_ref, k_ref, v_ref, o_ref, m_sc, l_sc, acc_sc, s0_sc, s1_sc, *, tq):
    qi = pl.program_id(2)
    key = lax.broadcasted_iota(jnp.int32, (tq, tq), 0)
    qry = lax.broadcasted_iota(jnp.int32, (tq, tq), 1)
    m_sc[...] = jnp.full(m_sc.shape, -jnp.inf, F32)
    l_sc[...] = jnp.zeros(l_sc.shape, F32)
    acc_sc[...] = jnp.zeros(acc_sc.shape, F32)

    def scores(j, buf):
        for hh in range(2):
            k = k_ref[0, hh, pl.ds(pl.multiple_of(j * tq, tq), tq), :]
            buf[hh] = _dot_nt(k, q_ref[0, hh])

    def consume(j, buf, masked):
        for hh in range(2):
            s = buf[hh]
            if masked:
                s = jnp.where(key <= qry, s, -jnp.inf)
            m_prev = m_sc[hh]
            m_next = jnp.maximum(m_prev, jnp.max(s, axis=0, keepdims=True))
            p = jnp.exp2(s - m_next)
            alpha = jnp.exp2(m_prev - m_next)
            l_sc[hh] = alpha * l_sc[hh] + jnp.sum(p, axis=0, keepdims=True)
            acc_sc[hh] = acc_sc[hh] * alpha + _dot(v_ref[0, hh, j], p.astype(BF16))
            m_sc[hh] = m_next

    def pair(jj, carry):
        j = 2 * jj
        scores(j + 1, s1_sc)
        consume(j, s0_sc, False)
        scores(j + 2, s0_sc)
        consume(j + 1, s1_sc, False)
        return carry

    scores(0, s0_sc)
    lax.fori_loop(0, qi // 2, pair, 0)

    @pl.when(qi % 2 == 0)
    def _():
        consume(qi, s0_sc, True)

    @pl.when(qi % 2 == 1)
    def _():
        scores(qi, s1_sc)
        consume(qi - 1, s0_sc, False)
        consume(qi, s1_sc, True)

    o_ref[0] = jnp.concatenate([acc_sc[0] / l_sc[0], acc_sc[1] / l_sc[1]], axis=0)


def _attention(q, k, v_t, tq):
    b, h, s, _ = q.shape
    qspec = pl.BlockSpec((1, 2, tq, LANES), lambda i, p, j: (i, p, j, 0))
    kspec = pl.BlockSpec((1, 2, s, LANES), lambda i, p, j: (i, p, 0, 0))
    vspec = pl.BlockSpec((1, 2, s // tq, V_HEAD, tq), lambda i, p, j: (i, p, 0, 0, 0))
    return pl.pallas_call(
        functools.partial(_attn_kernel, tq=tq),
        grid=(b, h // 2, s // tq),
        in_specs=[qspec, kspec, vspec],
        out_specs=pl.BlockSpec((1, LANES, tq), lambda i, p, j: (i, p, j)),
        out_shape=jax.ShapeDtypeStruct((b, h * V_HEAD, s), F32),
        scratch_shapes=[pltpu.VMEM((2, 1, tq), F32), pltpu.VMEM((2, 1, tq), F32), pltpu.VMEM((2, V_HEAD, tq), F32),
                        pltpu.VMEM((2, tq, tq), F32), pltpu.VMEM((2, tq, tq), F32)],
        compiler_params=_params("arbitrary", "arbitrary", "arbitrary"),
        name="mla_prompt_attention",
    )(q, k, v_t)


def _decode_kernel(pt_ref, qlat_ref, q_ref, lat_ref, krn_ref, ckv_hbm, kr_hbm, o_ref,
                   ckv_buf, kr_buf, sems, m_sc, l_sc, acc_sc, *, n_steps, n_seqs):
    npg = PAGES_PER_STEP
    ahead = DECODE_SLOTS - 1
    b = pl.program_id(0)
    step = pl.program_id(1)
    t = b * n_steps + step
    slot = t % DECODE_SLOTS

    def start_pages(seq, st, sl):
        for i in range(npg):
            page = pt_ref[seq, st * npg + i]
            pltpu.make_async_copy(ckv_hbm.at[0, page], ckv_buf.at[sl, i], sems.at[0, sl]).start(priority=i % 2)
            pltpu.make_async_copy(kr_hbm.at[0, page], kr_buf.at[sl, i], sems.at[1, sl]).start(priority=(i + 1) % 2)

    def wait_pages(sl):
        pltpu.make_async_copy(ckv_hbm.at[0, pl.ds(0, npg)], ckv_buf.at[sl], sems.at[0, sl]).wait()
        pltpu.make_async_copy(kr_hbm.at[0, pl.ds(0, npg)], kr_buf.at[sl], sems.at[1, sl]).wait()

    last = n_seqs * n_steps - 1

    @pl.when(t == 0)
    def _():
        for k in range(ahead):
            start_pages(k // n_steps, k % n_steps, k)

    @pl.when(step == 0)
    def _():
        m_sc[...] = jnp.full(m_sc.shape, -jnp.inf, F32)
        l_sc[...] = jnp.zeros(l_sc.shape, F32)
        acc_sc[...] = jnp.zeros(acc_sc.shape, F32)

    ql = qlat_ref[0]
    qr = q_ref[0][:, QK_NOPE:QK_NOPE + QK_ROPE]
    wait_pages(slot)
    nxt = jnp.minimum(t + ahead, last)
    start_pages(nxt // n_steps, nxt % n_steps, (t + ahead) % DECODE_SLOTS)
    ckv = ckv_buf[slot].reshape(npg * PAGE_SIZE, KV_LORA).astype(BF16)
    kr_t = jnp.concatenate([kr_buf[slot, i] for i in range(npg)], axis=1).astype(BF16)
    s = (_dot_nt(ql, ckv) + _dot(qr, kr_t)) * MLA_SCALE
    _softmax_step(s, ckv, m_sc, l_sc, acc_sc)

    @pl.when(step == n_steps - 1)
    def _():
        lat = lat_ref[0]
        s_new = (jnp.sum(ql.astype(F32) * lat, axis=1, keepdims=True)
                 + jnp.sum(qr.astype(F32) * krn_ref[0], axis=1, keepdims=True)) * MLA_SCALE
        m_prev = m_sc[...]
        m_next = jnp.maximum(m_prev, s_new)
        p_new = jnp.exp(s_new - m_next[:, 0:1])
        alpha = jnp.exp(m_prev - m_next)
        l_fin = alpha * l_sc[...] + p_new
        acc = acc_sc[...] * alpha[:, 0:1] + p_new * lat
        o_ref[0] = acc / l_fin[:, 0:1]

    @pl.when(t == last)
    def _():
        for k in range(1, DECODE_SLOTS):
            wait_pages((t + k) % DECODE_SLOTS)


def _decode(page_table, qlat, q, lat, krn, cache_ckv, cache_krope):
    bd, n_pages = page_table.shape
    npg = PAGES_PER_STEP
    n_steps = n_pages // npg
    assert n_steps * npg == n_pages and bd * n_steps >= DECODE_SLOTS
    h = MLA_HEADS
    per_b =lambda *shape: pl.BlockSpec((1,) + shape, lambda b, s, pt: (b,) + (0,) * len(shape))
    hbm = pl.BlockSpec(memory_space=pl.ANY)
    grid_spec = pltpu.PrefetchScalarGridSpec(
        num_scalar_prefetch=1,
        grid=(bd, n_steps),
        in_specs=[per_b(h, KV_LORA), per_b(h, LANES), per_b(1, KV_LORA), per_b(1, QK_ROPE), hbm, hbm],
        out_specs=per_b(h, KV_LORA),
        scratch_shapes=[pltpu.VMEM((DECODE_SLOTS, npg, PAGE_SIZE, KV_LORA), F32),
                        pltpu.VMEM((DECODE_SLOTS, npg, QK_ROPE, PAGE_SIZE), F32),
                        pltpu.SemaphoreType.DMA((2, DECODE_SLOTS)),
                        pltpu.VMEM((h, LANES), F32), pltpu.VMEM((h, LANES), F32), pltpu.VMEM((h, KV_LORA), F32)],
    )
    return pl.pallas_call(
        functools.partial(_decode_kernel, n_steps=n_steps, n_seqs=bd),
        grid_spec=grid_spec,
        out_shape=jax.ShapeDtypeStruct((bd, h, KV_LORA), F32),
        compiler_params=pltpu.CompilerParams(dimension_semantics=("arbitrary", "arbitrary"),
                                             vmem_limit_bytes=DECODE_VMEM_LIMIT),
        name="mla_decode",
    )(page_table, qlat, q, lat, krn, cache_ckv, cache_krope)


def _uv_kernel(olat_ref, wuv_ref, ya_ref):
    acc = None
    for h in range(MLA_HEADS):
        part = _dot(olat_ref[h].astype(BF16), wuv_ref[h])
        acc = part if acc is None else acc + part
    ya_ref[...] = acc


def _uv_project(olat, wuv_heads):
    h, n, c = olat.shape
    width = MLA_HEADS * V_HEAD
    return pl.pallas_call(
        _uv_kernel,
        grid=(1,),
        in_specs=[_full((h, n, c)), _full((h, c, width))],
        out_specs=_full((n, width)),
        out_shape=jax.ShapeDtypeStruct((n, width), F32),
        compiler_params=_params("arbitrary"),
        name="decode_uv",
    )(olat, wuv_heads)


def _mixout_math(x, ycn, ya, ga, wout, gcross, wmq):
    yan = _rms(ya, ga).astype(BF16)
    mixed = jnp.concatenate([ycn, yan], axis=1)
    h1 = x + _dot(mixed, wout)
    return h1, _dot(_rms(h1, gcross).astype(BF16), wmq).astype(BF16)


def _mixout_kernel(x_ref, ycn_ref, ya_ref, ga_ref, wout_ref, gcross_ref, wmq_ref, h1_ref, qm_ref):
    h1_ref[...], qm_ref[...] = _mixout_math(x_ref[...], ycn_ref[...], ya_ref[...], ga_ref[...], wout_ref[...],
                                            gcross_ref[...], wmq_ref[...])


def _mixout(x, ycn, ya, w, tm):
    n, d = x.shape
    width = MLA_HEADS * V_HEAD
    mix = CONV_WIDTH + width
    tok = lambda width: pl.BlockSpec((tm, width), lambda i: (i, 0))
    return pl.pallas_call(
        _mixout_kernel,
        grid=(n // tm,),
        in_specs=[tok(d), tok(CONV_WIDTH), tok(width), _full((1, width)),
                  _full((mix, d)), _full((1, d)), _full((d, d))],
        out_specs=[tok(d), tok(d)],
        out_shape=[jax.ShapeDtypeStruct((n, d), F32), jax.ShapeDtypeStruct((n, d), BF16)],
        compiler_params=_params("arbitrary"),
        name="mix_out",
    )(x, ycn, ya, w["g_attn_out"], w["w_out"], w["g_cross"], w["w_mq"])


def _cross_math(q, k_ref, v_ref):
    outs = []
    for h in range(MEM_HEADS):
        sl = slice(h * MEM_HEAD_DIM, (h + 1) * MEM_HEAD_DIM)
        s = _dot_nt(q[:, sl], k_ref[0, :, sl].astype(BF16)) * MEM_SCALE
        e = jnp.exp(s - jnp.max(s, axis=1, keepdims=True))
        p = e / jnp.sum(e, axis=1, keepdims=True)
        outs.append(_dot(p.astype(BF16), v_ref[0, :, sl].astype(BF16)))
    return jnp.concatenate(outs, axis=1).astype(BF16)


def _cross_sample_kernel(q_ref, k_ref, v_ref, o_ref, *, tb):
    for t in range(tb):
        s = jnp.sum(k_ref[t] * q_ref[t][None], axis=2, keepdims=True) * MEM_SCALE
        e = jnp.exp(s - jnp.max(s, axis=0, keepdims=True))
        p = e / jnp.sum(e, axis=0, keepdims=True)
        o_ref[t] = jnp.sum(p * v_ref[t], axis=0)


def _cross_sample(qm, mk, mv, tb):
    n = qm.shape[0]
    tok = pl.BlockSpec((tb, MEM_HEADS, MEM_HEAD_DIM), lambda i: (i, 0, 0))
    mem = pl.BlockSpec((tb, MEM_TOKENS, MEM_HEADS, MEM_HEAD_DIM), lambda i: (i, 0, 0, 0))
    return pl.pallas_call(
        functools.partial(_cross_sample_kernel, tb=tb),
        grid=(n // tb,),
        in_specs=[tok, mem, mem],
        out_specs=tok,
        out_shape=jax.ShapeDtypeStruct((n, MEM_HEADS, MEM_HEAD_DIM), F32),
        compiler_params=_params("arbitrary"),
        name="cross_sample",
    )(qm, mk, mv)


def _router_math(h1, o, wmo, gffn, wr, br, cnt_prev):
    h2 = h1 + _dot(o, wmo)
    xt = _rms(h2, gffn).astype(BF16)
    logits = _dot(xt, wr) + br
    tm = logits.shape[0]
    lane = lax.broadcasted_iota(jnp.int32, logits.shape, 1)
    logits = jnp.where(lane < N_EXPERTS, logits, -jnp.inf)
    work = logits
    sel = lane < 0
    picks = []
    for k in range(TOP_K):
        m = jnp.max(work, axis=1, keepdims=True)
        idx = jnp.min(jnp.where(work == m, lane, LANES), axis=1, keepdims=True)
        hit = lane == idx
        sel = jnp.logical_or(sel, hit)
        work = jnp.where(hit, -jnp.inf, work)
        picks.append((m, idx, hit))
    top = picks[0][0]
    e = jnp.where(sel, jnp.exp(logits - top), 0.0)
    denom = jnp.sum(e, axis=1, keepdims=True)

    sel_f = sel.astype(F32)
    earlier = (lax.broadcasted_iota(jnp.int32, (tm, tm), 0) > lax.broadcasted_iota(jnp.int32, (tm, tm), 1))
    rank_all = _dot(earlier.astype(BF16), sel_f.astype(BF16)) + cnt_prev
    route = jnp.zeros(logits.shape, jnp.int32)
    gw = jnp.zeros(logits.shape, F32)
    for k, (m, idx, hit) in enumerate(picks):
        rank = jnp.sum(jnp.where(hit, rank_all, 0.0), axis=1, keepdims=True).astype(jnp.int32)
        route = jnp.where(lane == k, idx, route)
        route = jnp.where(lane == TOP_K + k, rank, route)
        gw = jnp.where(lane == k, jnp.exp(m - top) / denom, gw)
    cnt = cnt_prev + jnp.sum(sel_f, axis=0, keepdims=True)
    return h2, _pack_rows(xt.astype(F32)), e / denom, route, gw, cnt


def _router_kernel(h1_ref, o_ref, wmo_ref, gffn_ref, wr_ref, br_ref,
                   h2_ref, xt_ref, gate_ref, route_ref, gw_ref, cnt_ref, cnt_sc):
    @pl.when(pl.program_id(0) == 0)
    def _():
        cnt_sc[...] = jnp.zeros(cnt_sc.shape, F32)

    h2_ref[...], xt_ref[...], gate_ref[...], route_ref[...], gw_ref[...], cnt = _router_math(
        h1_ref[...], o_ref[...], wmo_ref[...], gffn_ref[...], wr_ref[...], br_ref[...], cnt_sc[...])
    cnt_sc[...] = cnt
    cnt_ref[...] = cnt


def _post_prompt_kernel(x_ref, ycn_ref, yat_ref, k_ref, v_ref, ga_ref, wout_ref, gcross_ref, wmq_ref,
                        wmo_ref, gffn_ref, wr_ref, br_ref, h2_ref, xt_ref, route_ref, gw_ref, cnt_ref, cnt_sc):
    @pl.when(pl.program_id(0) == 0)
    def _():
        cnt_sc[...] = jnp.zeros(cnt_sc.shape, F32)

    h1, qm = _mixout_math(x_ref[...], ycn_ref[...], yat_ref[0].T, ga_ref[...], wout_ref[...], gcross_ref[...],
                          wmq_ref[...])
    o = _cross_math(qm, k_ref, v_ref)
    h2_ref[...], xt_ref[...], _, route_ref[...], gw_ref[...], cnt = _router_math(
        h1, o, wmo_ref[...], gffn_ref[...], wr_ref[...], br_ref[...], cnt_sc[...])
    cnt_sc[...] = cnt
    cnt_ref[...] = cnt


def _post_prompt(x, ycn, ya_t, mk, mv, w, tm):
    n, d = x.shape
    width = MLA_HEADS * V_HEAD
    tiles = ya_t.shape[2] // tm
    tok = lambda width: pl.BlockSpec((tm, width), lambda i: (i, 0))
    mem = pl.BlockSpec((1, MEM_TOKENS, d), lambda i: (i // tiles, 0, 0))
    return pl.pallas_call(
        _post_prompt_kernel,
        grid=(n // tm,),
        in_specs=[tok(d), tok(CONV_WIDTH), pl.BlockSpec((1, width, tm), lambda i: (i // tiles, 0, i % tiles)),
                  mem, mem, _full((1, width)), _full((CONV_WIDTH + width, d)), _full((1, d)), _full((d, d)),
                  _full((d, d)), _full((1, d)), _full((d, LANES)), _full((1, LANES))],
        out_specs=[tok(d), tok(d // 2), tok(LANES), tok(LANES), _full((1, LANES))],
        out_shape=[jax.ShapeDtypeStruct((n, d), F32), jax.ShapeDtypeStruct((n, d // 2), jnp.uint32),
                   jax.ShapeDtypeStruct((n, LANES), jnp.int32), jax.ShapeDtypeStruct((n, LANES), F32),
                   jax.ShapeDtypeStruct((1, LANES), F32)],
        scratch_shapes=[pltpu.VMEM((1, LANES), F32)],
        compiler_params=_params("arbitrary"),
        name="post_prompt",
    )(x, ycn, ya_t, mk, mv, w["g_attn_out"], w["w_out"], w["g_cross"], w["w_mq"], w["w_mo"], w["g_ffn"],
      w["w_router"], w["b_router"])


def _router(h1, o, w, tm):
    n, d = h1.shape
    tok = lambda width: pl.BlockSpec((tm, width), lambda i: (i, 0))
    return pl.pallas_call(
        _router_kernel,
        grid=(n // tm,),
        in_specs=[tok(d), tok(d), _full((d, d)), _full((1, d)), _full((d, LANES)), _full((1, LANES))],
        out_specs=[tok(d), tok(d // 2), tok(LANES), tok(LANES), tok(LANES), _full((1, LANES))],
        out_shape=[jax.ShapeDtypeStruct((n, d), F32), jax.ShapeDtypeStruct((n, d // 2), jnp.uint32),
                   jax.ShapeDtypeStruct((n, LANES), F32), jax.ShapeDtypeStruct((n, LANES), jnp.int32),
                   jax.ShapeDtypeStruct((n, LANES), F32), jax.ShapeDtypeStruct((1, LANES), F32)],
        scratch_shapes=[pltpu.VMEM((1, LANES), F32)],
        compiler_params=_params("arbitrary"),
        name="router",
    )(h1, o, w["w_mo"], w["g_ffn"], w["w_router"], w["b_router"])


SC_CORES = 2
SC_SUBCORES = 16
SC_WORKERS = SC_CORES * SC_SUBCORES
SC_CHUNK = 32
EXPERT_ROWS = 512
EXPERTS_VMEM_LIMIT = 56 * 1024 * 1024


def _sc_gather(table, idx):
    b = idx.shape[0]
    d = table.shape[1]
    per_worker = b // SC_WORKERS
    n_chunks = per_worker // SC_CHUNK
    assert per_worker * SC_WORKERS == b and n_chunks * SC_CHUNK == per_worker
    mesh = plsc.VectorSubcoreMesh(core_axis_name="c", subcore_axis_name="s")

    assert n_chunks % 2 == 0
    row_buf = pltpu.VMEM((SC_CHUNK, d), table.dtype)

    @functools.partial(
        pl.kernel, mesh=mesh,
        out_type=jax.ShapeDtypeStruct((b, d), table.dtype),
        scratch_types=[pltpu.VMEM((per_worker,), jnp.int32), row_buf, row_buf] + [pltpu.SemaphoreType.DMA] * 4,
    )
    def gather(table_hbm, idx_hbm, out_hbm, idx_v, rows0, rows1, g0, g1, w0, w1):
        wid = lax.axis_index("s") * SC_CORES + lax.axis_index("c")
        base = pl.multiple_of(wid * per_worker, 8)
        rows, gsem, wsem = (rows0, rows1), (g0, g1), (w0, w1)
        pltpu.sync_copy(idx_hbm.at[pl.ds(base, per_worker)], idx_v)

        def fetch(c, buf):
            ids = idx_v.at[pl.ds(pl.multiple_of(c * SC_CHUNK, 8), SC_CHUNK)]
            return pltpu.make_async_copy(table_hbm.at[ids], rows[buf], gsem[buf])

        def flush(c, buf):
            dst = out_hbm.at[pl.ds(pl.multiple_of(base + c * SC_CHUNK, 8), SC_CHUNK)]
            return pltpu.make_async_copy(rows[buf], dst, wsem[buf])

        fetch(0, 0).start()

        @pl.loop(0, n_chunks, step=2)
        def _(c0):
            for buf in (0, 1):
                c = c0 + buf
                fetch(c, buf).wait()

                @pl.when(c + 1 < n_chunks)
                def _():
                    @pl.when(c >= 1)
                    def _():
                        flush(c - 1, 1 - buf).wait()

                    fetch(c + 1, 1 - buf).start()

                flush(c, buf).start()

        flush(n_chunks - 2, 0).wait()
        flush(n_chunks - 1, 1).wait()

    return gather(table, idx)


def _sc_scatter_rows(x, slot_rows, n_out):
    n, d = x.shape
    per_worker = n // SC_WORKERS
    n_chunks = per_worker // SC_CHUNK
    assert per_worker * SC_WORKERS == n and n_chunks * SC_CHUNK == per_worker and n_chunks % 2 == 0
    assert slot_rows.shape == (n // SC_CHUNK * TOP_K, SC_CHUNK)
    idx_rows = n_chunks * TOP_K
    mesh = plsc.VectorSubcoreMesh(core_axis_name="c", subcore_axis_name="s")
    row_buf = pltpu.VMEM((SC_CHUNK, d), x.dtype)

    @functools.partial(
        pl.kernel, mesh=mesh,
        out_type=jax.ShapeDtypeStruct((n_out, d), x.dtype),
        scratch_types=[pltpu.VMEM((idx_rows, SC_CHUNK), jnp.int32), row_buf, row_buf]
        + [pltpu.SemaphoreType.DMA] * 4,
    )
    def scatter(x_hbm, idx_hbm, out_hbm, idx_v, rows0, rows1, r0, r1, s0, s1):
        wid = lax.axis_index("s") * SC_CORES + lax.axis_index("c")
        base = pl.multiple_of(wid * per_worker, 8)
        rows, rsem, ssem = (rows0, rows1), (r0, r1), (s0, s1)
        pltpu.sync_copy(idx_hbm.at[pl.ds(pl.multiple_of(wid * idx_rows, 8), idx_rows)], idx_v)

        def fetch(c, buf):
            src = x_hbm.at[pl.ds(pl.multiple_of(base + c * SC_CHUNK, 8), SC_CHUNK)]
            return pltpu.make_async_copy(src, rows[buf], rsem[buf])

        def spread(c, k, buf):
            return pltpu.make_async_copy(rows[buf], out_hbm.at[idx_v.at[c * TOP_K + k]], ssem[buf])

        fetch(0, 0).start()

        @pl.loop(0, n_chunks, step=2)
        def _(c0):
            for buf in (0, 1):
                c = c0 + buf
                fetch(c, buf).wait()

                @pl.when(c + 1 < n_chunks)
                def _():
                    @pl.when(c >= 1)
                    def _():
                        for k in range(TOP_K):
                            spread(c - 1, k, 1 - buf).wait()

                    fetch(c + 1, 1 - buf).start()

                for k in range(TOP_K):
                    spread(c, k, buf).start()

        for k in range(TOP_K):
            spread(n_chunks - 2, k, 0).wait()
        for k in range(TOP_K):
            spread(n_chunks - 1, k, 1).wait()

    return scatter(x, slot_rows)


def _swiglu_expert(x, wgu, bgu, wd, bd):
    gu = _dot(x, wgu) + bgu
    g = jnp.minimum(gu[:, :D_FF], SWIGLU_LIMIT)
    u = jnp.clip(gu[:, D_FF:], -SWIGLU_LIMIT, SWIGLU_LIMIT)
    hdn = (u + 1.0) * (g * (1.0 / (1.0 + jnp.exp(-SWIGLU_ALPHA * g))))
    return _dot(hdn.astype(BF16), wd) + bd


def _pack_rows(x):
    bits = lax.bitcast_convert_type(x, jnp.uint32)
    n = x.shape[1] // 2
    return bits[:, :n] | (bits[:, n:] >> 16)


def _unpack_rows_f32(p):
    hi = lax.bitcast_convert_type(p & jnp.uint32(0xFFFF0000), F32)
    lo = lax.bitcast_convert_type(p << 16, F32)
    return jnp.concatenate([hi, lo], axis=1)


def _unpack_rows(p):
    return _unpack_rows_f32(p).astype(BF16)


def _experts_kernel(te_ref, tb_ref, first_ref, used_ref, xs_ref, wgu_ref, bgu_ref, wd_ref, bd_ref, ys_ref,
                    wgu_sc, wd_sc):
    del te_ref, tb_ref
    t = pl.program_id(0)

    @pl.when(t < used_ref[0])
    def _():
        @pl.when(first_ref[t] == 1)
        def _():
            wgu_sc[...] = wgu_ref[0].astype(BF16)
            wd_sc[...] = wd_ref[0].astype(BF16)

        ys = _swiglu_expert(_unpack_rows(xs_ref[...]), wgu_sc[...], bgu_ref[0], wd_sc[...], bd_ref[0])
        ys_ref[...] = _pack_rows(ys.astype(BF16).astype(F32))


def _experts(tile_expert, tile_block, tile_first, used, xs, w):
    n_slots = xs.shape[0]
    d = D_MODEL
    r = EXPERT_ROWS
    rows = lambda width: pl.BlockSpec((r, width), lambda t, te, tb, tf, u: (tb[t], 0))
    per_e = lambda *shape: pl.BlockSpec((1,) + shape, lambda t, te, tb, tf, u: (te[t],) + (0,) * len(shape))
    grid_spec = pltpu.PrefetchScalarGridSpec(
        num_scalar_prefetch=4,
        grid=(n_slots // r,),
        in_specs=[rows(d // 2), per_e(d, 2 * D_FF), per_e(1, 2 * D_FF), per_e(D_FF, d), per_e(1, d)],
        out_specs=rows(d // 2),
        scratch_shapes=[pltpu.VMEM((d, 2 * D_FF), BF16), pltpu.VMEM((D_FF, d), BF16)],
    )
    return pl.pallas_call(
        _experts_kernel,
        grid_spec=grid_spec,
        out_shape=jax.ShapeDtypeStruct((n_slots, d // 2), jnp.uint32),
        compiler_params=pltpu.CompilerParams(dimension_semantics=("arbitrary",),
                                             vmem_limit_bytes=EXPERTS_VMEM_LIMIT),
        name="experts",
    )(tile_expert, tile_block, tile_first, used, xs, w["w_gate_up"], w["b_gate_up"], w["w_down"], w["b_down"])


def _combine_kernel(h2_ref, gw_ref, y0_ref, y1_ref, y2_ref, y3_ref, gfin_ref, y_ref):
    gw = gw_ref[...]
    moe = None
    for k, yk in enumerate((y0_ref, y1_ref, y2_ref, y3_ref)):
        part = gw[:, k:k + 1] * _unpack_rows_f32(yk[0])
        moe = part if moe is None else moe + part
    y_ref[...] = _rms(h2_ref[...] + moe, gfin_ref[...])


def _combine(h2, gw, ys4, w, tm):
    n, d = h2.shape
    tok = lambda width: pl.BlockSpec((tm, width), lambda i: (i, 0))
    part = lambda k: pl.BlockSpec((1, tm, d // 2), lambda i, k=k: (k, i, 0))
    return pl.pallas_call(
        _combine_kernel,
        grid=(n // tm,),
        in_specs=[tok(d), tok(LANES)] + [part(k) for k in range(TOP_K)] + [_full((1, d))],
        out_specs=tok(d),
        out_shape=jax.ShapeDtypeStruct((n, d), F32),
        compiler_params=_params("arbitrary"),
        name="moe_combine",
    )(h2, gw, *([ys4] * TOP_K), w["g_final"])


def _routed_moe(h2, xt, route, gw, counts, w):
    n, d = h2.shape
    r = EXPERT_ROWS
    n_tiles = (n * TOP_K) // r + N_EXPERTS
    n_slots = n_tiles * r
    cnt = counts[0, :N_EXPERTS].astype(jnp.int32)
    padded = ((cnt + r - 1) // r) * r
    ends = jnp.cumsum(padded)
    starts = ends - padded
    eid = route[:, 0:TOP_K]
    rank = route[:, TOP_K:2 * TOP_K]
    onehot = eid[:, :, None] == jnp.arange(N_EXPERTS, dtype=jnp.int32)[None, None, :]
    slot = jnp.sum(jnp.where(onehot, starts[None, None, :], 0), axis=-1) + rank
    used = (ends[-1] // r).astype(jnp.int32)
    tile_block = jnp.minimum(jnp.arange(n_tiles, dtype=jnp.int32), used - 1)
    tile_expert = jnp.minimum(
        jnp.sum((tile_block[:, None] >= (ends // r)[None, :]).astype(jnp.int32), axis=1), N_EXPERTS - 1)
    tile_first = jnp.concatenate([jnp.ones((1,), jnp.int32),
                                  (tile_expert[1:] != tile_expert[:-1]).astype(jnp.int32)])
    slot_rows = slot.reshape(n // SC_CHUNK, SC_CHUNK, TOP_K).transpose(0, 2, 1).reshape(-1, SC_CHUNK)
    xs = _sc_scatter_rows(xt, slot_rows, n_slots)
    ys = _experts(tile_expert, tile_block, tile_first, used.reshape(1), xs, w)
    ys4 = _sc_gather(ys, slot.T.reshape(-1)).reshape(TOP_K, n, d // 2)
    return _combine(h2, gw, ys4, w, tm=512)


def _moe_kernel(xt_ref, gate_ref, h2_ref, wgu_ref, bgu_ref, wd_ref, bd_ref, gfin_ref, y_ref, acc_sc):
    e = pl.program_id(1)

    @pl.when(e == 0)
    def _():
        acc_sc[...] = jnp.zeros(acc_sc.shape, F32)

    down = _swiglu_expert(_unpack_rows(xt_ref[...]), wgu_ref[0].astype(BF16), bgu_ref[0],
                          wd_ref[0].astype(BF16), bd_ref[0])
    gate = gate_ref[...]
    lane = lax.broadcasted_iota(jnp.int32, gate.shape, 1)
    ge = jnp.sum(jnp.where(lane == e, gate, 0.0), axis=1, keepdims=True)
    acc_sc[...] += ge * down

    @pl.when(e == N_EXPERTS - 1)
    def _():
        y_ref[...] = _rms(h2_ref[...] + acc_sc[...], gfin_ref[...])


def _moe(xt, gate, h2, w, tm):
    n, d = h2.shape
    tok = lambda width: pl.BlockSpec((tm, width), lambda i, e: (i, 0))
    return pl.pallas_call(
        _moe_kernel,
        grid=(n // tm, N_EXPERTS),
        in_specs=[tok(d // 2), tok(LANES), tok(d),
                  pl.BlockSpec((1, d, 2 * D_FF), lambda i, e: (e, 0, 0)),
                  pl.BlockSpec((1, 1, 2 * D_FF), lambda i, e: (e, 0, 0)),
                  pl.BlockSpec((1, D_FF, d), lambda i, e: (e, 0, 0)),
                  pl.BlockSpec((1, 1, d), lambda i, e: (e, 0, 0)),
                  _full((1, d))],
        out_specs=tok(d),
        out_shape=jax.ShapeDtypeStruct((n, d), F32),
        scratch_shapes=[pltpu.VMEM((tm, d), F32)],
        compiler_params=_params("arbitrary", "arbitrary"),
        name="moe",
    )(xt, gate, h2, w["w_gate_up"], w["b_gate_up"], w["w_down"], w["b_down"], w["g_final"])


def _rope_tables(pos):
    inv = ROPE_THETA ** (-jnp.arange(HALF_ROPE, dtype=F32) / HALF_ROPE)
    ang = pos.astype(F32)[:, None] * inv[None, :]
    cos, sin = jnp.cos(ang), jnp.sin(ang)
    n = pos.shape[0]
    pad = LANES - QK_NOPE - QK_ROPE
    rc = jnp.concatenate([jnp.ones((n, QK_NOPE), F32), cos, cos, jnp.zeros((n, pad), F32)], axis=1)
    ra = jnp.concatenate([jnp.zeros((n, QK_NOPE), F32), -sin, jnp.zeros((n, HALF_ROPE + pad), F32)], axis=1)
    rb = jnp.concatenate([jnp.zeros((n, QK_NOPE + HALF_ROPE), F32), sin, jnp.zeros((n, pad), F32)], axis=1)
    return rc, ra, rb


def _prepare(l, s_len, n_sample, past, g_mix, w_in, w_conv, g_q, w_uq, g_kv, w_uk, w_uv, g_conv_out, g_attn_out,
             w_out, g_cross, g_mem, w_mq, w_mk, w_mv, w_mo, g_ffn, w_router, b_router, w_gate_up, b_gate_up,
             w_down, b_down, g_final):
    h = MLA_HEADS
    row = lambda g: g.reshape(1, -1).astype(F32)
    head_pad = LANES - QK_NOPE - QK_ROPE
    win = w_in[l]
    win = jnp.concatenate([win[:, :OFF_KR], jnp.zeros((D_MODEL, QK_NOPE), F32), win[:, OFF_KR:],
                           jnp.zeros((D_MODEL, head_pad), F32)], axis=1)
    wuq = jnp.pad(w_uq[l].reshape(Q_LORA, h, QK_NOPE + QK_ROPE), ((0, 0), (0, 0), (0, head_pad)))
    wuk = jnp.pad(w_uk[l], ((0, 0), (0, 0), (0, LANES - QK_NOPE)))
    wabs = jnp.pad(jnp.transpose(w_uk[l], (1, 2, 0)), ((0, 0), (0, LANES - QK_NOPE), (0, 0)))
    eye = jnp.eye(h, dtype=F32)
    wuv_heads = jnp.einsum("chd,hg->hcgd", w_uv[l], eye).reshape(h, KV_LORA, h * V_HEAD)
    wr = jnp.pad(w_router[l], ((0, 0), (0, LANES - N_EXPERTS)))
    br = jnp.pad(b_router[l], (0, LANES - N_EXPERTS)).reshape(1, LANES)
    return {
        "g_mix": row(g_mix[l]), "w_in": win.astype(BF16), "w_conv": w_conv[l].astype(F32),
        "g_q": row(g_q[l]), "w_uq": wuq.reshape(Q_LORA, h * LANES).astype(BF16),
        "g_kv": row(g_kv[l]), "w_uk": wuk.reshape(KV_LORA, h * LANES).astype(BF16),
        "w_uv": w_uv[l].reshape(KV_LORA, h * V_HEAD).T.astype(BF16), "w_abs": wabs.astype(BF16),
        "w_uv_heads": wuv_heads.astype(BF16),
        "g_conv_out": row(g_conv_out[l]), "g_attn_out": row(g_attn_out[l]),
        "w_out": w_out[l].astype(BF16), "g_cross": row(g_cross[l]), "g_mem": row(g_mem[l]),
        "w_mq": w_mq[l].astype(BF16), "w_mk": w_mk[l].astype(BF16), "w_mv": w_mv[l].astype(BF16),
        "w_mo": w_mo[l].astype(BF16), "g_ffn": row(g_ffn[l]),
        "w_router": wr.astype(BF16), "b_router": br.astype(F32),
        "w_gate_up": w_gate_up[l], "b_gate_up": b_gate_up[l].reshape(N_EXPERTS, 1, 2 * D_FF),
        "w_down": w_down[l], "b_down": b_down[l].reshape(N_EXPERTS, 1, D_MODEL),
        "g_final": row(g_final),
        "rope_prompt": _rope_tables(jnp.arange(s_len)),
        "rope_sample": _rope_tables(jnp.full((n_sample,), past)),
    }


def kernel(x_prompt, x_sample, mem_prompt, cache_conv, cache_ckv, cache_krope, cache_mem_k, cache_mem_v, page_table, g_mix, w_in, w_conv, g_q, w_uq, g_kv, w_uk, w_uv, g_conv_out, g_attn_out, w_out, g_cross, g_mem, w_mq, w_mk, w_mv, w_mo, g_ffn, w_router, b_router, w_gate_up, b_gate_up, w_down, b_down, g_final):
    bp, s_len, d = x_prompt.shape
    bd, t_len, _ = x_sample.shape
    depth = g_mix.shape[0]
    assert depth == 1 and t_len == 1, "kernel is written for one layer and one decode token per sequence"
    n_pages = page_table.shape[1]
    past = n_pages * PAGE_SIZE
    n_p = bp * s_len
    l = 0
    w = _prepare(l, s_len, bd, past, g_mix, w_in, w_conv, g_q, w_uq, g_kv, w_uk, w_uv, g_conv_out, g_attn_out,
                 w_out, g_cross, g_mem, w_mq, w_mk, w_mv, w_mo, g_ffn, w_router, b_router, w_gate_up,
                 b_gate_up, w_down, b_down, g_final)

    xs = x_sample.reshape(bd, d)
    prev0 = cache_conv[l, :, 0, :]
    prev1 = cache_conv[l, :, 1, :]
    q_s, qlat_s, lat_s, kr_s, ycn_s, u_s = _inproj_sample(xs, prev0, prev1, w)
    olat = _decode(page_table, jnp.transpose(qlat_s, (1, 0, 2)), jnp.transpose(q_s, (1, 0, 2)),
                   lat_s.reshape(bd, 1, KV_LORA), kr_s.reshape(bd, 1, QK_ROPE), cache_ckv,
                   jnp.swapaxes(cache_krope, 2, 3))
    ya_s = _uv_project(jnp.transpose(olat, (1, 0, 2)), w["w_uv_heads"])
    h1_s, qm_s = _mixout(xs, ycn_s, ya_s, w, tm=bd)
    o_s = _cross_sample(qm_s.astype(F32).reshape(bd, MEM_HEADS, MEM_HEAD_DIM), cache_mem_k[l], cache_mem_v[l], tb=8)
    h2_s, xt_s, gate_s, _, _, _ = _router(h1_s, o_s.reshape(bd, d).astype(BF16), w, tm=bd)
    y_s = _moe(xt_s, gate_s, h2_s, w, tm=bd)

    q, k, v, lat_p, kr_p, ycn_p, conv_p = _inproj_prompt(x_prompt, w, tm=512)
    ya_p = _attention(q, k, v, tq=512)
    mk_p, mv_p = _memory_kv(mem_prompt, w["g_mem"], w["w_mk"], w["w_mv"])
    xp = x_prompt.reshape(n_p, d)
    h2_p, xt_p, route_p, gw_p, cnt_p = _post_prompt(xp, ycn_p.reshape(n_p, -1), ya_p, mk_p, mv_p, w, tm=512)
    y_p = _routed_moe(h2_p, xt_p, route_p, gw_p, cnt_p, w)

    mem_shape = (1, bp, MEM_TOKENS, MEM_HEADS, MEM_HEAD_DIM)
    return (y_p.reshape(bp, s_len, d), y_s.reshape(bd, 1, d),
            conv_p[None], lat_p[None], kr_p[None], mk_p.reshape(mem_shape), mv_p.reshape(mem_shape),
            jnp.stack([prev1, u_s], axis=1)[None], lat_s.reshape(1, bd, 1, KV_LORA),
            kr_s.reshape(1, bd, 1, QK_ROPE))
```

```python
import functools

import jax
import jax.numpy as jnp
from jax import lax
from jax.experimental import pallas as pl
from jax.experimental.pallas import tpu as pltpu
from jax.experimental.pallas import tpu_sc as plsc

D_MODEL = 1024
CONV_WIDTH = 512
CONV_K = 3
MLA_HEADS = 8
QK_NOPE = 64
QK_ROPE = 32
V_HEAD = 64
Q_LORA = 384
KV_LORA = 256
ROPE_THETA = 10000.0
PAGE_SIZE = 128
MEM_TOKENS = 256
MEM_HEADS = 4
MEM_HEAD_DIM = D_MODEL // MEM_HEADS
N_EXPERTS = 32
TOP_K = 4
D_FF = D_MODEL
SWIGLU_LIMIT = 7.0
SWIGLU_ALPHA = 1.702
NORM_EPS = 1e-6

LANES = 128
HALF_ROPE = QK_ROPE // 2
IN_WIDTH = 3 * CONV_WIDTH + Q_LORA + KV_LORA + QK_ROPE
IN_WIDTH_PAD = 3 * CONV_WIDTH + Q_LORA + KV_LORA + LANES
OFF_CQ = 3 * CONV_WIDTH
OFF_CKV = OFF_CQ + Q_LORA
OFF_KR = OFF_CKV + KV_LORA
MLA_SCALE = (QK_NOPE + QK_ROPE) ** -0.5
MEM_SCALE = MEM_HEAD_DIM ** -0.5
LOG2_E = 1.4426950408889634
VMEM_LIMIT = 48 * 1024 * 1024
ATTN_HEADS_PER_STEP = 4
PAGES_PER_STEP = 64
DECODE_SLOTS = 3

BF16 = jnp.bfloat16
F32 = jnp.float32
NT_DIMS = (((1,), (1,)), ((), ()))


def _params(*sem):
    return pltpu.CompilerParams(dimension_semantics=sem, vmem_limit_bytes=VMEM_LIMIT)


def _rms(x, g):
    return x * lax.rsqrt(jnp.mean(x * x, axis=-1, keepdims=True) + NORM_EPS) * g


def _dot(a, b):
    return jnp.dot(a, b, preferred_element_type=F32)


def _dot_nt(a, b):
    return lax.dot_general(a, b, NT_DIMS, preferred_element_type=F32)


def _rope_group(x, rc, ra, rb):
    return x * rc + pltpu.roll(x, LANES - HALF_ROPE, 1) * ra + pltpu.roll(x, HALF_ROPE, 1) * rb


def _full(shape):
    return pl.BlockSpec(shape, lambda *_: (0,) * len(shape))


def _memkv_kernel(mem_ref, g_ref, wk_ref, wv_ref, k_ref, v_ref):
    m = _rms(mem_ref[0], g_ref[...]).astype(BF16)
    k_ref[0] = _dot(m, wk_ref[...])
    v_ref[0] = _dot(m, wv_ref[...])


def _memory_kv(mem, g_mem, w_mk, w_mv):
    b, n, d = mem.shape
    blk = pl.BlockSpec((1, n, d), lambda i: (i, 0, 0))
    return pl.pallas_call(
        _memkv_kernel,
        grid=(b,),
        in_specs=[blk, _full((1, d)), _full((d, d)), _full((d, d))],
        out_specs=[blk, blk],
        out_shape=[jax.ShapeDtypeStruct((b, n, d), F32)] * 2,
        compiler_params=_params("arbitrary"),
        name="memory_kv",
    )(mem, g_mem, w_mk, w_mv)


def _inproj_common(x, gmix, win, gq, wuq, gkv, rc, ra, rb):
    a = _rms(x, gmix).astype(BF16)
    z = _dot(a, win)
    b_g = z[:, 0:CONV_WIDTH]
    u = z[:, CONV_WIDTH:2 * CONV_WIDTH] * z[:, 2 * CONV_WIDTH:3 * CONV_WIDTH]
    cq = _rms(z[:, OFF_CQ:OFF_CKV], gq).astype(BF16)
    q = _dot(cq, wuq)
    q_heads = [_rope_group(q[:, h * LANES:(h + 1) * LANES], rc, ra, rb) for h in range(MLA_HEADS)]
    lat = _rms(z[:, OFF_CKV:OFF_KR], gkv)
    kr = _rope_group(z[:, OFF_KR:OFF_KR + LANES], rc, ra, rb)
    return b_g, u, q_heads, lat, kr


def _inproj_prompt_kernel(x_ref, gmix_ref, win_ref, wconv_ref, gq_ref, wuq_ref, gkv_ref, wuk_ref, wuv_ref,
                          gco_ref, rc_ref, ra_ref, rb_ref,
                          q_ref, k_ref, v_ref, lat_ref, kr_ref, ycn_ref, conv_ref, ubuf, *, tm):
    j = pl.program_id(1)
    b_g, u, q_heads, lat, kr = _inproj_common(
        x_ref[0], gmix_ref[...], win_ref[...], gq_ref[...], wuq_ref[...], gkv_ref[...],
        rc_ref[...], ra_ref[...], rb_ref[...])
    for h in range(MLA_HEADS):
        q_ref[0, h] = (q_heads[h] * (MLA_SCALE * LOG2_E)).astype(BF16)
    lat_ref[0] = lat
    kr_ref[0] = kr[:, QK_NOPE:QK_NOPE + QK_ROPE]
    lat_b = lat.astype(BF16)
    kn = _dot(lat_b, wuk_ref[...])
    v_t = _dot_nt(wuv_ref[...], lat_b)
    for h in range(MLA_HEADS):
        k_ref[0, h] = (kn[:, h * LANES:(h + 1) * LANES] + kr).astype(BF16)
        v_ref[0, h, 0] = v_t[h * V_HEAD:(h + 1) * V_HEAD, :].astype(BF16)

    @pl.when(j == 0)
    def _():
        ubuf[0:8, :] = jnp.zeros((8, CONV_WIDTH), F32)

    ubuf[8:8 + tm, :] = u
    u1 = ubuf[7:7 + tm, :]
    u2 = ubuf[6:6 + tm, :]
    wc = wconv_ref[...]
    yc = b_g * (wc[0:1, :] * u2 + wc[1:2, :] * u1 + wc[2:3, :] * u)
    ycn_ref[0] = _rms(yc, gco_ref[...]).astype(BF16)
    ubuf[0:8, :] = ubuf[tm:tm + 8, :]
    conv_ref[0] = u[tm - (CONV_K - 1):tm, :]


def _inproj_prompt(x, w, tm):
    b, s, d = x.shape
    h = MLA_HEADS
    tok = lambda n: pl.BlockSpec((1, tm, n), lambda i, j: (i, j, 0))
    head = pl.BlockSpec((1, h, tm, LANES), lambda i, j: (i, 0, j, 0))
    rope = pl.BlockSpec((tm, LANES), lambda i, j: (j, 0))
    qkv_shape = jax.ShapeDtypeStruct((b, h, s, LANES), BF16)
    return pl.pallas_call(
        functools.partial(_inproj_prompt_kernel, tm=tm),
        grid=(b, s // tm),
        in_specs=[tok(d), _full((1, d)), _full((d, IN_WIDTH_PAD)), _full((CONV_K, CONV_WIDTH)),
                  _full((1, Q_LORA)), _full((Q_LORA, h * LANES)), _full((1, KV_LORA)),
                  _full((KV_LORA, h * LANES)), _full((h * V_HEAD, KV_LORA)), _full((1, CONV_WIDTH)),
                  rope, rope, rope],
        out_specs=[head, head, pl.BlockSpec((1, h, 1, V_HEAD, tm), lambda i, j: (i, 0, j, 0, 0)),
                   tok(KV_LORA), tok(QK_ROPE), tok(CONV_WIDTH),
                   pl.BlockSpec((1, CONV_K - 1, CONV_WIDTH), lambda i, j: (i, 0, 0))],
        out_shape=[qkv_shape, qkv_shape, jax.ShapeDtypeStruct((b, h, s // tm, V_HEAD, tm), BF16),
                   jax.ShapeDtypeStruct((b, s, KV_LORA), F32),
                   jax.ShapeDtypeStruct((b, s, QK_ROPE), F32),
                   jax.ShapeDtypeStruct((b, s, CONV_WIDTH), BF16),
                   jax.ShapeDtypeStruct((b, CONV_K - 1, CONV_WIDTH), F32)],
        scratch_shapes=[pltpu.VMEM((tm + 8, CONV_WIDTH), F32)],
        compiler_params=_params("arbitrary", "arbitrary"),
        name="inproj_prompt",
    )(x, w["g_mix"], w["w_in"], w["w_conv"], w["g_q"], w["w_uq"], w["g_kv"], w["w_uk"], w["w_uv"],
      w["g_conv_out"], *w["rope_prompt"])


def _inproj_sample_kernel(x_ref, gmix_ref, win_ref, wconv_ref, gq_ref, wuq_ref, gkv_ref, wabs_ref,
                          gco_ref, rc_ref, ra_ref, rb_ref, p0_ref, p1_ref,
                          q_ref, qlat_ref, lat_ref, kr_ref, ycn_ref, u_ref):
    b_g, u, q_heads, lat, kr = _inproj_common(
        x_ref[...], gmix_ref[...], win_ref[...], gq_ref[...], wuq_ref[...], gkv_ref[...],
        rc_ref[...], ra_ref[...], rb_ref[...])
    for h in range(MLA_HEADS):
        qh = q_heads[h].astype(BF16)
        q_ref[h] = qh
        qlat_ref[h] = _dot(qh, wabs_ref[h]).astype(BF16)
    lat_ref[...] = lat
    kr_ref[...] = kr[:, QK_NOPE:QK_NOPE + QK_ROPE]
    wc = wconv_ref[...]
    yc = b_g * (wc[0:1, :] * p0_ref[...] + wc[1:2, :] * p1_ref[...] + wc[2:3, :] * u)
    ycn_ref[...] = _rms(yc, gco_ref[...]).astype(BF16)
    u_ref[...] = u


def _inproj_sample(x, prev0, prev1, w):
    n, d = x.shape
    h = MLA_HEADS
    return pl.pallas_call(
        _inproj_sample_kernel,
        grid=(1,),
        in_specs=[_full((n, d)), _full((1, d)), _full((d, IN_WIDTH_PAD)), _full((CONV_K, CONV_WIDTH)),
                  _full((1, Q_LORA)), _full((Q_LORA, h * LANES)), _full((1, KV_LORA)),
                  _full((h, LANES, KV_LORA)), _full((1, CONV_WIDTH)),
                  _full((n, LANES)), _full((n, LANES)), _full((n, LANES)),
                  _full((n, CONV_WIDTH)), _full((n, CONV_WIDTH))],
        out_specs=[_full((h, n, LANES)), _full((h, n, KV_LORA)), _full((n, KV_LORA)), _full((n, QK_ROPE)),
                   _full((n, CONV_WIDTH)), _full((n, CONV_WIDTH))],
        out_shape=[jax.ShapeDtypeStruct((h, n, LANES), BF16),
                   jax.ShapeDtypeStruct((h, n, KV_LORA), BF16),
                   jax.ShapeDtypeStruct((n, KV_LORA), F32),
                   jax.ShapeDtypeStruct((n, QK_ROPE), F32),
                   jax.ShapeDtypeStruct((n, CONV_WIDTH), BF16),
                   jax.ShapeDtypeStruct((n, CONV_WIDTH), F32)],
        compiler_params=_params("arbitrary"),
        name="inproj_sample",
    )(x, w["g_mix"], w["w_in"], w["w_conv"], w["g_q"], w["w_uq"], w["g_kv"], w["w_abs"],
      w["g_conv_out"], *w["rope_sample"], prev0, prev1)


def _softmax_step(s, v, m_sc, l_sc, acc_sc):
    m_prev = m_sc[...]
    m_next = jnp.maximum(m_prev, jnp.max(s, axis=1, keepdims=True))
    p = jnp.exp(s - m_next[:, 0:1])
    alpha = jnp.exp(m_prev - m_next)
    l_sc[...] = alpha * l_sc[...] + jnp.sum(p, axis=1, keepdims=True)
    pv = _dot(p.astype(BF16), v)
    acc_sc[...] = acc_sc[...] * alpha[:, 0:1] + pv
    m_sc[...] = m_next


def _attn_kernel(q_ref, k_ref, v_ref, o_ref, m_sc, l_sc, acc_sc, s0_sc, s1_sc, *, tq):
    qi = pl.program_id(2)
    key = lax.broadcasted_iota(jnp.int32, (tq, tq), 0)
    qry = lax.broadcasted_iota(jnp.int32, (tq, tq), 1)
    m_sc[...] = jnp.full(m_sc.shape, -jnp.inf, F32)
    l_sc[...] = jnp.zeros(l_sc.shape, F32)
    acc_sc[...] = jnp.zeros(acc_sc.shape, F32)

    def scores(j, buf):
        for hh in range(ATTN_HEADS_PER_STEP):
            k = k_ref[0, hh, pl.ds(pl.multiple_of(j * tq, tq), tq), :]
            buf[hh] = _dot_nt(k, q_ref[0, hh])

    def consume(j, buf, masked):
        for hh in range(ATTN_HEADS_PER_STEP):
            s = buf[hh]
            if masked:
                s = jnp.where(key <= qry, s, -jnp.inf)
            m_prev = m_sc[hh]
            m_next = jnp.maximum(m_prev, jnp.max(s, axis=0, keepdims=True))
            p = jnp.exp2(s - m_next)
            alpha = jnp.exp2(m_prev - m_next)
            l_sc[hh] = alpha * l_sc[hh] + jnp.sum(p, axis=0, keepdims=True)
            acc_sc[hh] = acc_sc[hh] * alpha + _dot(v_ref[0, hh, j], p.astype(BF16))
            m_sc[hh] = m_next

    def pair(jj, carry):
        j = 2 * jj
        scores(j + 1, s1_sc)
        consume(j, s0_sc, False)
        scores(j + 2, s0_sc)
        consume(j + 1, s1_sc, False)
        return carry

    scores(0, s0_sc)
    lax.fori_loop(0, qi // 2, pair, 0)

    @pl.when(qi % 2 == 0)
    def _():
        consume(qi, s0_sc, True)

    @pl.when(qi % 2 == 1)
    def _():
        scores(qi, s1_sc)
        consume(qi - 1, s0_sc, False)
        consume(qi, s1_sc, True)

    o_ref[0] = jnp.concatenate([acc_sc[hh] / l_sc[hh] for hh in range(ATTN_HEADS_PER_STEP)], axis=0)


def _attention(q, k, v_t, tq):
    b, h, s, _ = q.shape
    hps = ATTN_HEADS_PER_STEP
    qspec = pl.BlockSpec((1, hps, tq, LANES), lambda i, p, j: (i, p, j, 0))
    kspec = pl.BlockSpec((1, hps, s, LANES), lambda i, p, j: (i, p, 0, 0))
    vspec = pl.BlockSpec((1, hps, s // tq, V_HEAD, tq), lambda i, p, j: (i, p, 0, 0, 0))
    return pl.pallas_call(
        functools.partial(_attn_kernel, tq=tq),
        grid=(b, h // hps, s // tq),
        in_specs=[qspec, kspec, vspec],
        out_specs=pl.BlockSpec((1, hps * V_HEAD, tq), lambda i, p, j: (i, p, j)),
        out_shape=jax.ShapeDtypeStruct((b, h * V_HEAD, s), F32),
        scratch_shapes=[pltpu.VMEM((hps, 1, tq), F32), pltpu.VMEM((hps, 1, tq), F32),
                        pltpu.VMEM((hps, V_HEAD, tq), F32),
                        pltpu.VMEM((hps, tq, tq), F32), pltpu.VMEM((hps, tq, tq), F32)],
        compiler_params=_params("arbitrary", "arbitrary", "arbitrary"),
        name="mla_prompt_attention",
    )(q, k, v_t)


def _decode_kernel(pt_ref, qlat_ref, q_ref, lat_ref, krn_ref, ckv_hbm, kr_hbm, o_ref,
                   ckv_buf, kr_buf, sems, m_sc, l_sc, acc_sc, *, n_steps, n_seqs):
    npg = PAGES_PER_STEP
    ahead = DECODE_SLOTS - 1
    b = pl.program_id(0)
    step = pl.program_id(1)
    t = b * n_steps + step
    slot = t % DECODE_SLOTS

    def start_pages(seq, st, sl):
        for i in range(npg):
            page = pt_ref[seq, st * npg + i]
            pltpu.make_async_copy(ckv_hbm.at[0, page], ckv_buf.at[sl, i], sems.at[0, sl]).start(priority=i % 2)
            pltpu.make_async_copy(kr_hbm.at[0, page], kr_buf.at[sl, i], sems.at[1, sl]).start(priority=(i + 1) % 2)

    def wait_pages(sl):
        pltpu.make_async_copy(ckv_hbm.at[0, pl.ds(0, npg)], ckv_buf.at[sl], sems.at[0, sl]).wait()
        pltpu.make_async_copy(kr_hbm.at[0, pl.ds(0, npg)], kr_buf.at[sl], sems.at[1, sl]).wait()

    last = n_seqs * n_steps - 1

    @pl.when(t == 0)
    def _():
        for k in range(ahead):
            start_pages(k // n_steps, k % n_steps, k)

    @pl.when(step == 0)
    def _():
        m_sc[...] = jnp.full(m_sc.shape, -jnp.inf, F32)
        l_sc[...] = jnp.zeros(l_sc.shape, F32)
        acc_sc[...] = jnp.zeros(acc_sc.shape, F32)

    ql = qlat_ref[0]
    qr = q_ref[0][:, QK_NOPE:QK_NOPE + QK_ROPE]
    wait_pages(slot)
    nxt = jnp.minimum(t + ahead, last)
    start_pages(nxt // n_steps, nxt % n_steps, (t + ahead) % DECODE_SLOTS)
    ckv = ckv_buf[slot].reshape(npg * PAGE_SIZE, KV_LORA).astype(BF16)
    kr_t = jnp.concatenate([kr_buf[slot, i] for i in range(npg)], axis=1).astype(BF16)
    s = (_dot_nt(ql, ckv) + _dot(qr, kr_t)) * MLA_SCALE
    _softmax_step(s, ckv, m_sc, l_sc, acc_sc)

    @pl.when(step == n_steps - 1)
    def _():
        lat = lat_ref[0]
        s_new = (jnp.sum(ql.astype(F32) * lat, axis=1, keepdims=True)
                 + jnp.sum(qr.astype(F32) * krn_ref[0], axis=1, keepdims=True)) * MLA_SCALE
        m_prev = m_sc[...]
        m_next = jnp.maximum(m_prev, s_new)
        p_new = jnp.exp(s_new - m_next[:, 0:1])
        alpha = jnp.exp(m_prev - m_next)
        l_fin = alpha * l_sc[...] + p_new
        acc = acc_sc[...] * alpha[:, 0:1] + p_new * lat
        o_ref[0] = acc / l_fin[:, 0:1]

    @pl.when(t == last)
    def _():
        for k in range(1, DECODE_SLOTS):
            wait_pages((t + k) % DECODE_SLOTS)


def _decode(page_table, qlat, q, lat, krn, cache_ckv, cache_krope):
    bd, n_pages = page_table.shape
    npg = PAGES_PER_STEP
    n_steps = n_pages // npg
    assert n_steps * npg == n_pages and bd * n_steps >= DECODE_SLOTS
    h = MLA_HEADS
    per_b =lambda *shape: pl.BlockSpec((1,) + shape, lambda b, s, pt: (b,) + (0,) * len(shape))
    hbm = pl.BlockSpec(memory_space=pl.ANY)
    grid_spec = pltpu.PrefetchScalarGridSpec(
        num_scalar_prefetch=1,
        grid=(bd, n_steps),
        in_specs=[per_b(h, KV_LORA), per_b(h, LANES), per_b(1, KV_LORA), per_b(1, QK_ROPE), hbm, hbm],
        out_specs=per_b(h, KV_LORA),
        scratch_shapes=[pltpu.VMEM((DECODE_SLOTS, npg, PAGE_SIZE, KV_LORA), F32),
                        pltpu.VMEM((DECODE_SLOTS, npg, QK_ROPE, PAGE_SIZE), F32),
                        pltpu.SemaphoreType.DMA((2, DECODE_SLOTS)),
                        pltpu.VMEM((h, LANES), F32), pltpu.VMEM((h, LANES), F32), pltpu.VMEM((h, KV_LORA), F32)],
    )
    return pl.pallas_call(
        functools.partial(_decode_kernel, n_steps=n_steps, n_seqs=bd),
        grid_spec=grid_spec,
        out_shape=jax.ShapeDtypeStruct((bd, h, KV_LORA), F32),
        compiler_params=_params("arbitrary", "arbitrary"),
        name="mla_decode",
    )(page_table, qlat, q, lat, krn, cache_ckv, cache_krope)


def _uv_kernel(olat_ref, wuv_ref, ya_ref):
    acc = None
    for h in range(MLA_HEADS):
        part = _dot(olat_ref[h].astype(BF16), wuv_ref[h])
        acc = part if acc is None else acc + part
    ya_ref[...] = acc


def _uv_project(olat, wuv_heads):
    h, n, c = olat.shape
    width = MLA_HEADS * V_HEAD
    return pl.pallas_call(
        _uv_kernel,
        grid=(1,),
        in_specs=[_full((h, n, c)), _full((h, c, width))],
        out_specs=_full((n, width)),
        out_shape=jax.ShapeDtypeStruct((n, width), F32),
        compiler_params=_params("arbitrary"),
        name="decode_uv",
    )(olat, wuv_heads)


def _mixout_math(x, ycn, ya, ga, wout, gcross, wmq):
    yan = _rms(ya, ga).astype(BF16)
    mixed = jnp.concatenate([ycn, yan], axis=1)
    h1 = x + _dot(mixed, wout)
    return h1, _dot(_rms(h1, gcross).astype(BF16), wmq).astype(BF16)


def _mixout_kernel(x_ref, ycn_ref, ya_ref, ga_ref, wout_ref, gcross_ref, wmq_ref, h1_ref, qm_ref):
    h1_ref[...], qm_ref[...] = _mixout_math(x_ref[...], ycn_ref[...], ya_ref[...], ga_ref[...], wout_ref[...],
                                            gcross_ref[...], wmq_ref[...])


def _mixout(x, ycn, ya, w, tm):
    n, d = x.shape
    width = MLA_HEADS * V_HEAD
    mix = CONV_WIDTH + width
    tok = lambda width: pl.BlockSpec((tm, width), lambda i: (i, 0))
    return pl.pallas_call(
        _mixout_kernel,
        grid=(n // tm,),
        in_specs=[tok(d), tok(CONV_WIDTH), tok(width), _full((1, width)),
                  _full((mix, d)), _full((1, d)), _full((d, d))],
        out_specs=[tok(d), tok(d)],
        out_shape=[jax.ShapeDtypeStruct((n, d), F32), jax.ShapeDtypeStruct((n, d), BF16)],
        compiler_params=_params("arbitrary"),
        name="mix_out",
    )(x, ycn, ya, w["g_attn_out"], w["w_out"], w["g_cross"], w["w_mq"])


def _cross_math(q, k_ref, v_ref):
    outs = []
    for h in range(MEM_HEADS):
        sl = slice(h * MEM_HEAD_DIM, (h + 1) * MEM_HEAD_DIM)
        s = _dot_nt(q[:, sl], k_ref[0, :, sl].astype(BF16)) * MEM_SCALE
        e = jnp.exp(s - jnp.max(s, axis=1, keepdims=True))
        p = e / jnp.sum(e, axis=1, keepdims=True)
        outs.append(_dot(p.astype(BF16), v_ref[0, :, sl].astype(BF16)))
    return jnp.concatenate(outs, axis=1).astype(BF16)


def _cross_sample_kernel(q_ref, k_ref, v_ref, o_ref, *, tb):
    for t in range(tb):
        s = jnp.sum(k_ref[t] * q_ref[t][None], axis=2, keepdims=True) * MEM_SCALE
        e = jnp.exp(s - jnp.max(s, axis=0, keepdims=True))
        p = e / jnp.sum(e, axis=0, keepdims=True)
        o_ref[t] = jnp.sum(p * v_ref[t], axis=0)


def _cross_sample(qm, mk, mv, tb):
    n = qm.shape[0]
    tok = pl.BlockSpec((tb, MEM_HEADS, MEM_HEAD_DIM), lambda i: (i, 0, 0))
    mem = pl.BlockSpec((tb, MEM_TOKENS, MEM_HEADS, MEM_HEAD_DIM), lambda i: (i, 0, 0, 0))
    return pl.pallas_call(
        functools.partial(_cross_sample_kernel, tb=tb),
        grid=(n // tb,),
        in_specs=[tok, mem, mem],
        out_specs=tok,
        out_shape=jax.ShapeDtypeStruct((n, MEM_HEADS, MEM_HEAD_DIM), F32),
        compiler_params=_params("arbitrary"),
        name="cross_sample",
    )(qm, mk, mv)


def _router_math(h1, o, wmo, gffn, wr, br, cnt_prev):
    h2 = h1 + _dot(o, wmo)
    xt = _rms(h2, gffn).astype(BF16)
    logits = _dot(xt, wr) + br
    tm = logits.shape[0]
    lane = lax.broadcasted_iota(jnp.int32, logits.shape, 1)
    logits = jnp.where(lane < N_EXPERTS, logits, -jnp.inf)
    work = logits
    sel = lane < 0
    picks = []
    for k in range(TOP_K):
        m = jnp.max(work, axis=1, keepdims=True)
        idx = jnp.min(jnp.where(work == m, lane, LANES), axis=1, keepdims=True)
        hit = lane == idx
        sel = jnp.logical_or(sel, hit)
        work = jnp.where(hit, -jnp.inf, work)
        picks.append((m, idx, hit))
    top = picks[0][0]
    e = jnp.where(sel, jnp.exp(logits - top), 0.0)
    denom = jnp.sum(e, axis=1, keepdims=True)

    sel_f = sel.astype(F32)
    earlier = (lax.broadcasted_iota(jnp.int32, (tm, tm), 0) > lax.broadcasted_iota(jnp.int32, (tm, tm), 1))
    rank_all = _dot(earlier.astype(BF16), sel_f.astype(BF16)) + cnt_prev
    route = jnp.zeros(logits.shape, jnp.int32)
    gw = jnp.zeros(logits.shape, F32)
    for k, (m, idx, hit) in enumerate(picks):
        rank = jnp.sum(jnp.where(hit, rank_all, 0.0), axis=1, keepdims=True).astype(jnp.int32)
        route = jnp.where(lane == k, idx, route)
        route = jnp.where(lane == TOP_K + k, rank, route)
        gw = jnp.where(lane == k, jnp.exp(m - top) / denom, gw)
    cnt = cnt_prev + jnp.sum(sel_f, axis=0, keepdims=True)
    return h2, _pack_rows(xt.astype(F32)), e / denom, route, gw, cnt


def _router_kernel(h1_ref, o_ref, wmo_ref, gffn_ref, wr_ref, br_ref,
                   h2_ref, xt_ref, gate_ref, route_ref, gw_ref, cnt_ref, cnt_sc):
    @pl.when(pl.program_id(0) == 0)
    def _():
        cnt_sc[...] = jnp.zeros(cnt_sc.shape, F32)

    h2_ref[...], xt_ref[...], gate_ref[...], route_ref[...], gw_ref[...], cnt = _router_math(
        h1_ref[...], o_ref[...], wmo_ref[...], gffn_ref[...], wr_ref[...], br_ref[...], cnt_sc[...])
    cnt_sc[...] = cnt
    cnt_ref[...] = cnt


def _post_prompt_kernel(x_ref, ycn_ref, yat_ref, k_ref, v_ref, ga_ref, wout_ref, gcross_ref, wmq_ref,
                        wmo_ref, gffn_ref, wr_ref, br_ref, h2_ref, xt_ref, route_ref, gw_ref, cnt_ref, cnt_sc):
    @pl.when(pl.program_id(0) == 0)
    def _():
        cnt_sc[...] = jnp.zeros(cnt_sc.shape, F32)

    h1, qm = _mixout_math(x_ref[...], ycn_ref[...], yat_ref[0].T, ga_ref[...], wout_ref[...], gcross_ref[...],
                          wmq_ref[...])
    o = _cross_math(qm, k_ref, v_ref)
    h2_ref[...], xt_ref[...], _, route_ref[...], gw_ref[...], cnt = _router_math(
        h1, o, wmo_ref[...], gffn_ref[...], wr_ref[...], br_ref[...], cnt_sc[...])
    cnt_sc[...] = cnt
    cnt_ref[...] = cnt


def _post_prompt(x, ycn, ya_t, mk, mv, w, tm):
    n, d = x.shape
    width = MLA_HEADS * V_HEAD
    tiles = ya_t.shape[2] // tm
    tok = lambda width: pl.BlockSpec((tm, width), lambda i: (i, 0))
    mem = pl.BlockSpec((1, MEM_TOKENS, d), lambda i: (i // tiles, 0, 0))
    return pl.pallas_call(
        _post_prompt_kernel,
        grid=(n // tm,),
        in_specs=[tok(d), tok(CONV_WIDTH), pl.BlockSpec((1, width, tm), lambda i: (i // tiles, 0, i % tiles)),
                  mem, mem, _full((1, width)), _full((CONV_WIDTH + width, d)), _full((1, d)), _full((d, d)),
                  _full((d, d)), _full((1, d)), _full((d, LANES)), _full((1, LANES))],
        out_specs=[tok(d), tok(d // 2), tok(LANES), tok(LANES), _full((1, LANES))],
        out_shape=[jax.ShapeDtypeStruct((n, d), F32), jax.ShapeDtypeStruct((n, d // 2), jnp.uint32),
                   jax.ShapeDtypeStruct((n, LANES), jnp.int32), jax.ShapeDtypeStruct((n, LANES), F32),
                   jax.ShapeDtypeStruct((1, LANES), F32)],
        scratch_shapes=[pltpu.VMEM((1, LANES), F32)],
        compiler_params=_params("arbitrary"),
        name="post_prompt",
    )(x, ycn, ya_t, mk, mv, w["g_attn_out"], w["w_out"], w["g_cross"], w["w_mq"], w["w_mo"], w["g_ffn"],
      w["w_router"], w["b_router"])


def _router(h1, o, w, tm):
    n, d = h1.shape
    tok = lambda width: pl.BlockSpec((tm, width), lambda i: (i, 0))
    return pl.pallas_call(
        _router_kernel,
        grid=(n // tm,),
        in_specs=[tok(d), tok(d), _full((d, d)), _full((1, d)), _full((d, LANES)), _full((1, LANES))],
        out_specs=[tok(d), tok(d // 2), tok(LANES), tok(LANES), tok(LANES), _full((1, LANES))],
        out_shape=[jax.ShapeDtypeStruct((n, d), F32), jax.ShapeDtypeStruct((n, d // 2), jnp.uint32),
                   jax.ShapeDtypeStruct((n, LANES), F32), jax.ShapeDtypeStruct((n, LANES), jnp.int32),
                   jax.ShapeDtypeStruct((n, LANES), F32), jax.ShapeDtypeStruct((1, LANES), F32)],
        scratch_shapes=[pltpu.VMEM((1, LANES), F32)],
        compiler_params=_params("arbitrary"),
        name="router",
    )(h1, o, w["w_mo"], w["g_ffn"], w["w_router"], w["b_router"])


SC_CORES = 2
SC_SUBCORES = 16
SC_WORKERS = SC_CORES * SC_SUBCORES
SC_CHUNK = 32
EXPERT_ROWS = 512
EXPERTS_VMEM_LIMIT = 56 * 1024 * 1024


def _sc_gather(table, idx):
    b = idx.shape[0]
    d = table.shape[1]
    per_worker = b // SC_WORKERS
    n_chunks = per_worker // SC_CHUNK
    assert per_worker * SC_WORKERS == b and n_chunks * SC_CHUNK == per_worker
    mesh = plsc.VectorSubcoreMesh(core_axis_name="c", subcore_axis_name="s")

    assert n_chunks % 2 == 0
    row_buf = pltpu.VMEM((SC_CHUNK, d), table.dtype)

    @functools.partial(
        pl.kernel, mesh=mesh,
        out_type=jax.ShapeDtypeStruct((b, d), table.dtype),
        scratch_types=[pltpu.VMEM((per_worker,), jnp.int32), row_buf, row_buf] + [pltpu.SemaphoreType.DMA] * 4,
    )
    def gather(table_hbm, idx_hbm, out_hbm, idx_v, rows0, rows1, g0, g1, w0, w1):
        wid = lax.axis_index("s") * SC_CORES + lax.axis_index("c")
        base = pl.multiple_of(wid * per_worker, 8)
        rows, gsem, wsem = (rows0, rows1), (g0, g1), (w0, w1)
        pltpu.sync_copy(idx_hbm.at[pl.ds(base, per_worker)], idx_v)

        def fetch(c, buf):
            ids = idx_v.at[pl.ds(pl.multiple_of(c * SC_CHUNK, 8), SC_CHUNK)]
            return pltpu.make_async_copy(table_hbm.at[ids], rows[buf], gsem[buf])

        def flush(c, buf):
            dst = out_hbm.at[pl.ds(pl.multiple_of(base + c * SC_CHUNK, 8), SC_CHUNK)]
            return pltpu.make_async_copy(rows[buf], dst, wsem[buf])

        fetch(0, 0).start()

        @pl.loop(0, n_chunks, step=2)
        def _(c0):
            for buf in (0, 1):
                c = c0 + buf
                fetch(c, buf).wait()

                @pl.when(c + 1 < n_chunks)
                def _():
                    @pl.when(c >= 1)
                    def _():
                        flush(c - 1, 1 - buf).wait()

                    fetch(c + 1, 1 - buf).start()

                flush(c, buf).start()

        flush(n_chunks - 2, 0).wait()
        flush(n_chunks - 1, 1).wait()

    return gather(table, idx)


def _sc_scatter_rows(x, slot_rows, n_out):
    n, d = x.shape
    per_worker = n // SC_WORKERS
    n_chunks = per_worker // SC_CHUNK
    assert per_worker * SC_WORKERS == n and n_chunks * SC_CHUNK == per_worker and n_chunks % 2 == 0
    assert slot_rows.shape == (n // SC_CHUNK * TOP_K, SC_CHUNK)
    idx_rows = n_chunks * TOP_K
    mesh = plsc.VectorSubcoreMesh(core_axis_name="c", subcore_axis_name="s")
    row_buf = pltpu.VMEM((SC_CHUNK, d), x.dtype)

    @functools.partial(
        pl.kernel, mesh=mesh,
        out_type=jax.ShapeDtypeStruct((n_out, d), x.dtype),
        scratch_types=[pltpu.VMEM((idx_rows, SC_CHUNK), jnp.int32), row_buf, row_buf]
        + [pltpu.SemaphoreType.DMA] * 4,
    )
    def scatter(x_hbm, idx_hbm, out_hbm, idx_v, rows0, rows1, r0, r1, s0, s1):
        wid = lax.axis_index("s") * SC_CORES + lax.axis_index("c")
        base = pl.multiple_of(wid * per_worker, 8)
        rows, rsem, ssem = (rows0, rows1), (r0, r1), (s0, s1)
        pltpu.sync_copy(idx_hbm.at[pl.ds(pl.multiple_of(wid * idx_rows, 8), idx_rows)], idx_v)

        def fetch(c, buf):
            src = x_hbm.at[pl.ds(pl.multiple_of(base + c * SC_CHUNK, 8), SC_CHUNK)]
            return pltpu.make_async_copy(src, rows[buf], rsem[buf])

        def spread(c, k, buf):
            return pltpu.make_async_copy(rows[buf], out_hbm.at[idx_v.at[c * TOP_K + k]], ssem[buf])

        fetch(0, 0).start()

        @pl.loop(0, n_chunks, step=2)
        def _(c0):
            for buf in (0, 1):
                c = c0 + buf
                fetch(c, buf).wait()

                @pl.when(c + 1 < n_chunks)
                def _():
                    @pl.when(c >= 1)
                    def _():
                        for k in range(TOP_K):
                            spread(c - 1, k, 1 - buf).wait()

                    fetch(c + 1, 1 - buf).start()

                for k in range(TOP_K):
                    spread(c, k, buf).start()

        for k in range(TOP_K):
            spread(n_chunks - 2, k, 0).wait()
        for k in range(TOP_K):
            spread(n_chunks - 1, k, 1).wait()

    return scatter(x, slot_rows)


def _swiglu_expert(x, wgu, bgu, wd, bd):
    gu = _dot(x, wgu) + bgu
    g = jnp.minimum(gu[:, :D_FF], SWIGLU_LIMIT)
    u = jnp.clip(gu[:, D_FF:], -SWIGLU_LIMIT, SWIGLU_LIMIT)
    hdn = (u + 1.0) * (g * (1.0 / (1.0 + jnp.exp(-SWIGLU_ALPHA * g))))
    return _dot(hdn.astype(BF16), wd) + bd


def _pack_rows(x):
    bits = lax.bitcast_convert_type(x, jnp.uint32)
    n = x.shape[1] // 2
    return bits[:, :n] | (bits[:, n:] >> 16)


def _unpack_rows_f32(p):
    hi = lax.bitcast_convert_type(p & jnp.uint32(0xFFFF0000), F32)
    lo = lax.bitcast_convert_type(p << 16, F32)
    return jnp.concatenate([hi, lo], axis=1)


def _unpack_rows(p):
    return _unpack_rows_f32(p).astype(BF16)


def _experts_kernel(te_ref, tb_ref, first_ref, used_ref, xs_ref, wgu_ref, bgu_ref, wd_ref, bd_ref, ys_ref,
                    wgu_sc, wd_sc):
    del te_ref, tb_ref
    t = pl.program_id(0)

    @pl.when(t < used_ref[0])
    def _():
        @pl.when(first_ref[t] == 1)
        def _():
            wgu_sc[...] = wgu_ref[0].astype(BF16)
            wd_sc[...] = wd_ref[0].astype(BF16)

        ys = _swiglu_expert(_unpack_rows(xs_ref[...]), wgu_sc[...], bgu_ref[0], wd_sc[...], bd_ref[0])
        ys_ref[...] = _pack_rows(ys.astype(BF16).astype(F32))


def _experts(tile_expert, tile_block, tile_first, used, xs, w):
    n_slots = xs.shape[0]
    d = D_MODEL
    r = EXPERT_ROWS
    rows = lambda width: pl.BlockSpec((r, width), lambda t, te, tb, tf, u: (tb[t], 0))
    per_e = lambda *shape: pl.BlockSpec((1,) + shape, lambda t, te, tb, tf, u: (te[t],) + (0,) * len(shape))
    grid_spec = pltpu.PrefetchScalarGridSpec(
        num_scalar_prefetch=4,
        grid=(n_slots // r,),
        in_specs=[rows(d // 2), per_e(d, 2 * D_FF), per_e(1, 2 * D_FF), per_e(D_FF, d), per_e(1, d)],
        out_specs=rows(d // 2),
        scratch_shapes=[pltpu.VMEM((d, 2 * D_FF), BF16), pltpu.VMEM((D_FF, d), BF16)],
    )
    return pl.pallas_call(
        _experts_kernel,
        grid_spec=grid_spec,
        out_shape=jax.ShapeDtypeStruct((n_slots, d // 2), jnp.uint32),
        compiler_params=pltpu.CompilerParams(dimension_semantics=("arbitrary",),
                                             vmem_limit_bytes=EXPERTS_VMEM_LIMIT),
        name="experts",
    )(tile_expert, tile_block, tile_first, used, xs, w["w_gate_up"], w["b_gate_up"], w["w_down"], w["b_down"])


def _combine_kernel(h2_ref, gw_ref, y0_ref, y1_ref, y2_ref, y3_ref, gfin_ref, y_ref):
    gw = gw_ref[...]
    moe = None
    for k, yk in enumerate((y0_ref, y1_ref, y2_ref, y3_ref)):
        part = gw[:, k:k + 1] * _unpack_rows_f32(yk[0])
        moe = part if moe is None else moe + part
    y_ref[...] = _rms(h2_ref[...] + moe, gfin_ref[...])


def _combine(h2, gw, ys4, w, tm):
    n, d = h2.shape
    tok = lambda width: pl.BlockSpec((tm, width), lambda i: (i, 0))
    part = lambda k: pl.BlockSpec((1, tm, d // 2), lambda i, k=k: (k, i, 0))
    return pl.pallas_call(
        _combine_kernel,
        grid=(n // tm,),
        in_specs=[tok(d), tok(LANES)] + [part(k) for k in range(TOP_K)] + [_full((1, d))],
        out_specs=tok(d),
        out_shape=jax.ShapeDtypeStruct((n, d), F32),
        compiler_params=_params("arbitrary"),
        name="moe_combine",
    )(h2, gw, *([ys4] * TOP_K), w["g_final"])


def _routed_moe(h2, xt, route, gw, counts, w):
    n, d = h2.shape
    r = EXPERT_ROWS
    n_tiles = (n * TOP_K) // r + N_EXPERTS
    n_slots = n_tiles * r
    cnt = counts[0, :N_EXPERTS].astype(jnp.int32)
    padded = ((cnt + r - 1) // r) * r
    ends = jnp.cumsum(padded)
    starts = ends - padded
    eid = route[:, 0:TOP_K]
    rank = route[:, TOP_K:2 * TOP_K]
    onehot = eid[:, :, None] == jnp.arange(N_EXPERTS, dtype=jnp.int32)[None, None, :]
    slot = jnp.sum(jnp.where(onehot, starts[None, None, :], 0), axis=-1) + rank
    used = (ends[-1] // r).astype(jnp.int32)
    tile_block = jnp.minimum(jnp.arange(n_tiles, dtype=jnp.int32), used - 1)
    tile_expert = jnp.minimum(
        jnp.sum((tile_block[:, None] >= (ends // r)[None, :]).astype(jnp.int32), axis=1), N_EXPERTS - 1)
    tile_first = jnp.concatenate([jnp.ones((1,), jnp.int32),
                                  (tile_expert[1:] != tile_expert[:-1]).astype(jnp.int32)])
    slot_rows = slot.reshape(n // SC_CHUNK, SC_CHUNK, TOP_K).transpose(0, 2, 1).reshape(-1, SC_CHUNK)
    xs = _sc_scatter_rows(xt, slot_rows, n_slots)
    ys = _experts(tile_expert, tile_block, tile_first, used.reshape(1), xs, w)
    ys4 = _sc_gather(ys, slot.T.reshape(-1)).reshape(TOP_K, n, d // 2)
    return _combine(h2, gw, ys4, w, tm=512)


def _moe_kernel(xt_ref, gate_ref, h2_ref, wgu_ref, bgu_ref, wd_ref, bd_ref, gfin_ref, y_ref, acc_sc):
    e = pl.program_id(1)

    @pl.when(e == 0)
    def _():
        acc_sc[...] = jnp.zeros(acc_sc.shape, F32)

    down = _swiglu_expert(_unpack_rows(xt_ref[...]), wgu_ref[0].astype(BF16), bgu_ref[0],
                          wd_ref[0].astype(BF16), bd_ref[0])
    gate = gate_ref[...]
    lane = lax.broadcasted_iota(jnp.int32, gate.shape, 1)
    ge = jnp.sum(jnp.where(lane == e, gate, 0.0), axis=1, keepdims=True)
    acc_sc[...] += ge * down

    @pl.when(e == N_EXPERTS - 1)
    def _():
        y_ref[...] = _rms(h2_ref[...] + acc_sc[...], gfin_ref[...])


def _moe(xt, gate, h2, w, tm):
    n, d = h2.shape
    tok = lambda width: pl.BlockSpec((tm, width), lambda i, e: (i, 0))
    return pl.pallas_call(
        _moe_kernel,
        grid=(n // tm, N_EXPERTS),
        in_specs=[tok(d // 2), tok(LANES), tok(d),
                  pl.BlockSpec((1, d, 2 * D_FF), lambda i, e: (e, 0, 0)),
                  pl.BlockSpec((1, 1, 2 * D_FF), lambda i, e: (e, 0, 0)),
                  pl.BlockSpec((1, D_FF, d), lambda i, e: (e, 0, 0)),
                  pl.BlockSpec((1, 1, d), lambda i, e: (e, 0, 0)),
                  _full((1, d))],
        out_specs=tok(d),
        out_shape=jax.ShapeDtypeStruct((n, d), F32),
        scratch_shapes=[pltpu.VMEM((tm, d), F32)],
        compiler_params=_params("arbitrary", "arbitrary"),
        name="moe",
    )(xt, gate, h2, w["w_gate_up"], w["b_gate_up"], w["w_down"], w["b_down"], w["g_final"])


def _rope_tables(pos):
    inv = ROPE_THETA ** (-jnp.arange(HALF_ROPE, dtype=F32) / HALF_ROPE)
    ang = pos.astype(F32)[:, None] * inv[None, :]
    cos, sin = jnp.cos(ang), jnp.sin(ang)
    n = pos.shape[0]
    pad = LANES - QK_NOPE - QK_ROPE
    rc = jnp.concatenate([jnp.ones((n, QK_NOPE), F32), cos, cos, jnp.zeros((n, pad), F32)], axis=1)
    ra = jnp.concatenate([jnp.zeros((n, QK_NOPE), F32), -sin, jnp.zeros((n, HALF_ROPE + pad), F32)], axis=1)
    rb = jnp.concatenate([jnp.zeros((n, QK_NOPE + HALF_ROPE), F32), sin, jnp.zeros((n, pad), F32)], axis=1)
    return rc, ra, rb


def _prepare(l, s_len, n_sample, past, g_mix, w_in, w_conv, g_q, w_uq, g_kv, w_uk, w_uv, g_conv_out, g_attn_out,
             w_out, g_cross, g_mem, w_mq, w_mk, w_mv, w_mo, g_ffn, w_router, b_router, w_gate_up, b_gate_up,
             w_down, b_down, g_final):
    h = MLA_HEADS
    row = lambda g: g.reshape(1, -1).astype(F32)
    head_pad = LANES - QK_NOPE - QK_ROPE
    win = w_in[l]
    win = jnp.concatenate([win[:, :OFF_KR], jnp.zeros((D_MODEL, QK_NOPE), F32), win[:, OFF_KR:],
                           jnp.zeros((D_MODEL, head_pad), F32)], axis=1)
    wuq = jnp.pad(w_uq[l].reshape(Q_LORA, h, QK_NOPE + QK_ROPE), ((0, 0), (0, 0), (0, head_pad)))
    wuk = jnp.pad(w_uk[l], ((0, 0), (0, 0), (0, LANES - QK_NOPE)))
    wabs = jnp.pad(jnp.transpose(w_uk[l], (1, 2, 0)), ((0, 0), (0, LANES - QK_NOPE), (0, 0)))
    eye = jnp.eye(h, dtype=F32)
    wuv_heads = jnp.einsum("chd,hg->hcgd", w_uv[l], eye).reshape(h, KV_LORA, h * V_HEAD)
    wr = jnp.pad(w_router[l], ((0, 0), (0, LANES - N_EXPERTS)))
    br = jnp.pad(b_router[l], (0, LANES - N_EXPERTS)).reshape(1, LANES)
    return {
        "g_mix": row(g_mix[l]), "w_in": win.astype(BF16), "w_conv": w_conv[l].astype(F32),
        "g_q": row(g_q[l]), "w_uq": wuq.reshape(Q_LORA, h * LANES).astype(BF16),
        "g_kv": row(g_kv[l]), "w_uk": wuk.reshape(KV_LORA, h * LANES).astype(BF16),
        "w_uv": w_uv[l].reshape(KV_LORA, h * V_HEAD).T.astype(BF16), "w_abs": wabs.astype(BF16),
        "w_uv_heads": wuv_heads.astype(BF16),
        "g_conv_out": row(g_conv_out[l]), "g_attn_out": row(g_attn_out[l]),
        "w_out": w_out[l].astype(BF16), "g_cross": row(g_cross[l]), "g_mem": row(g_mem[l]),
        "w_mq": w_mq[l].astype(BF16), "w_mk": w_mk[l].astype(BF16), "w_mv": w_mv[l].astype(BF16),
        "w_mo": w_mo[l].astype(BF16), "g_ffn": row(g_ffn[l]),
        "w_router": wr.astype(BF16), "b_router": br.astype(F32),
        "w_gate_up": w_gate_up[l], "b_gate_up": b_gate_up[l].reshape(N_EXPERTS, 1, 2 * D_FF),
        "w_down": w_down[l], "b_down": b_down[l].reshape(N_EXPERTS, 1, D_MODEL),
        "g_final": row(g_final),
        "rope_prompt": _rope_tables(jnp.arange(s_len)),
        "rope_sample": _rope_tables(jnp.full((n_sample,), past)),
    }


def kernel(x_prompt, x_sample, mem_prompt, cache_conv, cache_ckv, cache_krope, cache_mem_k, cache_mem_v, page_table, g_mix, w_in, w_conv, g_q, w_uq, g_kv, w_uk, w_uv, g_conv_out, g_attn_out, w_out, g_cross, g_mem, w_mq, w_mk, w_mv, w_mo, g_ffn, w_router, b_router, w_gate_up, b_gate_up, w_down, b_down, g_final):
    bp, s_len, d = x_prompt.shape
    bd, t_len, _ = x_sample.shape
    depth = g_mix.shape[0]
    assert depth == 1 and t_len == 1, "kernel is written for one layer and one decode token per sequence"
    n_pages = page_table.shape[1]
    past = n_pages * PAGE_SIZE
    n_p = bp * s_len
    l = 0
    w = _prepare(l, s_len, bd, past, g_mix, w_in, w_conv, g_q, w_uq, g_kv, w_uk, w_uv, g_conv_out, g_attn_out,
                 w_out, g_cross, g_mem, w_mq, w_mk, w_mv, w_mo, g_ffn, w_router, b_router, w_gate_up,
                 b_gate_up, w_down, b_down, g_final)

    xs = x_sample.reshape(bd, d)
    prev0 = cache_conv[l, :, 0, :]
    prev1 = cache_conv[l, :, 1, :]
    q_s, qlat_s, lat_s, kr_s, ycn_s, u_s = _inproj_sample(xs, prev0, prev1, w)
    olat = _decode(page_table, jnp.transpose(qlat_s, (1, 0, 2)), jnp.transpose(q_s, (1, 0, 2)),
                   lat_s.reshape(bd, 1, KV_LORA), kr_s.reshape(bd, 1, QK_ROPE), cache_ckv,
                   jnp.swapaxes(cache_krope, 2, 3))
    ya_s = _uv_project(jnp.transpose(olat, (1, 0, 2)), w["w_uv_heads"])
    h1_s, qm_s = _mixout(xs, ycn_s, ya_s, w, tm=bd)
    o_s = _cross_sample(qm_s.astype(F32).reshape(bd, MEM_HEADS, MEM_HEAD_DIM), cache_mem_k[l], cache_mem_v[l], tb=4)
    h2_s, xt_s, gate_s, _, _, _ = _router(h1_s, o_s.reshape(bd, d).astype(BF16), w, tm=bd)
    y_s = _moe(xt_s, gate_s, h2_s, w, tm=bd)

    q, k, v, lat_p, kr_p, ycn_p, conv_p = _inproj_prompt(x_prompt, w, tm=512)
    ya_p = _attention(q, k, v, tq=512)
    mk_p, mv_p = _memory_kv(mem_prompt, w["g_mem"], w["w_mk"], w["w_mv"])
    xp = x_prompt.reshape(n_p, d)
    h2_p, xt_p, route_p, gw_p, cnt_p = _post_prompt(xp, ycn_p.reshape(n_p, -1), ya_p, mk_p, mv_p, w, tm=512)
    y_p = _routed_moe(h2_p, xt_p, route_p, gw_p, cnt_p, w)

    mem_shape = (1, bp, MEM_TOKENS, MEM_HEADS, MEM_HEAD_DIM)
    return (y_p.reshape(bp, s_len, d), y_s.reshape(bd, 1, d),
            conv_p[None], lat_p[None], kr_p[None], mk_p.reshape(mem_shape), mv_p.reshape(mem_shape),
            jnp.stack([prev1, u_s], axis=1)[None], lat_s.reshape(1, bd, 1, KV_LORA),
            kr_s.reshape(1, bd, 1, QK_ROPE))
```

```python
import functools

import jax
import jax.numpy as jnp
from jax import lax
from jax.experimental import pallas as pl
from jax.experimental.pallas import tpu as pltpu
from jax.experimental.pallas import tpu_sc as plsc

D_MODEL = 1024
CONV_WIDTH = 512
CONV_K = 3
MLA_HEADS = 8
QK_NOPE = 64
QK_ROPE = 32
V_HEAD = 64
Q_LORA = 384
KV_LORA = 256
ROPE_THETA = 10000.0
PAGE_SIZE = 128
MEM_TOKENS = 256
MEM_HEADS = 4
MEM_HEAD_DIM = D_MODEL // MEM_HEADS
N_EXPERTS = 32
TOP_K = 4
D_FF = D_MODEL
SWIGLU_LIMIT = 7.0
SWIGLU_ALPHA = 1.702
NORM_EPS = 1e-6

LANES = 128
HALF_ROPE = QK_ROPE // 2
IN_WIDTH = 3 * CONV_WIDTH + Q_LORA + KV_LORA + QK_ROPE
IN_WIDTH_PAD = 3 * CONV_WIDTH + Q_LORA + KV_LORA + LANES
OFF_CQ = 3 * CONV_WIDTH
OFF_CKV = OFF_CQ + Q_LORA
OFF_KR = OFF_CKV + KV_LORA
MLA_SCALE = (QK_NOPE + QK_ROPE) ** -0.5
MEM_SCALE = MEM_HEAD_DIM ** -0.5
LOG2_E = 1.4426950408889634
VMEM_LIMIT = 48 * 1024 * 1024
ATTN_HEADS_PER_STEP = 4
PAGES_PER_STEP = 64
DECODE_SLOTS = 3

BF16 = jnp.bfloat16
F32 = jnp.float32
NT_DIMS = (((1,), (1,)), ((), ()))


def _params(*sem):
    return pltpu.CompilerParams(dimension_semantics=sem, vmem_limit_bytes=VMEM_LIMIT)


def _rms(x, g):
    return x * lax.rsqrt(jnp.mean(x * x, axis=-1, keepdims=True) + NORM_EPS) * g


def _dot(a, b):
    return jnp.dot(a, b, preferred_element_type=F32)


def _dot_nt(a, b):
    return lax.dot_general(a, b, NT_DIMS, preferred_element_type=F32)


def _rope_group(x, rc, ra, rb):
    return x * rc + pltpu.roll(x, LANES - HALF_ROPE, 1) * ra + pltpu.roll(x, HALF_ROPE, 1) * rb


def _full(shape):
    return pl.BlockSpec(shape, lambda *_: (0,) * len(shape))


def _memkv_kernel(mem_ref, g_ref, wk_ref, wv_ref, k_ref, v_ref):
    m = _rms(mem_ref[0], g_ref[...]).astype(BF16)
    k_ref[0] = _dot(m, wk_ref[...])
    v_ref[0] = _dot(m, wv_ref[...])


def _memory_kv(mem, g_mem, w_mk, w_mv):
    b, n, d = mem.shape
    blk = pl.BlockSpec((1, n, d), lambda i: (i, 0, 0))
    return pl.pallas_call(
        _memkv_kernel,
        grid=(b,),
        in_specs=[blk, _full((1, d)), _full((d, d)), _full((d, d))],
        out_specs=[blk, blk],
        out_shape=[jax.ShapeDtypeStruct((b, n, d), F32)] * 2,
        compiler_params=_params("arbitrary"),
        name="memory_kv",
    )(mem, g_mem, w_mk, w_mv)


def _inproj_common(x, gmix, win, gq, wuq, gkv, rc, ra, rb):
    a = _rms(x, gmix).astype(BF16)
    z = _dot(a, win)
    b_g = z[:, 0:CONV_WIDTH]
    u = z[:, CONV_WIDTH:2 * CONV_WIDTH] * z[:, 2 * CONV_WIDTH:3 * CONV_WIDTH]
    cq = _rms(z[:, OFF_CQ:OFF_CKV], gq).astype(BF16)
    q = _dot(cq, wuq)
    q_heads = [_rope_group(q[:, h * LANES:(h + 1) * LANES], rc, ra, rb) for h in range(MLA_HEADS)]
    lat = _rms(z[:, OFF_CKV:OFF_KR], gkv)
    kr = _rope_group(z[:, OFF_KR:OFF_KR + LANES], rc, ra, rb)
    return b_g, u, q_heads, lat, kr


def _inproj_prompt_kernel(x_ref, gmix_ref, win_ref, wconv_ref, gq_ref, wuq_ref, gkv_ref, wuk_ref, wuv_ref,
                          gco_ref, rc_ref, ra_ref, rb_ref,
                          q_ref, k_ref, v_ref, lat_ref, kr_ref, ycn_ref, conv_ref, ubuf, *, tm):
    j = pl.program_id(1)
    b_g, u, q_heads, lat, kr = _inproj_common(
        x_ref[0], gmix_ref[...], win_ref[...], gq_ref[...], wuq_ref[...], gkv_ref[...],
        rc_ref[...], ra_ref[...], rb_ref[...])
    for h in range(MLA_HEADS):
        q_ref[0, h] = (q_heads[h] * (MLA_SCALE * LOG2_E)).astype(BF16)
    lat_ref[0] = lat
    kr_ref[0] = kr[:, QK_NOPE:QK_NOPE + QK_ROPE]
    lat_b = lat.astype(BF16)
    kn = _dot(lat_b, wuk_ref[...])
    v_t = _dot_nt(wuv_ref[...], lat_b)
    for h in range(MLA_HEADS):
        k_ref[0, h] = (kn[:, h * LANES:(h + 1) * LANES] + kr).astype(BF16)
        v_ref[0, h, 0] = v_t[h * V_HEAD:(h + 1) * V_HEAD, :].astype(BF16)

    @pl.when(j == 0)
    def _():
        ubuf[0:8, :] = jnp.zeros((8, CONV_WIDTH), F32)

    ubuf[8:8 + tm, :] = u
    u1 = ubuf[7:7 + tm, :]
    u2 = ubuf[6:6 + tm, :]
    wc = wconv_ref[...]
    yc = b_g * (wc[0:1, :] * u2 + wc[1:2, :] * u1 + wc[2:3, :] * u)
    ycn_ref[0] = _rms(yc, gco_ref[...]).astype(BF16)
    ubuf[0:8, :] = ubuf[tm:tm + 8, :]
    conv_ref[0] = u[tm - (CONV_K - 1):tm, :]


def _inproj_prompt(x, w, tm):
    b, s, d = x.shape
    h = MLA_HEADS
    tok = lambda n: pl.BlockSpec((1, tm, n), lambda i, j: (i, j, 0))
    head = pl.BlockSpec((1, h, tm, LANES), lambda i, j: (i, 0, j, 0))
    rope = pl.BlockSpec((tm, LANES), lambda i, j: (j, 0))
    qkv_shape = jax.ShapeDtypeStruct((b, h, s, LANES), BF16)
    return pl.pallas_call(
        functools.partial(_inproj_prompt_kernel, tm=tm),
        grid=(b, s // tm),
        in_specs=[tok(d), _full((1, d)), _full((d, IN_WIDTH_PAD)), _full((CONV_K, CONV_WIDTH)),
                  _full((1, Q_LORA)), _full((Q_LORA, h * LANES)), _full((1, KV_LORA)),
                  _full((KV_LORA, h * LANES)), _full((h * V_HEAD, KV_LORA)), _full((1, CONV_WIDTH)),
                  rope, rope, rope],
        out_specs=[head, head, pl.BlockSpec((1, h, 1, V_HEAD, tm), lambda i, j: (i, 0, j, 0, 0)),
                   tok(KV_LORA), tok(QK_ROPE), tok(CONV_WIDTH),
                   pl.BlockSpec((1, CONV_K - 1, CONV_WIDTH), lambda i, j: (i, 0, 0))],
        out_shape=[qkv_shape, qkv_shape, jax.ShapeDtypeStruct((b, h, s // tm, V_HEAD, tm), BF16),
                   jax.ShapeDtypeStruct((b, s, KV_LORA), F32),
                   jax.ShapeDtypeStruct((b, s, QK_ROPE), F32),
                   jax.ShapeDtypeStruct((b, s, CONV_WIDTH), BF16),
                   jax.ShapeDtypeStruct((b, CONV_K - 1, CONV_WIDTH), F32)],
        scratch_shapes=[pltpu.VMEM((tm + 8, CONV_WIDTH), F32)],
        compiler_params=_params("arbitrary", "arbitrary"),
        name="inproj_prompt",
    )(x, w["g_mix"], w["w_in"], w["w_conv"], w["g_q"], w["w_uq"], w["g_kv"], w["w_uk"], w["w_uv"],
      w["g_conv_out"], *w["rope_prompt"])


def _inproj_sample_kernel(x_ref, gmix_ref, win_ref, wconv_ref, gq_ref, wuq_ref, gkv_ref, wabs_ref,
                          gco_ref, rc_ref, ra_ref, rb_ref, p0_ref, p1_ref,
                          q_ref, qlat_ref, lat_ref, kr_ref, ycn_ref, u_ref):
    b_g, u, q_heads, lat, kr = _inproj_common(
        x_ref[...], gmix_ref[...], win_ref[...], gq_ref[...], wuq_ref[...], gkv_ref[...],
        rc_ref[...], ra_ref[...], rb_ref[...])
    for h in range(MLA_HEADS):
        qh = q_heads[h].astype(BF16)
        q_ref[h] = qh
        qlat_ref[h] = _dot(qh, wabs_ref[h]).astype(BF16)
    lat_ref[...] = lat
    kr_ref[...] = kr[:, QK_NOPE:QK_NOPE + QK_ROPE]
    wc = wconv_ref[...]
    yc = b_g * (wc[0:1, :] * p0_ref[...] + wc[1:2, :] * p1_ref[...] + wc[2:3, :] * u)
    ycn_ref[...] = _rms(yc, gco_ref[...]).astype(BF16)
    u_ref[...] = u


def _inproj_sample(x, prev0, prev1, w):
    n, d = x.shape
    h = MLA_HEADS
    return pl.pallas_call(
        _inproj_sample_kernel,
        grid=(1,),
        in_specs=[_full((n, d)), _full((1, d)), _full((d, IN_WIDTH_PAD)), _full((CONV_K, CONV_WIDTH)),
                  _full((1, Q_LORA)), _full((Q_LORA, h * LANES)), _full((1, KV_LORA)),
                  _full((h, LANES, KV_LORA)), _full((1, CONV_WIDTH)),
                  _full((n, LANES)), _full((n, LANES)), _full((n, LANES)),
                  _full((n, CONV_WIDTH)), _full((n, CONV_WIDTH))],
        out_specs=[_full((h, n, LANES)), _full((h, n, KV_LORA)), _full((n, KV_LORA)), _full((n, QK_ROPE)),
                   _full((n, CONV_WIDTH)), _full((n, CONV_WIDTH))],
        out_shape=[jax.ShapeDtypeStruct((h, n, LANES), BF16),
                   jax.ShapeDtypeStruct((h, n, KV_LORA), BF16),
                   jax.ShapeDtypeStruct((n, KV_LORA), F32),
                   jax.ShapeDtypeStruct((n, QK_ROPE), F32),
                   jax.ShapeDtypeStruct((n, CONV_WIDTH), BF16),
                   jax.ShapeDtypeStruct((n, CONV_WIDTH), F32)],
        compiler_params=_params("arbitrary"),
        name="inproj_sample",
    )(x, w["g_mix"], w["w_in"], w["w_conv"], w["g_q"], w["w_uq"], w["g_kv"], w["w_abs"],
      w["g_conv_out"], *w["rope_sample"], prev0, prev1)


def _softmax_step(s, v, m_sc, l_sc, acc_sc):
    m_prev = m_sc[...]
    m_next = jnp.maximum(m_prev, jnp.max(s, axis=1, keepdims=True))
    p = jnp.exp(s - m_next[:, 0:1])
    alpha = jnp.exp(m_prev - m_next)
    l_sc[...] = alpha * l_sc[...] + jnp.sum(p, axis=1, keepdims=True)
    pv = _dot(p.astype(BF16), v)
    acc_sc[...] = acc_sc[...] * alpha[:, 0:1] + pv
    m_sc[...] = m_next


def _attn_kernel(q_ref, k_ref, v_ref, o_ref, m_sc, l_sc, acc_sc, s0_sc, s1_sc, *, tq):
    qi = pl.program_id(2)
    key = lax.broadcasted_iota(jnp.int32, (tq, tq), 0)
    qry = lax.broadcasted_iota(jnp.int32, (tq, tq), 1)
    m_sc[...] = jnp.full(m_sc.shape, -jnp.inf, F32)
    l_sc[...] = jnp.zeros(l_sc.shape, F32)
    acc_sc[...] = jnp.zeros(acc_sc.shape, F32)

    def scores(j, buf):
        for hh in range(ATTN_HEADS_PER_STEP):
            k = k_ref[0, hh, pl.ds(pl.multiple_of(j * tq, tq), tq), :]
            buf[hh] = _dot_nt(k, q_ref[0, hh])

    def consume(j, buf, masked):
        for hh in range(ATTN_HEADS_PER_STEP):
            s = buf[hh]
            if masked:
                s = jnp.where(key <= qry, s, -jnp.inf)
            m_prev = m_sc[hh]
            m_next = jnp.maximum(m_prev, jnp.max(s, axis=0, keepdims=True))
            p = jnp.exp2(s - m_next)
            alpha = jnp.exp2(m_prev - m_next)
            l_sc[hh] = alpha * l_sc[hh] + jnp.sum(p, axis=0, keepdims=True)
            acc_sc[hh] = acc_sc[hh] * alpha + _dot(v_ref[0, hh, j], p.astype(BF16))
            m_sc[hh] = m_next

    def pair(jj, carry):
        j = 2 * jj
        scores(j + 1, s1_sc)
        consume(j, s0_sc, False)
        scores(j + 2, s0_sc)
        consume(j + 1, s1_sc, False)
        return carry

    scores(0, s0_sc)
    lax.fori_loop(0, qi // 2, pair, 0)

    @pl.when(qi % 2 == 0)
    def _():
        consume(qi, s0_sc, True)

    @pl.when(qi % 2 == 1)
    def _():
        scores(qi, s1_sc)
        consume(qi - 1, s0_sc, False)
        consume(qi, s1_sc, True)

    o_ref[0] = jnp.concatenate([acc_sc[hh] / l_sc[hh] for hh in range(ATTN_HEADS_PER_STEP)], axis=0)


def _attention(q, k, v_t, tq):
    b, h, s, _ = q.shape
    hps = ATTN_HEADS_PER_STEP
    qspec = pl.BlockSpec((1, hps, tq, LANES), lambda i, p, j: (i, p, j, 0))
    kspec = pl.BlockSpec((1, hps, s, LANES), lambda i, p, j: (i, p, 0, 0))
    vspec = pl.BlockSpec((1, hps, s // tq, V_HEAD, tq), lambda i, p, j: (i, p, 0, 0, 0))
    return pl.pallas_call(
        functools.partial(_attn_kernel, tq=tq),
        grid=(b, h // hps, s // tq),
        in_specs=[qspec, kspec, vspec],
        out_specs=pl.BlockSpec((1, hps * V_HEAD, tq), lambda i, p, j: (i, p, j)),
        out_shape=jax.ShapeDtypeStruct((b, h * V_HEAD, s), F32),
        scratch_shapes=[pltpu.VMEM((hps, 1, tq), F32), pltpu.VMEM((hps, 1, tq), F32),
                        pltpu.VMEM((hps, V_HEAD, tq), F32),
                        pltpu.VMEM((hps, tq, tq), F32), pltpu.VMEM((hps, tq, tq), F32)],
        compiler_params=_params("arbitrary", "arbitrary", "arbitrary"),
        name="mla_prompt_attention",
    )(q, k, v_t)


def _decode_kernel(pt_ref, qlat_ref, q_ref, lat_ref, krn_ref, ckv_hbm, kr_hbm, o_ref,
                   ckv_buf, kr_buf, sems, m_sc, l_sc, acc_sc, *, n_steps, n_seqs):
    npg = PAGES_PER_STEP
    ahead = DECODE_SLOTS - 1
    b = pl.program_id(0)
    step = pl.program_id(1)
    t = b * n_steps + step
    slot = t % DECODE_SLOTS

    def start_pages(seq, st, sl):
        for i in range(npg):
            page = pt_ref[seq, st * npg + i]
            pltpu.make_async_copy(ckv_hbm.at[0, page], ckv_buf.at[sl, i], sems.at[0, sl]).start(priority=i % 2)
            pltpu.make_async_copy(kr_hbm.at[0, page], kr_buf.at[sl, i], sems.at[1, sl]).start(priority=(i + 1) % 2)

    def wait_pages(sl):
        pltpu.make_async_copy(ckv_hbm.at[0, pl.ds(0, npg)], ckv_buf.at[sl], sems.at[0, sl]).wait()
        pltpu.make_async_copy(kr_hbm.at[0, pl.ds(0, npg)], kr_buf.at[sl], sems.at[1, sl]).wait()

    last = n_seqs * n_steps - 1

    @pl.when(t == 0)
    def _():
        for k in range(ahead):
            start_pages(k // n_steps, k % n_steps, k)

    @pl.when(step == 0)
    def _():
        m_sc[...] = jnp.full(m_sc.shape, -jnp.inf, F32)
        l_sc[...] = jnp.zeros(l_sc.shape, F32)
        acc_sc[...] = jnp.zeros(acc_sc.shape, F32)

    ql = qlat_ref[0]
    qr = q_ref[0][:, QK_NOPE:QK_NOPE + QK_ROPE]
    wait_pages(slot)
    nxt = jnp.minimum(t + ahead, last)
    start_pages(nxt // n_steps, nxt % n_steps, (t + ahead) % DECODE_SLOTS)
    ckv = ckv_buf[slot].reshape(npg * PAGE_SIZE, KV_LORA).astype(BF16)
    kr_t = jnp.concatenate([kr_buf[slot, i] for i in range(npg)], axis=1).astype(BF16)
    s = (_dot_nt(ql, ckv) + _dot(qr, kr_t)) * MLA_SCALE
    _softmax_step(s, ckv, m_sc, l_sc, acc_sc)

    @pl.when(step == n_steps - 1)
    def _():
        lat = lat_ref[0]
        s_new = (jnp.sum(ql.astype(F32) * lat, axis=1, keepdims=True)
                 + jnp.sum(qr.astype(F32) * krn_ref[0], axis=1, keepdims=True)) * MLA_SCALE
        m_prev = m_sc[...]
        m_next = jnp.maximum(m_prev, s_new)
        p_new = jnp.exp(s_new - m_next[:, 0:1])
        alpha = jnp.exp(m_prev - m_next)
        l_fin = alpha * l_sc[...] + p_new
        acc = acc_sc[...] * alpha[:, 0:1] + p_new * lat
        o_ref[0] = acc / l_fin[:, 0:1]

    @pl.when(t == last)
    def _():
        for k in range(1, DECODE_SLOTS):
            wait_pages((t + k) % DECODE_SLOTS)


def _decode(page_table, qlat, q, lat, krn, cache_ckv, cache_krope):
    bd, n_pages = page_table.shape
    npg = PAGES_PER_STEP
    n_steps = n_pages // npg
    assert n_steps * npg == n_pages and bd * n_steps >= DECODE_SLOTS
    h = MLA_HEADS
    per_b =lambda *shape: pl.BlockSpec((1,) + shape, lambda b, s, pt: (b,) + (0,) * len(shape))
    hbm = pl.BlockSpec(memory_space=pl.ANY)
    grid_spec = pltpu.PrefetchScalarGridSpec(
        num_scalar_prefetch=1,
        grid=(bd, n_steps),
        in_specs=[per_b(h, KV_LORA), per_b(h, LANES), per_b(1, KV_LORA), per_b(1, QK_ROPE), hbm, hbm],
        out_specs=per_b(h, KV_LORA),
        scratch_shapes=[pltpu.VMEM((DECODE_SLOTS, npg, PAGE_SIZE, KV_LORA), F32),
                        pltpu.VMEM((DECODE_SLOTS, npg, QK_ROPE, PAGE_SIZE), F32),
                        pltpu.SemaphoreType.DMA((2, DECODE_SLOTS)),
                        pltpu.VMEM((h, LANES), F32), pltpu.VMEM((h, LANES), F32), pltpu.VMEM((h, KV_LORA), F32)],
    )
    return pl.pallas_call(
        functools.partial(_decode_kernel, n_steps=n_steps, n_seqs=bd),
        grid_spec=grid_spec,
        out_shape=jax.ShapeDtypeStruct((bd, h, KV_LORA), F32),
        compiler_params=_params("arbitrary", "arbitrary"),
        name="mla_decode",
    )(page_table, qlat, q, lat, krn, cache_ckv, cache_krope)


def _uv_kernel(olat_ref, wuv_ref, ya_ref):
    acc = None
    for h in range(MLA_HEADS):
        part = _dot(olat_ref[h].astype(BF16), wuv_ref[h])
        acc = part if acc is None else acc + part
    ya_ref[...] = acc


def _uv_project(olat, wuv_heads):
    h, n, c = olat.shape
    width = MLA_HEADS * V_HEAD
    return pl.pallas_call(
        _uv_kernel,
        grid=(1,),
        in_specs=[_full((h, n, c)), _full((h, c, width))],
        out_specs=_full((n, width)),
        out_shape=jax.ShapeDtypeStruct((n, width), F32),
        compiler_params=_params("arbitrary"),
        name="decode_uv",
    )(olat, wuv_heads)


def _mixout_math(x, ycn, ya, ga, wout, gcross, wmq):
    yan = _rms(ya, ga).astype(BF16)
    mixed = jnp.concatenate([ycn, yan], axis=1)
    h1 = x + _dot(mixed, wout)
    return h1, _dot(_rms(h1, gcross).astype(BF16), wmq).astype(BF16)


def _mixout_kernel(x_ref, ycn_ref, ya_ref, ga_ref, wout_ref, gcross_ref, wmq_ref, h1_ref, qm_ref):
    h1_ref[...], qm_ref[...] = _mixout_math(x_ref[...], ycn_ref[...], ya_ref[...], ga_ref[...], wout_ref[...],
                                            gcross_ref[...], wmq_ref[...])


def _mixout(x, ycn, ya, w, tm):
    n, d = x.shape
    width = MLA_HEADS * V_HEAD
    mix = CONV_WIDTH + width
    tok = lambda width: pl.BlockSpec((tm, width), lambda i: (i, 0))
    return pl.pallas_call(
        _mixout_kernel,
        grid=(n // tm,),
        in_specs=[tok(d), tok(CONV_WIDTH), tok(width), _full((1, width)),
                  _full((mix, d)), _full((1, d)), _full((d, d))],
        out_specs=[tok(d), tok(d)],
        out_shape=[jax.ShapeDtypeStruct((n, d), F32), jax.ShapeDtypeStruct((n, d), BF16)],
        compiler_params=_params("arbitrary"),
        name="mix_out",
    )(x, ycn, ya, w["g_attn_out"], w["w_out"], w["g_cross"], w["w_mq"])


def _cross_math(q, k_ref, v_ref):
    outs = []
    for h in range(MEM_HEADS):
        sl = slice(h * MEM_HEAD_DIM, (h + 1) * MEM_HEAD_DIM)
        s = _dot_nt(q[:, sl], k_ref[0, :, sl].astype(BF16)) * MEM_SCALE
        e = jnp.exp(s - jnp.max(s, axis=1, keepdims=True))
        p = e / jnp.sum(e, axis=1, keepdims=True)
        outs.append(_dot(p.astype(BF16), v_ref[0, :, sl].astype(BF16)))
    return jnp.concatenate(outs, axis=1).astype(BF16)


def _cross_sample_kernel(q_ref, k_ref, v_ref, o_ref, *, tb):
    for t in range(tb):
        s = jnp.sum(k_ref[t] * q_ref[t][None], axis=2, keepdims=True) * MEM_SCALE
        e = jnp.exp(s - jnp.max(s, axis=0, keepdims=True))
        p = e / jnp.sum(e, axis=0, keepdims=True)
        o_ref[t] = jnp.sum(p * v_ref[t], axis=0)


def _cross_sample(qm, mk, mv, tb):
    n = qm.shape[0]
    tok = pl.BlockSpec((tb, MEM_HEADS, MEM_HEAD_DIM), lambda i: (i, 0, 0))
    mem = pl.BlockSpec((tb, MEM_TOKENS, MEM_HEADS, MEM_HEAD_DIM), lambda i: (i, 0, 0, 0))
    return pl.pallas_call(
        functools.partial(_cross_sample_kernel, tb=tb),
        grid=(n // tb,),
        in_specs=[tok, mem, mem],
        out_specs=tok,
        out_shape=jax.ShapeDtypeStruct((n, MEM_HEADS, MEM_HEAD_DIM), F32),
        compiler_params=_params("arbitrary"),
        name="cross_sample",
    )(qm, mk, mv)


def _router_math(h1, o, wmo, gffn, wr, br, cnt_prev):
    h2 = h1 + _dot(o, wmo)
    xt = _rms(h2, gffn).astype(BF16)
    logits = _dot(xt, wr) + br
    tm = logits.shape[0]
    lane = lax.broadcasted_iota(jnp.int32, logits.shape, 1)
    logits = jnp.where(lane < N_EXPERTS, logits, -jnp.inf)
    work = logits
    sel = lane < 0
    picks = []
    for k in range(TOP_K):
        m = jnp.max(work, axis=1, keepdims=True)
        idx = jnp.min(jnp.where(work == m, lane, LANES), axis=1, keepdims=True)
        hit = lane == idx
        sel = jnp.logical_or(sel, hit)
        work = jnp.where(hit, -jnp.inf, work)
        picks.append((m, idx, hit))
    top = picks[0][0]
    e = jnp.where(sel, jnp.exp(logits - top), 0.0)
    denom = jnp.sum(e, axis=1, keepdims=True)

    sel_f = sel.astype(F32)
    earlier = (lax.broadcasted_iota(jnp.int32, (tm, tm), 0) > lax.broadcasted_iota(jnp.int32, (tm, tm), 1))
    rank_all = _dot(earlier.astype(BF16), sel_f.astype(BF16)) + cnt_prev
    route = jnp.zeros(logits.shape, jnp.int32)
    gw = jnp.zeros(logits.shape, F32)
    for k, (m, idx, hit) in enumerate(picks):
        rank = jnp.sum(jnp.where(hit, rank_all, 0.0), axis=1, keepdims=True).astype(jnp.int32)
        route = jnp.where(lane == k, idx, route)
        route = jnp.where(lane == TOP_K + k, rank, route)
        gw = jnp.where(lane == k, jnp.exp(m - top) / denom, gw)
    cnt = cnt_prev + jnp.sum(sel_f, axis=0, keepdims=True)
    return h2, _pack_rows(xt.astype(F32)), e / denom, route, gw, cnt


def _router_kernel(h1_ref, o_ref, wmo_ref, gffn_ref, wr_ref, br_ref,
                   h2_ref, xt_ref, gate_ref, route_ref, gw_ref, cnt_ref, cnt_sc):
    @pl.when(pl.program_id(0) == 0)
    def _():
        cnt_sc[...] = jnp.zeros(cnt_sc.shape, F32)

    h2_ref[...], xt_ref[...], gate_ref[...], route_ref[...], gw_ref[...], cnt = _router_math(
        h1_ref[...], o_ref[...], wmo_ref[...], gffn_ref[...], wr_ref[...], br_ref[...], cnt_sc[...])
    cnt_sc[...] = cnt
    cnt_ref[...] = cnt


def _post_prompt_kernel(x_ref, ycn_ref, yat_ref, k_ref, v_ref, ga_ref, wout_ref, gcross_ref, wmq_ref,
                        wmo_ref, gffn_ref, wr_ref, br_ref, h2_ref, xt_ref, route_ref, gw_ref, cnt_ref, cnt_sc):
    @pl.when(pl.program_id(0) == 0)
    def _():
        cnt_sc[...] = jnp.zeros(cnt_sc.shape, F32)

    h1, qm = _mixout_math(x_ref[...], ycn_ref[...], yat_ref[0].T, ga_ref[...], wout_ref[...], gcross_ref[...],
                          wmq_ref[...])
    o = _cross_math(qm, k_ref, v_ref)
    h2_ref[...], xt_ref[...], _, route_ref[...], gw_ref[...], cnt = _router_math(
        h1, o, wmo_ref[...], gffn_ref[...], wr_ref[...], br_ref[...], cnt_sc[...])
    cnt_sc[...] = cnt
    cnt_ref[...] = cnt


def _post_prompt(x, ycn, ya_t, mk, mv, w, tm):
    n, d = x.shape
    width = MLA_HEADS * V_HEAD
    tiles = ya_t.shape[2] // tm
    tok = lambda width: pl.BlockSpec((tm, width), lambda i: (i, 0))
    mem = pl.BlockSpec((1, MEM_TOKENS, d), lambda i: (i // tiles, 0, 0))
    return pl.pallas_call(
        _post_prompt_kernel,
        grid=(n // tm,),
        in_specs=[tok(d), tok(CONV_WIDTH), pl.BlockSpec((1, width, tm), lambda i: (i // tiles, 0, i % tiles)),
                  mem, mem, _full((1, width)), _full((CONV_WIDTH + width, d)), _full((1, d)), _full((d, d)),
                  _full((d, d)), _full((1, d)), _full((d, LANES)), _full((1, LANES))],
        out_specs=[tok(d), tok(d // 2), tok(LANES), tok(LANES), _full((1, LANES))],
        out_shape=[jax.ShapeDtypeStruct((n, d), F32), jax.ShapeDtypeStruct((n, d // 2), jnp.uint32),
                   jax.ShapeDtypeStruct((n, LANES), jnp.int32), jax.ShapeDtypeStruct((n, LANES), F32),
                   jax.ShapeDtypeStruct((1, LANES), F32)],
        scratch_shapes=[pltpu.VMEM((1, LANES), F32)],
        compiler_params=_params("arbitrary"),
        name="post_prompt",
    )(x, ycn, ya_t, mk, mv, w["g_attn_out"], w["w_out"], w["g_cross"], w["w_mq"], w["w_mo"], w["g_ffn"],
      w["w_router"], w["b_router"])


def _router(h1, o, w, tm):
    n, d = h1.shape
    tok = lambda width: pl.BlockSpec((tm, width), lambda i: (i, 0))
    return pl.pallas_call(
        _router_kernel,
        grid=(n // tm,),
        in_specs=[tok(d), tok(d), _full((d, d)), _full((1, d)), _full((d, LANES)), _full((1, LANES))],
        out_specs=[tok(d), tok(d // 2), tok(LANES), tok(LANES), tok(LANES), _full((1, LANES))],
        out_shape=[jax.ShapeDtypeStruct((n, d), F32), jax.ShapeDtypeStruct((n, d // 2), jnp.uint32),
                   jax.ShapeDtypeStruct((n, LANES), F32), jax.ShapeDtypeStruct((n, LANES), jnp.int32),
                   jax.ShapeDtypeStruct((n, LANES), F32), jax.ShapeDtypeStruct((1, LANES), F32)],
        scratch_shapes=[pltpu.VMEM((1, LANES), F32)],
        compiler_params=_params("arbitrary"),
        name="router",
    )(h1, o, w["w_mo"], w["g_ffn"], w["w_router"], w["b_router"])


SC_CORES = 2
SC_SUBCORES = 16
SC_WORKERS = SC_CORES * SC_SUBCORES
SC_CHUNK = 64
EXPERT_ROWS = 512
EXPERTS_VMEM_LIMIT = 56 * 1024 * 1024


def _sc_gather(table, idx):
    b = idx.shape[0]
    d = table.shape[1]
    per_worker = b // SC_WORKERS
    n_chunks = per_worker // SC_CHUNK
    assert per_worker * SC_WORKERS == b and n_chunks * SC_CHUNK == per_worker
    mesh = plsc.VectorSubcoreMesh(core_axis_name="c", subcore_axis_name="s")

    assert n_chunks % 2 == 0
    row_buf = pltpu.VMEM((SC_CHUNK, d), table.dtype)

    @functools.partial(
        pl.kernel, mesh=mesh,
        out_type=jax.ShapeDtypeStruct((b, d), table.dtype),
        scratch_types=[pltpu.VMEM((per_worker,), jnp.int32), row_buf, row_buf] + [pltpu.SemaphoreType.DMA] * 4,
    )
    def gather(table_hbm, idx_hbm, out_hbm, idx_v, rows0, rows1, g0, g1, w0, w1):
        wid = lax.axis_index("s") * SC_CORES + lax.axis_index("c")
        base = pl.multiple_of(wid * per_worker, 8)
        rows, gsem, wsem = (rows0, rows1), (g0, g1), (w0, w1)
        pltpu.sync_copy(idx_hbm.at[pl.ds(base, per_worker)], idx_v)

        def fetch(c, buf):
            ids = idx_v.at[pl.ds(pl.multiple_of(c * SC_CHUNK, 8), SC_CHUNK)]
            return pltpu.make_async_copy(table_hbm.at[ids], rows[buf], gsem[buf])

        def flush(c, buf):
            dst = out_hbm.at[pl.ds(pl.multiple_of(base + c * SC_CHUNK, 8), SC_CHUNK)]
            return pltpu.make_async_copy(rows[buf], dst, wsem[buf])

        fetch(0, 0).start()

        @pl.loop(0, n_chunks, step=2)
        def _(c0):
            for buf in (0, 1):
                c = c0 + buf
                fetch(c, buf).wait()

                @pl.when(c + 1 < n_chunks)
                def _():
                    @pl.when(c >= 1)
                    def _():
                        flush(c - 1, 1 - buf).wait()

                    fetch(c + 1, 1 - buf).start()

                flush(c, buf).start()

        flush(n_chunks - 2, 0).wait()
        flush(n_chunks - 1, 1).wait()

    return gather(table, idx)


def _sc_scatter_rows(x, slot_rows, n_out):
    n, d = x.shape
    per_worker = n // SC_WORKERS
    n_chunks = per_worker // SC_CHUNK
    assert per_worker * SC_WORKERS == n and n_chunks * SC_CHUNK == per_worker and n_chunks % 2 == 0
    assert slot_rows.shape == (n // SC_CHUNK * TOP_K, SC_CHUNK)
    idx_rows = n_chunks * TOP_K
    mesh = plsc.VectorSubcoreMesh(core_axis_name="c", subcore_axis_name="s")
    row_buf = pltpu.VMEM((SC_CHUNK, d), x.dtype)

    @functools.partial(
        pl.kernel, mesh=mesh,
        out_type=jax.ShapeDtypeStruct((n_out, d), x.dtype),
        scratch_types=[pltpu.VMEM((idx_rows, SC_CHUNK), jnp.int32), row_buf, row_buf]
        + [pltpu.SemaphoreType.DMA] * 4,
    )
    def scatter(x_hbm, idx_hbm, out_hbm, idx_v, rows0, rows1, r0, r1, s0, s1):
        wid = lax.axis_index("s") * SC_CORES + lax.axis_index("c")
        base = pl.multiple_of(wid * per_worker, 8)
        rows, rsem, ssem = (rows0, rows1), (r0, r1), (s0, s1)
        pltpu.sync_copy(idx_hbm.at[pl.ds(pl.multiple_of(wid * idx_rows, 8), idx_rows)], idx_v)

        def fetch(c, buf):
            src = x_hbm.at[pl.ds(pl.multiple_of(base + c * SC_CHUNK, 8), SC_CHUNK)]
            return pltpu.make_async_copy(src, rows[buf], rsem[buf])

        def spread(c, k, buf):
            return pltpu.make_async_copy(rows[buf], out_hbm.at[idx_v.at[c * TOP_K + k]], ssem[buf])

        fetch(0, 0).start()

        @pl.loop(0, n_chunks, step=2)
        def _(c0):
            for buf in (0, 1):
                c = c0 + buf
                fetch(c, buf).wait()

                @pl.when(c + 1 < n_chunks)
                def _():
                    @pl.when(c >= 1)
                    def _():
                        for k in range(TOP_K):
                            spread(c - 1, k, 1 - buf).wait()

                    fetch(c + 1, 1 - buf).start()

                for k in range(TOP_K):
                    spread(c, k, buf).start()

        for k in range(TOP_K):
            spread(n_chunks - 2, k, 0).wait()
        for k in range(TOP_K):
            spread(n_chunks - 1, k, 1).wait()

    return scatter(x, slot_rows)


def _swiglu_expert(x, wgu, bgu, wd, bd):
    gu = _dot(x, wgu) + bgu
    g = jnp.minimum(gu[:, :D_FF], SWIGLU_LIMIT)
    u = jnp.clip(gu[:, D_FF:], -SWIGLU_LIMIT, SWIGLU_LIMIT)
    hdn = (u + 1.0) * (g * (1.0 / (1.0 + jnp.exp(-SWIGLU_ALPHA * g))))
    return _dot(hdn.astype(BF16), wd) + bd


def _pack_rows(x):
    bits = lax.bitcast_convert_type(x, jnp.uint32)
    n = x.shape[1] // 2
    return bits[:, :n] | (bits[:, n:] >> 16)


def _unpack_rows_f32(p):
    hi = lax.bitcast_convert_type(p & jnp.uint32(0xFFFF0000), F32)
    lo = lax.bitcast_convert_type(p << 16, F32)
    return jnp.concatenate([hi, lo], axis=1)


def _unpack_rows(p):
    return _unpack_rows_f32(p).astype(BF16)


def _experts_kernel(te_ref, tb_ref, first_ref, used_ref, xs_ref, wgu_ref, bgu_ref, wd_ref, bd_ref, ys_ref,
                    wgu_sc, wd_sc):
    del te_ref, tb_ref
    t = pl.program_id(0)

    @pl.when(t < used_ref[0])
    def _():
        @pl.when(first_ref[t] == 1)
        def _():
            wgu_sc[...] = wgu_ref[0].astype(BF16)
            wd_sc[...] = wd_ref[0].astype(BF16)

        ys = _swiglu_expert(_unpack_rows(xs_ref[...]), wgu_sc[...], bgu_ref[0], wd_sc[...], bd_ref[0])
        ys_ref[...] = _pack_rows(ys.astype(BF16).astype(F32))


def _experts(tile_expert, tile_block, tile_first, used, xs, w):
    n_slots = xs.shape[0]
    d = D_MODEL
    r = EXPERT_ROWS
    rows = lambda width: pl.BlockSpec((r, width), lambda t, te, tb, tf, u: (tb[t], 0))
    per_e = lambda *shape: pl.BlockSpec((1,) + shape, lambda t, te, tb, tf, u: (te[t],) + (0,) * len(shape))
    grid_spec = pltpu.PrefetchScalarGridSpec(
        num_scalar_prefetch=4,
        grid=(n_slots // r,),
        in_specs=[rows(d // 2), per_e(d, 2 * D_FF), per_e(1, 2 * D_FF), per_e(D_FF, d), per_e(1, d)],
        out_specs=rows(d // 2),
        scratch_shapes=[pltpu.VMEM((d, 2 * D_FF), BF16), pltpu.VMEM((D_FF, d), BF16)],
    )
    return pl.pallas_call(
        _experts_kernel,
        grid_spec=grid_spec,
        out_shape=jax.ShapeDtypeStruct((n_slots, d // 2), jnp.uint32),
        compiler_params=pltpu.CompilerParams(dimension_semantics=("arbitrary",),
                                             vmem_limit_bytes=EXPERTS_VMEM_LIMIT),
        name="experts",
    )(tile_expert, tile_block, tile_first, used, xs, w["w_gate_up"], w["b_gate_up"], w["w_down"], w["b_down"])


def _combine_kernel(h2_ref, gw_ref, y0_ref, y1_ref, y2_ref, y3_ref, gfin_ref, y_ref):
    gw = gw_ref[...]
    moe = None
    for k, yk in enumerate((y0_ref, y1_ref, y2_ref, y3_ref)):
        part = gw[:, k:k + 1] * _unpack_rows_f32(yk[0])
        moe = part if moe is None else moe + part
    y_ref[...] = _rms(h2_ref[...] + moe, gfin_ref[...])


def _combine(h2, gw, ys4, w, tm):
    n, d = h2.shape
    tok = lambda width: pl.BlockSpec((tm, width), lambda i: (i, 0))
    part = lambda k: pl.BlockSpec((1, tm, d // 2), lambda i, k=k: (k, i, 0))
    return pl.pallas_call(
        _combine_kernel,
        grid=(n // tm,),
        in_specs=[tok(d), tok(LANES)] + [part(k) for k in range(TOP_K)] + [_full((1, d))],
        out_specs=tok(d),
        out_shape=jax.ShapeDtypeStruct((n, d), F32),
        compiler_params=_params("arbitrary"),
        name="moe_combine",
    )(h2, gw, *([ys4] * TOP_K), w["g_final"])


def _routed_moe(h2, xt, route, gw, counts, w):
    n, d = h2.shape
    r = EXPERT_ROWS
    n_tiles = (n * TOP_K) // r + N_EXPERTS
    n_slots = n_tiles * r
    cnt = counts[0, :N_EXPERTS].astype(jnp.int32)
    padded = ((cnt + r - 1) // r) * r
    ends = jnp.cumsum(padded)
    starts = ends - padded
    eid = route[:, 0:TOP_K]
    rank = route[:, TOP_K:2 * TOP_K]
    onehot = eid[:, :, None] == jnp.arange(N_EXPERTS, dtype=jnp.int32)[None, None, :]
    slot = jnp.sum(jnp.where(onehot, starts[None, None, :], 0), axis=-1) + rank
    used = (ends[-1] // r).astype(jnp.int32)
    tile_block = jnp.minimum(jnp.arange(n_tiles, dtype=jnp.int32), used - 1)
    tile_expert = jnp.minimum(
        jnp.sum((tile_block[:, None] >= (ends // r)[None, :]).astype(jnp.int32), axis=1), N_EXPERTS - 1)
    tile_first = jnp.concatenate([jnp.ones((1,), jnp.int32),
                                  (tile_expert[1:] != tile_expert[:-1]).astype(jnp.int32)])
    slot_rows = slot.reshape(n // SC_CHUNK, SC_CHUNK, TOP_K).transpose(0, 2, 1).reshape(-1, SC_CHUNK)
    xs = _sc_scatter_rows(xt, slot_rows, n_slots)
    ys = _experts(tile_expert, tile_block, tile_first, used.reshape(1), xs, w)
    ys4 = _sc_gather(ys, slot.T.reshape(-1)).reshape(TOP_K, n, d // 2)
    return _combine(h2, gw, ys4, w, tm=512)


def _moe_kernel(xt_ref, gate_ref, h2_ref, wgu_ref, bgu_ref, wd_ref, bd_ref, gfin_ref, y_ref, acc_sc):
    e = pl.program_id(1)

    @pl.when(e == 0)
    def _():
        acc_sc[...] = jnp.zeros(acc_sc.shape, F32)

    down = _swiglu_expert(_unpack_rows(xt_ref[...]), wgu_ref[0].astype(BF16), bgu_ref[0],
                          wd_ref[0].astype(BF16), bd_ref[0])
    gate = gate_ref[...]
    lane = lax.broadcasted_iota(jnp.int32, gate.shape, 1)
    ge = jnp.sum(jnp.where(lane == e, gate, 0.0), axis=1, keepdims=True)
    acc_sc[...] += ge * down

    @pl.when(e == N_EXPERTS - 1)
    def _():
        y_ref[...] = _rms(h2_ref[...] + acc_sc[...], gfin_ref[...])


def _moe(xt, gate, h2, w, tm):
    n, d = h2.shape
    tok = lambda width: pl.BlockSpec((tm, width), lambda i, e: (i, 0))
    return pl.pallas_call(
        _moe_kernel,
        grid=(n // tm, N_EXPERTS),
        in_specs=[tok(d // 2), tok(LANES), tok(d),
                  pl.BlockSpec((1, d, 2 * D_FF), lambda i, e: (e, 0, 0)),
                  pl.BlockSpec((1, 1, 2 * D_FF), lambda i, e: (e, 0, 0)),
                  pl.BlockSpec((1, D_FF, d), lambda i, e: (e, 0, 0)),
                  pl.BlockSpec((1, 1, d), lambda i, e: (e, 0, 0)),
                  _full((1, d))],
        out_specs=tok(d),
        out_shape=jax.ShapeDtypeStruct((n, d), F32),
        scratch_shapes=[pltpu.VMEM((tm, d), F32)],
        compiler_params=_params("arbitrary", "arbitrary"),
        name="moe",
    )(xt, gate, h2, w["w_gate_up"], w["b_gate_up"], w["w_down"], w["b_down"], w["g_final"])


def _rope_tables(pos):
    inv = ROPE_THETA ** (-jnp.arange(HALF_ROPE, dtype=F32) / HALF_ROPE)
    ang = pos.astype(F32)[:, None] * inv[None, :]
    cos, sin = jnp.cos(ang), jnp.sin(ang)
    n = pos.shape[0]
    pad = LANES - QK_NOPE - QK_ROPE
    rc = jnp.concatenate([jnp.ones((n, QK_NOPE), F32), cos, cos, jnp.zeros((n, pad), F32)], axis=1)
    ra = jnp.concatenate([jnp.zeros((n, QK_NOPE), F32), -sin, jnp.zeros((n, HALF_ROPE + pad), F32)], axis=1)
    rb = jnp.concatenate([jnp.zeros((n, QK_NOPE + HALF_ROPE), F32), sin, jnp.zeros((n, pad), F32)], axis=1)
    return rc, ra, rb


def _prepare(l, s_len, n_sample, past, g_mix, w_in, w_conv, g_q, w_uq, g_kv, w_uk, w_uv, g_conv_out, g_attn_out,
             w_out, g_cross, g_mem, w_mq, w_mk, w_mv, w_mo, g_ffn, w_router, b_router, w_gate_up, b_gate_up,
             w_down, b_down, g_final):
    h = MLA_HEADS
    row = lambda g: g.reshape(1, -1).astype(F32)
    head_pad = LANES - QK_NOPE - QK_ROPE
    win = w_in[l]
    win = jnp.concatenate([win[:, :OFF_KR], jnp.zeros((D_MODEL, QK_NOPE), F32), win[:, OFF_KR:],
                           jnp.zeros((D_MODEL, head_pad), F32)], axis=1)
    wuq = jnp.pad(w_uq[l].reshape(Q_LORA, h, QK_NOPE + QK_ROPE), ((0, 0), (0, 0), (0, head_pad)))
    wuk = jnp.pad(w_uk[l], ((0, 0), (0, 0), (0, LANES - QK_NOPE)))
    wabs = jnp.pad(jnp.transpose(w_uk[l], (1, 2, 0)), ((0, 0), (0, LANES - QK_NOPE), (0, 0)))
    eye = jnp.eye(h, dtype=F32)
    wuv_heads = jnp.einsum("chd,hg->hcgd", w_uv[l], eye).reshape(h, KV_LORA, h * V_HEAD)
    wr = jnp.pad(w_router[l], ((0, 0), (0, LANES - N_EXPERTS)))
    br = jnp.pad(b_router[l], (0, LANES - N_EXPERTS)).reshape(1, LANES)
    return {
        "g_mix": row(g_mix[l]), "w_in": win.astype(BF16), "w_conv": w_conv[l].astype(F32),
        "g_q": row(g_q[l]), "w_uq": wuq.reshape(Q_LORA, h * LANES).astype(BF16),
        "g_kv": row(g_kv[l]), "w_uk": wuk.reshape(KV_LORA, h * LANES).astype(BF16),
        "w_uv": w_uv[l].reshape(KV_LORA, h * V_HEAD).T.astype(BF16), "w_abs": wabs.astype(BF16),
        "w_uv_heads": wuv_heads.astype(BF16),
        "g_conv_out": row(g_conv_out[l]), "g_attn_out": row(g_attn_out[l]),
        "w_out": w_out[l].astype(BF16), "g_cross": row(g_cross[l]), "g_mem": row(g_mem[l]),
        "w_mq": w_mq[l].astype(BF16), "w_mk": w_mk[l].astype(BF16), "w_mv": w_mv[l].astype(BF16),
        "w_mo": w_mo[l].astype(BF16), "g_ffn": row(g_ffn[l]),
        "w_router": wr.astype(BF16), "b_router": br.astype(F32),
        "w_gate_up": w_gate_up[l], "b_gate_up": b_gate_up[l].reshape(N_EXPERTS, 1, 2 * D_FF),
        "w_down": w_down[l], "b_down": b_down[l].reshape(N_EXPERTS, 1, D_MODEL),
        "g_final": row(g_final),
        "rope_prompt": _rope_tables(jnp.arange(s_len)),
        "rope_sample": _rope_tables(jnp.full((n_sample,), past)),
    }


def kernel(x_prompt, x_sample, mem_prompt, cache_conv, cache_ckv, cache_krope, cache_mem_k, cache_mem_v, page_table, g_mix, w_in, w_conv, g_q, w_uq, g_kv, w_uk, w_uv, g_conv_out, g_attn_out, w_out, g_cross, g_mem, w_mq, w_mk, w_mv, w_mo, g_ffn, w_router, b_router, w_gate_up, b_gate_up, w_down, b_down, g_final):
    bp, s_len, d = x_prompt.shape
    bd, t_len, _ = x_sample.shape
    depth = g_mix.shape[0]
    assert depth == 1 and t_len == 1, "kernel is written for one layer and one decode token per sequence"
    n_pages = page_table.shape[1]
    past = n_pages * PAGE_SIZE
    n_p = bp * s_len
    l = 0
    w = _prepare(l, s_len, bd, past, g_mix, w_in, w_conv, g_q, w_uq, g_kv, w_uk, w_uv, g_conv_out, g_attn_out,
                 w_out, g_cross, g_mem, w_mq, w_mk, w_mv, w_mo, g_ffn, w_router, b_router, w_gate_up,
                 b_gate_up, w_down, b_down, g_final)

    xs = x_sample.reshape(bd, d)
    prev0 = cache_conv[l, :, 0, :]
    prev1 = cache_conv[l, :, 1, :]
    q_s, qlat_s, lat_s, kr_s, ycn_s, u_s = _inproj_sample(xs, prev0, prev1, w)
    olat = _decode(page_table, jnp.transpose(qlat_s, (1, 0, 2)), jnp.transpose(q_s, (1, 0, 2)),
                   lat_s.reshape(bd, 1, KV_LORA), kr_s.reshape(bd, 1, QK_ROPE), cache_ckv,
                   jnp.swapaxes(cache_krope, 2, 3))
    ya_s = _uv_project(jnp.transpose(olat, (1, 0, 2)), w["w_uv_heads"])
    h1_s, qm_s = _mixout(xs, ycn_s, ya_s, w, tm=bd)
    o_s = _cross_sample(qm_s.astype(F32).reshape(bd, MEM_HEADS, MEM_HEAD_DIM), cache_mem_k[l], cache_mem_v[l], tb=4)
    h2_s, xt_s, gate_s, _, _, _ = _router(h1_s, o_s.reshape(bd, d).astype(BF16), w, tm=bd)
    y_s = _moe(xt_s, gate_s, h2_s, w, tm=bd)

    q, k, v, lat_p, kr_p, ycn_p, conv_p = _inproj_prompt(x_prompt, w, tm=512)
    ya_p = _attention(q, k, v, tq=512)
    mk_p, mv_p = _memory_kv(mem_prompt, w["g_mem"], w["w_mk"], w["w_mv"])
    xp = x_prompt.reshape(n_p, d)
    h2_p, xt_p, route_p, gw_p, cnt_p = _post_prompt(xp, ycn_p.reshape(n_p, -1), ya_p, mk_p, mv_p, w, tm=512)
    y_p = _routed_moe(h2_p, xt_p, route_p, gw_p, cnt_p, w)

    mem_shape = (1, bp, MEM_TOKENS, MEM_HEADS, MEM_HEAD_DIM)
    return (y_p.reshape(bp, s_len, d), y_s.reshape(bd, 1, d),
            conv_p[None], lat_p[None], kr_p[None], mk_p.reshape(mem_shape), mv_p.reshape(mem_shape),
            jnp.stack([prev1, u_s], axis=1)[None], lat_s.reshape(1, bd, 1, KV_LORA),
            kr_s.reshape(1, bd, 1, QK_ROPE))
```
